```python
import math
import jax, jax.numpy as jnp
from jax import lax
import numpy as np

D_MODEL = 1024
BATCH = 16
SEQ = 4096
DEPTH = 4

GRID_W = 64
CTX_LEN = 256
N_MIXERS = 3
EPS = 1e-6

S5_GROUP = 16
S5_GROUPS = D_MODEL // S5_GROUP
S5_STATE = 64
S5_DT_MIN = 1e-3
S5_DT_MAX = 1e-1

SSD_D_INNER = 2 * D_MODEL
SSD_HEAD_DIM = 64
SSD_HEADS = SSD_D_INNER // SSD_HEAD_DIM
SSD_GROUPS = 8
SSD_STATE = 128
SSD_CONV = 5
SSD_CHUNK = 128
SSD_GN = SSD_GROUPS * SSD_STATE
SSD_IN_DIM = 2 * SSD_D_INNER + 2 * SSD_GN + 2 * SSD_HEADS

DA_HEADS = 8
DA_HEAD_DIM = D_MODEL // (2 * DA_HEADS)
DA_WIDTH = DA_HEADS * 2 * DA_HEAD_DIM
ROPE_BASE = 10000.0
Q_BLOCK = 128

N_EXPERTS = 32
TOP_K = 4
D_FF_EXPERT = D_MODEL
SWIGLU_ALPHA = 1.702
SWIGLU_LIMIT = 7.0
MOE_BLOCK = 256

N_S5_LAYERS = len(range(0, DEPTH, N_MIXERS))
N_SSD_LAYERS = len(range(1, DEPTH, N_MIXERS))
N_DA_LAYERS = len(range(2, DEPTH, N_MIXERS))

kernel_name = 'hybrid_s5_ssd_diffattn_moe_dit'

F32 = jnp.float32


def rmsnorm(x, g):
    xf = x.astype(F32)
    y = xf * lax.rsqrt(jnp.mean(xf * xf, axis=-1, keepdims=True) + EPS)
    return (y * g.astype(F32)).astype(x.dtype)


def modulate(x, g, shift, scale):
    return rmsnorm(x, g) * (1 + scale) + shift


def _flip(t, rev):
    return jnp.flip(t, axis=1) if rev else t


def axial_rope_tables(n_tok):
    rows = n_tok // GRID_W
    row = jnp.repeat(jnp.arange(rows, dtype=F32), GRID_W)
    col = jnp.tile(jnp.arange(GRID_W, dtype=F32), rows)
    n_freq = DA_HEAD_DIM // 4
    inv = ROPE_BASE ** (-jnp.arange(n_freq, dtype=F32) / n_freq)
    ar = row[:, None] * inv
    ac = col[:, None] * inv
    return (jnp.cos(ar), jnp.sin(ar), jnp.cos(ac), jnp.sin(ac))


def _rotate(x, cos, sin):
    x1, x2 = jnp.split(x, 2, axis=-1)
    return jnp.concatenate([x1 * cos - x2 * sin, x2 * cos + x1 * sin], axis=-1)


def axial_rope(x, tabs):
    cr, sr, cc, sc = tabs
    xr, xc = jnp.split(x.astype(F32), 2, axis=-1)
    return jnp.concatenate([_rotate(xr, cr, sr), _rotate(xc, cc, sc)], axis=-1).astype(x.dtype)


def s5_discretize(a_re, a_im, log_dt, b_re, b_im):
    a_re = a_re.astype(F32); a_im = a_im.astype(F32)
    dt = jnp.exp(log_dt.astype(F32))[:, None]
    mag = jnp.exp(dt * a_re)
    ang = dt * a_im
    ab_re = mag * jnp.cos(ang)
    ab_im = mag * jnp.sin(ang)
    den = a_re * a_re + a_im * a_im
    num_re = ab_re - 1.0
    co_re = (num_re * a_re + ab_im * a_im) / den
    co_im = (ab_im * a_re - num_re * a_im) / den
    b_re = b_re.astype(F32); b_im = b_im.astype(F32)
    bb_re = co_re[..., None] * b_re - co_im[..., None] * b_im
    bb_im = co_re[..., None] * b_im + co_im[..., None] * b_re
    return ab_re, ab_im, bb_re, bb_im


def _complex_affine_combine(e1, e2):
    a1r, a1i, b1r, b1i = e1
    a2r, a2i, b2r, b2i = e2
    return (a1r * a2r - a1i * a2i, a1r * a2i + a1i * a2r,
            a2r * b1r - a2i * b1i + b2r, a2r * b1i + a2i * b1r + b2i)


def s5_scan(ab_re, ab_im, bb_re, bb_im, u, h0):
    bu_re = jnp.einsum('btgj,gpj->btgp', u, bb_re)
    bu_im = jnp.einsum('btgj,gpj->btgp', u, bb_im)
    if h0 is not None:
        h_re, h_im = h0
        bu_re = bu_re.at[:, 0].add(ab_re * h_re - ab_im * h_im)
        bu_im = bu_im.at[:, 0].add(ab_re * h_im + ab_im * h_re)
    shape = (1,) + bu_re.shape[1:]
    a_re = jnp.broadcast_to(ab_re, shape)
    a_im = jnp.broadcast_to(ab_im, shape)
    _, _, s_re, s_im = lax.associative_scan(_complex_affine_combine, (a_re, a_im, bu_re, bu_im), axis=1)
    return s_re, s_im


def s5_readout(s_re, s_im, c_re, c_im):
    y = jnp.einsum('btgp,gjp->btgj', s_re, c_re.astype(F32)) - jnp.einsum('btgp,gjp->btgj', s_im, c_im.astype(F32))
    return y.reshape(y.shape[0], y.shape[1], D_MODEL)


def s5_mixer(hc, hl, a_re, a_im, log_dt, b_re, b_im, c_re, c_im, d_skip, glu_w, glu_b, ctx_out):
    bsz, n_ctx, _ = hc.shape
    n_lat = hl.shape[1]
    uc = hc.astype(F32).reshape(bsz, n_ctx, S5_GROUPS, S5_GROUP)
    ul = hl.astype(F32).reshape(bsz, n_lat, S5_GROUPS, S5_GROUP)
    dsk = d_skip.astype(F32)
    yl = dsk * hl.astype(F32)
    yc = dsk * hc.astype(F32)
    for dirn in range(2):
        rev = dirn == 1
        ab_re, ab_im, bb_re, bb_im = s5_discretize(a_re[dirn], a_im[dirn], log_dt[dirn], b_re[dirn], b_im[dirn])
        sc_re, sc_im = s5_scan(ab_re, ab_im, bb_re, bb_im, _flip(uc, rev), None)
        sl_re, sl_im = s5_scan(ab_re, ab_im, bb_re, bb_im, _flip(ul, rev), (sc_re[:, -1], sc_im[:, -1]))
        yl = yl + _flip(s5_readout(sl_re, sl_im, c_re[dirn], c_im[dirn]), rev)
        if ctx_out:
            yc = yc + _flip(s5_readout(sc_re, sc_im, c_re[dirn], c_im[dirn]), rev)

    def glu(y, dtype):
        g = jax.nn.gelu(y.astype(dtype))
        a, b = jnp.split(g @ glu_w + glu_b, 2, axis=-1)
        return a * jax.nn.sigmoid(b)

    return (glu(yc, hc.dtype) if ctx_out else None), glu(yl, hl.dtype)


def depthwise_conv_centred(x, w, b):
    pad = (w.shape[0] - 1) // 2
    y = lax.conv_general_dilated(x, w[:, None, :], window_strides=(1,), padding=[(pad, pad)],
                                 dimension_numbers=('NWC', 'WIO', 'NWC'), feature_group_count=x.shape[-1])
    return y + b


def ssd_inputs(h, in_w, conv_w, conv_b, dt_bias):
    bsz, t, _ = h.shape
    zxbcdt = h @ in_w
    z = zxbcdt[..., :SSD_D_INNER]
    xbc = zxbcdt[..., SSD_D_INNER:2 * SSD_D_INNER + 2 * SSD_GN]
    dt = zxbcdt[..., 2 * SSD_D_INNER + 2 * SSD_GN:]
    xbc = jax.nn.silu(depthwise_conv_centred(xbc, conv_w, conv_b))
    xs = xbc[..., :SSD_D_INNER].reshape(bsz, t, SSD_HEADS, SSD_HEAD_DIM)
    bm = xbc[..., SSD_D_INNER:SSD_D_INNER + SSD_GN].reshape(bsz, t, SSD_GROUPS, SSD_STATE)
    cm = xbc[..., SSD_D_INNER + SSD_GN:].reshape(bsz, t, SSD_GROUPS, SSD_STATE)
    dt = jax.nn.softplus(dt.astype(F32).reshape(bsz, t, 2, SSD_HEADS) + dt_bias.astype(F32))
    return z, xs, bm, cm, dt


def ssd_scan(x, dt, a, bm, cm, h0, need_y):
    bsz, t, _, _ = x.shape
    nc, q, g, r = t // SSD_CHUNK, SSD_CHUNK, SSD_GROUPS, SSD_HEADS // SSD_GROUPS
    xdt = x.astype(F32).reshape(bsz, nc, q, g, r, SSD_HEAD_DIM) * dt.reshape(bsz, nc, q, g, r)[..., None]
    cs = jnp.cumsum((dt * a).reshape(bsz, nc, q, g, r), axis=2)
    bc = bm.astype(F32).reshape(bsz, nc, q, g, SSD_STATE)
    cc = cm.astype(F32).reshape(bsz, nc, q, g, SSD_STATE)
    decay_s = jnp.exp(cs[:, :, -1:] - cs)
    states = jnp.einsum('bcsgn,bcsgr,bcsgrp->bcgrpn', bc, decay_s, xdt)
    chunk_decay = jnp.exp(cs[:, :, -1])

    def step(h, inp):
        dec, st = inp
        return dec[..., None, None] * h + st, h

    h_fin, h_in = lax.scan(step, h0, (jnp.moveaxis(chunk_decay, 1, 0), jnp.moveaxis(states, 1, 0)))
    if not need_y:
        return None, h_fin
    h_in = jnp.moveaxis(h_in, 0, 1)
    mask = jnp.tril(jnp.ones((q, q), dtype=bool))[None, None, :, :, None, None]
    seg = cs[:, :, :, None] - cs[:, :, None, :]
    lmat = jnp.exp(jnp.where(mask, seg, -jnp.inf))
    scores = jnp.einsum('bclgn,bcsgn->bclsg', cc, bc)
    y_diag = jnp.einsum('bclsg,bclsgr,bcsgrp->bclgrp', scores, lmat, xdt)
    y_off = jnp.einsum('bclgn,bcgrpn,bclgr->bclgrp', cc, h_in, jnp.exp(cs))
    return (y_diag + y_off).reshape(bsz, t, SSD_HEADS, SSD_HEAD_DIM), h_fin


def ssd_mixer(hc, hl, in_w, conv_w, conv_b, dt_bias, a_log, d_skip, norm_g, out_w, ctx_out):
    zc, xc, bc, cc, dtc = ssd_inputs(hc, in_w, conv_w, conv_b, dt_bias)
    zl, xl, bl, cl, dtl = ssd_inputs(hl, in_w, conv_w, conv_b, dt_bias)
    dsk = d_skip.astype(F32)[:, None]
    yl = dsk * xl.astype(F32)
    yc = dsk * xc.astype(F32)
    zero = jnp.zeros((hc.shape[0], SSD_GROUPS, SSD_HEADS // SSD_GROUPS, SSD_HEAD_DIM, SSD_STATE), F32)
    for dirn in range(2):
        rev = dirn == 1
        a = -jnp.exp(a_log[dirn].astype(F32))
        yc_d, hc_fin = ssd_scan(_flip(xc, rev), _flip(dtc[:, :, dirn], rev), a, _flip(bc, rev), _flip(cc, rev), zero, ctx_out)
        yl_d, _ = ssd_scan(_flip(xl, rev), _flip(dtl[:, :, dirn], rev), a, _flip(bl, rev), _flip(cl, rev), hc_fin, True)
        yl = yl + _flip(yl_d, rev)
        if ctx_out:
            yc = yc + _flip(yc_d, rev)

    def out(y, z):
        bsz, t = z.shape[:2]
        gy = (y.reshape(bsz, t, SSD_D_INNER).astype(z.dtype) * jax.nn.silu(z)).reshape(bsz, t, SSD_GROUPS, -1)
        gy = rmsnorm(gy, norm_g.reshape(SSD_GROUPS, -1)).reshape(bsz, t, SSD_D_INNER)
        return gy @ out_w

    return (out(yc, zc) if ctx_out else None), out(yl, zl)


def _diff_attend(q, k, v, lam):
    s = jnp.einsum('bchqd,bchkd->bchqk', q, k).astype(F32) * (DA_HEAD_DIM ** -0.5)
    p = jax.nn.softmax(s, axis=-1)
    a = p[:, 0] - lam * p[:, 1]
    return jnp.einsum('bhqk,bhkv->bhqv', a.astype(v.dtype), v)


def diff_attention(hc, hl, qkv_w, q_g, k_g, lam_vec, sub_g, out_w, rope, layer_idx, ctx_out):
    lam_init = 0.8 - 0.6 * math.exp(-0.3 * layer_idx)
    lv = lam_vec.astype(F32)
    lam = jnp.exp(jnp.sum(lv[0] * lv[1])) - jnp.exp(jnp.sum(lv[2] * lv[3])) + lam_init

    def project(h, tabs):
        bsz, t, _ = h.shape
        q, k, v = jnp.split(h @ qkv_w, 3, axis=-1)
        q = rmsnorm(q.reshape(bsz, t, DA_HEADS, 2, DA_HEAD_DIM).transpose(0, 3, 2, 1, 4), q_g)
        k = rmsnorm(k.reshape(bsz, t, DA_HEADS, 2, DA_HEAD_DIM).transpose(0, 3, 2, 1, 4), k_g)
        v = v.reshape(bsz, t, DA_HEADS, 2 * DA_HEAD_DIM).transpose(0, 2, 1, 3)
        if tabs is not None:
            q = axial_rope(q, tabs)
            k = axial_rope(k, tabs)
        return q, k, v

    def finish(o):
        bsz, _, t, _ = o.shape
        o = rmsnorm(o, sub_g) * (1.0 - lam_init)
        return o.transpose(0, 2, 1, 3).reshape(bsz, t, DA_WIDTH) @ out_w

    qc, kc, vc = project(hc, None)
    ql, kl, vl = project(hl, rope)
    k_all = jnp.concatenate([kc, kl], axis=3)
    v_all = jnp.concatenate([vc, vl], axis=2)
    bsz, _, _, n_lat, _ = ql.shape
    nb = n_lat // Q_BLOCK
    qb = ql.reshape(bsz, 2, DA_HEADS, nb, Q_BLOCK, DA_HEAD_DIM).transpose(3, 0, 1, 2, 4, 5)
    ob = lax.map(lambda qq: _diff_attend(qq, k_all, v_all, lam), qb)
    ol = ob.transpose(1, 2, 0, 3, 4).reshape(bsz, DA_HEADS, n_lat, 2 * DA_HEAD_DIM)
    yc = finish(_diff_attend(qc, kc, vc, lam)) if ctx_out else None
    return yc, finish(ol)


def moe(h, router_w, router_b, gu_w, gu_b, dn_w, dn_b):
    n_tok, d = h.shape
    logits = (h @ router_w + router_b).astype(F32)
    top_v, top_i = lax.top_k(logits, TOP_K)
    gates = jax.nn.softmax(top_v, axis=-1)
    n_assign = n_tok * TOP_K
    n_blocks = -(-n_assign // MOE_BLOCK) + N_EXPERTS
    n_slots = n_blocks * MOE_BLOCK
    flat_e = top_i.reshape(-1)
    order = jnp.argsort(flat_e)
    sorted_e = flat_e[order]
    counts = jnp.bincount(flat_e, length=N_EXPERTS)
    padded = ((counts + MOE_BLOCK - 1) // MOE_BLOCK) * MOE_BLOCK
    pad_end = jnp.cumsum(padded)
    pad_start = pad_end - padded
    start = jnp.cumsum(counts) - counts
    dest = pad_start[sorted_e] + jnp.arange(n_assign) - start[sorted_e]
    slot_tok = jnp.full((n_slots,), n_tok, dtype=jnp.int32).at[dest].set((order // TOP_K).astype(jnp.int32))
    slot_gate = jnp.zeros((n_slots,), F32).at[dest].set(gates.reshape(-1)[order])
    block_e = jnp.minimum(jnp.searchsorted(pad_end, jnp.arange(n_blocks) * MOE_BLOCK, side='right'), N_EXPERTS - 1)
    h_pad = jnp.concatenate([h, jnp.zeros((1, d), h.dtype)], axis=0)

    def expert_block(args):
        tok, e = args
        gu = h_pad[tok] @ gu_w[e] + gu_b[e]
        gate, up = jnp.split(gu, 2, axis=-1)
        gate = jnp.minimum(gate, SWIGLU_LIMIT)
        up = jnp.clip(up, -SWIGLU_LIMIT, SWIGLU_LIMIT)
        act = (up + 1) * gate * jax.nn.sigmoid(SWIGLU_ALPHA * gate)
        return act @ dn_w[e] + dn_b[e]

    y = lax.map(expert_block, (slot_tok.reshape(n_blocks, MOE_BLOCK), block_e)).reshape(n_slots, d)
    out = jnp.zeros((n_tok + 1, d), h.dtype).at[slot_tok].add(y * slot_gate[:, None].astype(y.dtype))
    return out[:n_tok]


def setup_inputs(seed: int = 0) -> dict:
    key = jax.random.key(seed)
    ks = iter(jax.random.split(key, 64))

    def nrm(shape, scale):
        return scale * jax.random.normal(next(ks), shape, F32)

    def uni(shape, lo, hi):
        return jax.random.uniform(next(ks), shape, F32, lo, hi)

    D, G, P, J = D_MODEL, S5_GROUPS, S5_STATE, S5_GROUP
    ns5, nsd, nda = N_S5_LAYERS, N_SSD_LAYERS, N_DA_LAYERS
    conv_ch = SSD_D_INNER + 2 * SSD_GN
    ssd_dt = jnp.exp(uni((nsd, 2, SSD_HEADS), math.log(1e-3), math.log(1e-1)))
    return {
        'x': nrm((BATCH, SEQ, D), 1.0),
        'c': nrm((BATCH, D), 1.0),
        'ctx': nrm((BATCH, CTX_LEN, D), 1.0),
        'c_ctx': nrm((D,), 1.0),
        'mod_w': nrm((DEPTH, D, 6 * D), 0.5 * D ** -0.5),
        'mod_b': nrm((DEPTH, 6 * D), 0.02),
        'norm1_g': 1.0 + nrm((DEPTH, D), 0.02),
        'norm2_g': 1.0 + nrm((DEPTH, D), 0.02),
        's5_a_re': -0.5 + nrm((ns5, 2, G, P), 0.01),
        's5_a_im': math.pi * jnp.arange(P, dtype=F32) + nrm((ns5, 2, G, P), 0.01),
        's5_log_dt': uni((ns5, 2, G), math.log(S5_DT_MIN), math.log(S5_DT_MAX)),
        's5_b_re': nrm((ns5, 2, G, P, J), (2 * J) ** -0.5),
        's5_b_im': nrm((ns5, 2, G, P, J), (2 * J) ** -0.5),
        's5_c_re': nrm((ns5, 2, G, J, P), (2 * P) ** -0.5),
        's5_c_im': nrm((ns5, 2, G, J, P), (2 * P) ** -0.5),
        's5_d': nrm((ns5, D), 0.5),
        's5_glu_w': nrm((ns5, D, 2 * D), D ** -0.5),
        's5_glu_b': nrm((ns5, 2 * D), 0.02),
        'ssd_in_w': nrm((nsd, D, SSD_IN_DIM), D ** -0.5),
        'ssd_conv_w': nrm((nsd, SSD_CONV, conv_ch), SSD_CONV ** -0.5),
        'ssd_conv_b': nrm((nsd, conv_ch), 0.02),
        'ssd_dt_bias': ssd_dt + jnp.log(-jnp.expm1(-ssd_dt)),
        'ssd_a_log': jnp.log(uni((nsd, 2, SSD_HEADS), 1.0, 16.0)),
        'ssd_d': 1.0 + nrm((nsd, SSD_HEADS), 0.1),
        'ssd_norm_g': 1.0 + nrm((nsd, SSD_D_INNER), 0.02),
        'ssd_out_w': nrm((nsd, SSD_D_INNER, D), SSD_D_INNER ** -0.5),
        'da_qkv_w': nrm((nda, D, 3 * DA_WIDTH), D ** -0.5),
        'da_q_g': 1.0 + nrm((nda, DA_HEAD_DIM), 0.02),
        'da_k_g': 1.0 + nrm((nda, DA_HEAD_DIM), 0.02),
        'da_lam': nrm((nda, 4, DA_HEAD_DIM), 0.1),
        'da_sub_g': 1.0 + nrm((nda, 2 * DA_HEAD_DIM), 0.02),
        'da_out_w': nrm((nda, DA_WIDTH, D), DA_WIDTH ** -0.5),
        'moe_router_w': nrm((DEPTH, D, N_EXPERTS), D ** -0.5),
        'moe_router_b': nrm((DEPTH, N_EXPERTS), 0.01),
        'moe_gu_w': nrm((DEPTH, N_EXPERTS, D, 2 * D_FF_EXPERT), D ** -0.5),
        'moe_gu_b': nrm((DEPTH, N_EXPERTS, 2 * D_FF_EXPERT), 0.01),
        'moe_dn_w': nrm((DEPTH, N_EXPERTS, D_FF_EXPERT, D), D_FF_EXPERT ** -0.5),
        'moe_dn_b': nrm((DEPTH, N_EXPERTS, D), 0.01),
    }


def reference(x, c, ctx, c_ctx, mod_w, mod_b, norm1_g, norm2_g,
              s5_a_re, s5_a_im, s5_log_dt, s5_b_re, s5_b_im, s5_c_re, s5_c_im, s5_d, s5_glu_w, s5_glu_b,
              ssd_in_w, ssd_conv_w, ssd_conv_b, ssd_dt_bias, ssd_a_log, ssd_d, ssd_norm_g, ssd_out_w,
              da_qkv_w, da_q_g, da_k_g, da_lam, da_sub_g, da_out_w,
              moe_router_w, moe_router_b, moe_gu_w, moe_gu_b, moe_dn_w, moe_dn_b):
    bsz, n_lat, d = x.shape
    n_ctx = ctx.shape[1]
    rope = axial_rope_tables(n_lat)
    xl, xc = x, ctx
    silu_c = jax.nn.silu(c)
    silu_cc = jax.nn.silu(c_ctx)[None]
    for i in range(DEPTH):
        last = i == DEPTH - 1
        kind, j = i % N_MIXERS, i // N_MIXERS
        ml = jnp.split((silu_c @ mod_w[i] + mod_b[i])[:, None, :], 6, axis=-1)
        mc = jnp.split((silu_cc @ mod_w[i] + mod_b[i])[:, None, :], 6, axis=-1)
        hl = modulate(xl, norm1_g[i], ml[0], ml[1])
        hc = modulate(xc, norm1_g[i], mc[0], mc[1])
        if kind == 0:
            yc, yl = s5_mixer(hc, hl, s5_a_re[j], s5_a_im[j], s5_log_dt[j], s5_b_re[j], s5_b_im[j],
                              s5_c_re[j], s5_c_im[j], s5_d[j], s5_glu_w[j], s5_glu_b[j], not last)
        elif kind == 1:
            yc, yl = ssd_mixer(hc, hl, ssd_in_w[j], ssd_conv_w[j], ssd_conv_b[j], ssd_dt_bias[j],
                               ssd_a_log[j], ssd_d[j], ssd_norm_g[j], ssd_out_w[j], not last)
        else:
            yc, yl = diff_attention(hc, hl, da_qkv_w[j], da_q_g[j], da_k_g[j], da_lam[j], da_sub_g[j],
                                    da_out_w[j], rope, i, not last)
        xl = xl + ml[2] * yl
        hl = modulate(xl, norm2_g[i], ml[3], ml[4])
        moe_p = (moe_router_w[i], moe_router_b[i], moe_gu_w[i], moe_gu_b[i], moe_dn_w[i], moe_dn_b[i])
        if last:
            xl = xl + ml[5] * moe(hl.reshape(-1, d), *moe_p).reshape(bsz, n_lat, d)
        else:
            xc = xc + mc[2] * yc
            hc = modulate(xc, norm2_g[i], mc[3], mc[4])
            f = moe(jnp.concatenate([hc.reshape(-1, d), hl.reshape(-1, d)], axis=0), *moe_p)
            xc = xc + mc[5] * f[:bsz * n_ctx].reshape(bsz, n_ctx, d)
            xl = xl + ml[5] * f[bsz * n_ctx:].reshape(bsz, n_lat, d)
    return xl
```

```python
import functools
import math
from typing import NamedTuple

import jax
import jax.numpy as jnp
from jax import lax
from jax.experimental import pallas as pl
from jax.experimental.pallas import tpu as pltpu

F32 = jnp.float32
BF16 = jnp.bfloat16
EPS = 1e-6
HP = lax.Precision.HIGHEST

LANES = 128
VMEM_LIMIT = 56 * 1024 * 1024

N_EXPERTS = 32
TOP_K = 4
SWIGLU_ALPHA = 1.702
SWIGLU_LIMIT = 7.0
MOE_BLK = 512
MOD_TILE = 256
ROW_TILE = 512


class Geo(NamedTuple):
    n_batch: int
    n_ctx: int
    n_lat: int

    @property
    def rows_per_batch(self):
        return self.n_ctx + self.n_lat

    @property
    def rows(self):
        return self.n_batch * self.rows_per_batch

    def tiles(self, tm):
        return self.rows_per_batch // tm, self.n_ctx // tm, self.n_lat // tm


def _cp(sem, vmem=VMEM_LIMIT):
    return pltpu.CompilerParams(dimension_semantics=sem, vmem_limit_bytes=vmem)


def _tile_maps(geo, tm, lat_only):
    tpb, ct, lt = geo.tiles(tm)
    if lat_only:
        src = lambda i: (i // lt) * tpb + ct + i % lt
        mod = lambda i: i // lt
        return geo.n_batch * lt, src, mod
    mod = lambda i: jnp.where(i % tpb < ct, geo.n_batch, i // tpb)
    return geo.n_batch * tpb, (lambda i: i), mod


def _rms(x):
    return x * lax.rsqrt(jnp.mean(x * x, axis=-1, keepdims=True) + EPS)


def _mods_kernel(s_ref, w_ref, b_ref, o_ref):
    s = s_ref[...]
    s = s * jax.nn.sigmoid(s)
    o_ref[0] = jnp.dot(s, w_ref[0], preferred_element_type=F32, precision=HP) + b_ref[0]


def _mods(c, c_ctx, mod_w, mod_b):
    depth, d, d6 = mod_w.shape
    bsz = c.shape[0]
    rows = -(-(bsz + 1) // 8) * 8
    s = jnp.zeros((rows, d), F32).at[:bsz].set(c).at[bsz].set(c_ctx)
    nj = d6 // d
    out = pl.pallas_call(
        _mods_kernel,
        grid=(depth, nj),
        in_specs=[
            pl.BlockSpec((rows, d), lambda l, j: (0, 0)),
            pl.BlockSpec((1, d, d), lambda l, j: (l, 0, j)),
            pl.BlockSpec((1, 1, d), lambda l, j: (l, 0, j)),
        ],
        out_specs=pl.BlockSpec((1, rows, d), lambda l, j: (l, 0, j)),
        out_shape=jax.ShapeDtypeStruct((depth, rows, d6), F32),
        compiler_params=_cp(("arbitrary", "arbitrary")),
        name="mods",
    )(s, mod_w, mod_b.reshape(depth, 1, d6))
    out = out[:, :bsz + 1].reshape(depth, bsz + 1, nj, d)
    return jnp.pad(out, ((0, 0), (0, 0), (0, 8 - nj), (0, 0)))


def _route_kernel(x_ref, y_ref, mod_ref, g_ref, rw_ref, rb_ref, tri_ref,
                  xn_ref, h_ref, route_ref, cnt_ref, base_ref):
    i = pl.program_id(0)

    @pl.when(i == 0)
    def _():
        base_ref[...] = jnp.zeros_like(base_ref)

    m = mod_ref[0]
    x = x_ref[...] + m[2:3, :] * y_ref[...]
    xn_ref[...] = x
    h = _rms(x) * g_ref[...] * (1.0 + m[4:5, :]) + m[3:4, :]
    h_ref[...] = h
    logits = jnp.dot(h, rw_ref[...], preferred_element_type=F32, precision=HP) + rb_ref[...]
    tm = logits.shape[0]
    lane = lax.broadcasted_iota(jnp.int32, (tm, LANES), 1)
    neg = jnp.float32(-jnp.inf)
    work = jnp.where(lane < N_EXPERTS, logits, neg)
    vals, ohs, idxs = [], [], []
    for _ in range(TOP_K):
        mk = jnp.max(work, axis=-1, keepdims=True)
        ik = jnp.min(jnp.where(work == mk, lane, LANES), axis=-1, keepdims=True)
        oh = lane == ik
        work = jnp.where(oh, neg, work)
        vals.append(mk)
        ohs.append(oh)
        idxs.append(ik)
    es = [jnp.exp(v - vals[0]) for v in vals]
    den = es[0] + es[1] + es[2] + es[3]
    onehot = jnp.zeros((tm, LANES), F32)
    for oh in ohs:
        onehot = onehot + jnp.where(oh, 1.0, 0.0)
    prefix = jnp.dot(tri_ref[...], onehot.astype(BF16), preferred_element_type=F32)
    pos = prefix + base_ref[0:1, :]
    out = jnp.zeros((tm, LANES), F32)
    for k in range(TOP_K):
        rank = jnp.sum(jnp.where(ohs[k], pos, 0.0), axis=-1, keepdims=True)
        out = out + jnp.where(lane == k, idxs[k].astype(F32), 0.0)
        out = out + jnp.where(lane == TOP_K + k, rank, 0.0)
        out = out + jnp.where(lane == 2 * TOP_K + k, es[k] / den, 0.0)
    route_ref[...] = out
    newbase = base_ref[0:1, :] + jnp.sum(onehot, axis=0, keepdims=True)
    base_ref[...] = jnp.broadcast_to(newbase, base_ref.shape)
    cnt_ref[...] = jnp.broadcast_to(newbase, cnt_ref.shape)


def _route(x, y, mods, g2, rw, rb, geo, lat_only, tm=MOD_TILE):
    d = x.shape[1]
    nt, src, mod = _tile_maps(geo, tm, lat_only)
    n = nt * tm
    rw_p = jnp.zeros((d, LANES), F32).at[:, :N_EXPERTS].set(rw)
    rb_p = jnp.zeros((1, LANES), F32).at[0, :N_EXPERTS].set(rb)
    tri = (jnp.arange(tm)[:, None] > jnp.arange(tm)[None, :]).astype(BF16)
    const = lambda i: (0, 0)
    return pl.pallas_call(
        _route_kernel,
        grid=(nt,),
        in_specs=[
            pl.BlockSpec((tm, d), lambda i: (src(i), 0)),
            pl.BlockSpec((tm, d), lambda i: (src(i), 0)),
            pl.BlockSpec((1, 8, d), lambda i: (mod(i), 0, 0)),
            pl.BlockSpec((1, d), const),
            pl.BlockSpec((d, LANES), const),
            pl.BlockSpec((1, LANES), const),
            pl.BlockSpec((tm, tm), const),
        ],
        out_specs=[
            pl.BlockSpec((tm, d), lambda i: (i, 0)),
            pl.BlockSpec((tm, d), lambda i: (i, 0)),
            pl.BlockSpec((tm, LANES), lambda i: (i, 0)),
            pl.BlockSpec((8, LANES), const),
        ],
        out_shape=[
            jax.ShapeDtypeStruct((n, d), F32),
            jax.ShapeDtypeStruct((n, d), F32),
            jax.ShapeDtypeStruct((n, LANES), F32),
            jax.ShapeDtypeStruct((8, LANES), F32),
        ],
        scratch_shapes=[pltpu.VMEM((8, LANES), F32)],
        compiler_params=_cp(("arbitrary",)),
        name="route",
    )(x, y, mods, g2.reshape(1, d), rw_p, rb_p, tri)


def _dispatch_kernel(dest_ref, h_ref, xs_in_ref, xs_ref, sem):
    del xs_in_ref
    tm = h_ref.shape[0]

    def copy(r, k):
        j = r * TOP_K + k
        d = dest_ref[j // LANES, j % LANES]
        return pltpu.make_async_copy(h_ref.at[pl.ds(r, 1), :], xs_ref.at[pl.ds(d, 1), :], sem)

    def start(r, c):
        for k in range(TOP_K):
            copy(r, k).start()
        return c

    def wait(r, c):
        for k in range(TOP_K):
            copy(r, k).wait()
        return c

    lax.fori_loop(0, tm, start, 0)
    lax.fori_loop(0, tm, wait, 0)


def _dispatch(h, dest2d, n_slots, tm=ROW_TILE):
    n, d = h.shape
    nt = n // tm
    db = tm * TOP_K // LANES
    xs0 = jnp.zeros((n_slots, d), F32)
    return pl.pallas_call(
        _dispatch_kernel,
        grid=(nt,),
        in_specs=[
            pl.BlockSpec((db, LANES), lambda i: (i, 0), memory_space=pltpu.SMEM),
            pl.BlockSpec((tm, d), lambda i: (i, 0)),
            pl.BlockSpec(memory_space=pl.ANY),
        ],
        out_specs=pl.BlockSpec(memory_space=pl.ANY),
        out_shape=jax.ShapeDtypeStruct((n_slots, d), F32),
        scratch_shapes=[pltpu.SemaphoreType.DMA(())],
        input_output_aliases={2: 0},
        compiler_params=_cp(("arbitrary",)),
        name="dispatch",
    )(dest2d, h, xs0)


def _expert_kernel(be_ref, na_ref, xs_ref, guw_ref, gub_ref, dnw_ref, dnb_ref, ys_ref):
    j = pl.program_id(0)

    @pl.when(j < na_ref[0])
    def _():
        f = dnw_ref.shape[1]
        x = xs_ref[...].astype(BF16)
        gu = jnp.dot(x, guw_ref[0], preferred_element_type=F32) + gub_ref[0]
        gate = jnp.minimum(gu[:, :f], SWIGLU_LIMIT)
        up = jnp.clip(gu[:, f:], -SWIGLU_LIMIT, SWIGLU_LIMIT)
        act = (up + 1.0) * gate * jax.nn.sigmoid(SWIGLU_ALPHA * gate)
        ys_ref[...] = jnp.dot(act.astype(BF16), dnw_ref[0], preferred_element_type=F32) + dnb_ref[0]


def _experts(xs, block_e, n_active, guw, gub, dnw, dnb, blk=MOE_BLK):
    n_slots, d = xs.shape
    nb = n_slots // blk
    e, _, f2 = guw.shape
    f = f2 // 2
    act_blk = lambda j, be, na: (jnp.minimum(j, na[0] - 1), 0)
    wsel = lambda j, be, na: (be[j], 0, 0)
    grid_spec = pltpu.PrefetchScalarGridSpec(
        num_scalar_prefetch=2,
        grid=(nb,),
        in_specs=[
            pl.BlockSpec((blk, d), act_blk),
            pl.BlockSpec((1, d, f2), wsel),
            pl.BlockSpec((1, 1, f2), wsel),
            pl.BlockSpec((1, f, d), wsel),
            pl.BlockSpec((1, 1, d), wsel),
        ],
        out_specs=pl.BlockSpec((blk, d), act_blk),
    )
    return pl.pallas_call(
        _expert_kernel,
        grid_spec=grid_spec,
        out_shape=jax.ShapeDtypeStruct((n_slots, d), F32),
        compiler_params=_cp(("arbitrary",)),
        name="experts",
    )(block_e, n_active, xs, guw, gub.reshape(e, 1, f2), dnw, dnb.reshape(e, 1, d))


def _combine_kernel(dest_ref, x_ref, route_ref, mod_ref, ys_ref, o_ref, buf, sem):
    tm = x_ref.shape[0]

    def copy(r, k):
        j = r * TOP_K + k
        d = dest_ref[j // LANES, j % LANES]
        return pltpu.make_async_copy(ys_ref.at[pl.ds(d, 1), :], buf.at[k, pl.ds(r, 1), :], sem)

    def start(r, c):
        for k in range(TOP_K):
            copy(r, k).start()
        return c

    def wait(r, c):
        for k in range(TOP_K):
            copy(r, k).wait()
        return c

    lax.fori_loop(0, tm, start, 0)
    lax.fori_loop(0, tm, wait, 0)
    route = route_ref[...]
    f = jnp.zeros(x_ref.shape, F32)
    for k in range(TOP_K):
        f = f + route[:, 2 * TOP_K + k:2 * TOP_K + k + 1] * buf[k]
    o_ref[...] = x_ref[...] + mod_ref[0][5:6, :] * f


def _combine(x, route, mods, ys, dest2d, geo, lat_only, tm=MOD_TILE):
    n, d = x.shape
    nt, _, mod = _tile_maps(geo, tm, lat_only)
    db = tm * TOP_K // LANES
    return pl.pallas_call(
        _combine_kernel,
        grid=(nt,),
        in_specs=[
            pl.BlockSpec((db, LANES), lambda i: (i, 0), memory_space=pltpu.SMEM),
            pl.BlockSpec((tm, d), lambda i: (i, 0)),
            pl.BlockSpec((tm, LANES), lambda i: (i, 0)),
            pl.BlockSpec((1, 8, d), lambda i: (mod(i), 0, 0)),
            pl.BlockSpec(memory_space=pl.ANY),
        ],
        out_specs=pl.BlockSpec((tm, d), lambda i: (i, 0)),
        out_shape=jax.ShapeDtypeStruct((n, d), F32),
        scratch_shapes=[pltpu.VMEM((TOP_K, tm, d), F32), pltpu.SemaphoreType.DMA(())],
        compiler_params=_cp(("arbitrary",)),
        name="combine",
    )(dest2d, x, route, mods, ys)


def _moe_layer(x, y, mods, g2, rw, rb, guw, gub, dnw, dnb, geo, lat_only):
    xn, h, route, cnt = _route(x, y, mods, g2, rw, rb, geo, lat_only)
    n = xn.shape[0]
    counts = cnt[0, :N_EXPERTS].astype(jnp.int32)
    nblk = (counts + MOE_BLK - 1) // MOE_BLK
    blk_end = jnp.cumsum(nblk)
    pad_start = (blk_end - nblk) * MOE_BLK
    n_blocks = -(-(n * TOP_K) // MOE_BLK) + N_EXPERTS
    block_e = jnp.minimum(jnp.searchsorted(blk_end, jnp.arange(n_blocks), side='right'), N_EXPERTS - 1)
    n_active = blk_end[-1:].astype(jnp.int32)
    eid = route[:, :TOP_K].astype(jnp.int32)
    rank = route[:, TOP_K:2 * TOP_K].astype(jnp.int32)
    dest = (pad_start[eid] + rank).astype(jnp.int32).reshape(n * TOP_K // LANES, LANES)
    xs = _dispatch(h, dest, n_blocks * MOE_BLK)
    ys = _experts(xs, block_e.astype(jnp.int32), n_active, guw.astype(BF16), gub, dnw.astype(BF16), dnb)
    return _combine(xn, route, mods, ys, dest, geo, lat_only)


def _prenorm_kernel(x_ref, mod_ref, g_ref, h_ref):
    m = mod_ref[0]
    h = _rms(x_ref[...]) * g_ref[...] * (1.0 + m[1:2, :]) + m[0:1, :]
    h_ref[...] = h.astype(h_ref.dtype)


def _prenorm(x, mods, g1, geo, tm=MOD_TILE):
    n, d = x.shape
    nt, _, mod = _tile_maps(geo, tm, False)
    return pl.pallas_call(
        _prenorm_kernel,
        grid=(nt,),
        in_specs=[
            pl.BlockSpec((tm, d), lambda i: (i, 0)),
            pl.BlockSpec((1, 8, d), lambda i: (mod(i), 0, 0)),
            pl.BlockSpec((1, d), lambda i: (0, 0)),
        ],
        out_specs=pl.BlockSpec((tm, d), lambda i: (i, 0)),
        out_shape=jax.ShapeDtypeStruct((n, d), BF16),
        compiler_params=_cp(("parallel",)),
        name="prenorm",
    )(x, mods, g1.reshape(1, d))


S5_CHUNK = 16


def _s5_matrices(a_re, a_im, log_dt, b_re, b_im, c_re, c_im):
    L = S5_CHUNK
    a_re = a_re.astype(F32)
    a_im = a_im.astype(F32)
    dt = jnp.exp(log_dt.astype(F32))[..., None]
    mag = jnp.exp(dt * a_re)
    ang = dt * a_im
    ab_re, ab_im = mag * jnp.cos(ang), mag * jnp.sin(ang)
    den = a_re * a_re + a_im * a_im
    num_re = ab_re - 1.0
    co_re = (num_re * a_re + ab_im * a_im) / den
    co_im = (ab_im * a_re - num_re * a_im) / den
    bb_re = co_re[..., None] * b_re - co_im[..., None] * b_im
    bb_im = co_re[..., None] * b_im + co_im[..., None] * b_re
    taus = jnp.arange(L + 1, dtype=F32)[:, None, None, None]
    pmag = jnp.exp(taus * dt * a_re)
    pang = taus * dt * a_im
    pw_re, pw_im = pmag * jnp.cos(pang), pmag * jnp.sin(pang)
    cp_re = c_re[None] * pw_re[:, :, :, None, :] - c_im[None] * pw_im[:, :, :, None, :]
    cp_im = c_re[None] * pw_im[:, :, :, None, :] + c_im[None] * pw_re[:, :, :, None, :]
    kk = (jnp.einsum('tdgjp,dgpi->tdgji', cp_re, bb_re, precision=HP)
          - jnp.einsum('tdgjp,dgpi->tdgji', cp_im, bb_im, precision=HP))
    t_idx = jnp.arange(L)
    lag = t_idx[None, :] - t_idx[:, None]

    def toeplitz(kd, lagm):
        kt = kd[jnp.clip(lagm, 0, L)]
        kt = jnp.where((lagm >= 0)[:, :, None, None, None], kt, 0.0)
        return kt.transpose(2, 0, 4, 1, 3)

    g = a_re.shape[1]
    p = a_re.shape[2]
    j = b_re.shape[3]
    m = (toeplitz(kk[:, 0], lag) + toeplitz(kk[:, 1], -lag)).reshape(g, L * j, L * j)

    def inject(d, steps):
        pr, pi = pw_re[steps, d], pw_im[steps, d]
        re = pr[:, :, :, None] * bb_re[d][None] - pi[:, :, :, None] * bb_im[d][None]
        im = pr[:, :, :, None] * bb_im[d][None] + pi[:, :, :, None] * bb_re[d][None]
        f = lambda z: z.transpose(1, 0, 3, 2).reshape(g, L * j, p)
        return f(re), f(im)

    inj = inject(0, L - 1 - t_idx) + inject(1, t_idx)

    def readout(d, steps):
        re = cp_re[steps, d]
        im = cp_im[steps, d]
        f = lambda z: z.transpose(1, 3, 0, 2).reshape(g, p, L * j)
        return f(re), f(-im)

    rd = readout(0, t_idx + 1) + readout(1, L - t_idx)
    adv = jnp.stack([pw_re[L, 0], pw_im[L, 0], pw_re[L, 1], pw_im[L, 1]], axis=1)
    adv = jnp.pad(adv, ((0, 0), (0, 4), (0, 0)))
    bf = lambda z: z.astype(BF16)
    return bf(m), [bf(z) for z in inj], [bf(z) for z in rd], adv


def _s5_kernel(u_ref, m_ref, if_re, if_im, ib_re, ib_im, rf_re, rf_im, rb_re, rb_im, adv_ref, y_ref,
               sf_re, sf_im, sb_re, sb_im, *, n_batch, ctx_chunks, lat_chunks):
    u = u_ref[0]
    dot = functools.partial(jnp.dot, preferred_element_type=F32)
    sf_re[...] = dot(u, if_re[0])
    sf_im[...] = dot(u, if_im[0])
    sb_re[...] = dot(u, ib_re[0])
    sb_im[...] = dot(u, ib_im[0])
    adv = adv_ref[0]
    p = adv.shape[-1]
    nb = n_batch

    def scan(s_re_ref, s_im_ref, a_re, a_im, order):
        a_re = jnp.broadcast_to(a_re, (nb, p))
        a_im = jnp.broadcast_to(a_im, (nb, p))

        def step(row, carry):
            c_re, c_im = carry
            row = pl.multiple_of(row, nb)
            l_re = s_re_ref[pl.ds(row, nb), :]
            l_im = s_im_ref[pl.ds(row, nb), :]
            s_re_ref[pl.ds(row, nb), :] = c_re
            s_im_ref[pl.ds(row, nb), :] = c_im
            return (a_re * c_re - a_im * c_im + l_re, a_re * c_im + a_im * c_re + l_im)

        carry = (jnp.zeros((nb, p), F32), jnp.zeros((nb, p), F32))
        for start, count, rev in order:
            def body(i, carry, start=start, count=count, rev=rev):
                c = (count - 1 - i) if rev else i
                return step((start + c) * nb, carry)
            carry = lax.fori_loop(0, count, body, carry)

    scan(sf_re, sf_im, adv[0:1, :], adv[1:2, :], [(0, ctx_chunks + lat_chunks, False)])
    scan(sb_re, sb_im, adv[2:3, :], adv[3:4, :], [(0, ctx_chunks, True), (ctx_chunks, lat_chunks, True)])
    y = dot(u, m_ref[0])
    y = y + dot(sf_re[...].astype(BF16), rf_re[0]) + dot(sf_im[...].astype(BF16), rf_im[0])
    y = y + dot(sb_re[...].astype(BF16), rb_re[0]) + dot(sb_im[...].astype(BF16), rb_im[0])
    y_ref[0] = y.astype(y_ref.dtype)


def _s5_core(h, mats, geo):
    m, inj, rd, adv = mats
    g, lj, _ = m.shape
    p = adv.shape[-1]
    L = S5_CHUNK
    j = lj // L
    nb = geo.n_batch
    nchunk = geo.rows_per_batch // L
    rows = nchunk * nb
    u = h.reshape(nb, nchunk, L, g, j).transpose(3, 1, 0, 2, 4).reshape(g, rows, lj)
    gsel = lambda i: (i, 0, 0)
    y = pl.pallas_call(
        functools.partial(_s5_kernel, n_batch=nb, ctx_chunks=geo.n_ctx // L, lat_chunks=geo.n_lat // L),
        grid=(g,),
        in_specs=[pl.BlockSpec((1, rows, lj), gsel), pl.BlockSpec((1, lj, lj), gsel)]
        + [pl.BlockSpec((1, lj, p), gsel)] * 4 + [pl.BlockSpec((1, p, lj), gsel)] * 4
        + [pl.BlockSpec((1, 8, p), gsel)],
        out_specs=pl.BlockSpec((1, rows, lj), gsel),
        out_shape=jax.ShapeDtypeStruct((g, rows, lj), BF16),
        scratch_shapes=[pltpu.VMEM((rows, p), F32)] * 4,
        compiler_params=_cp(("parallel",)),
        name="s5_core",
    )(u, m, *inj, *rd, adv)
    return y.reshape(g, nchunk, nb, L, j).transpose(2, 1, 3, 0, 4).reshape(nb * nchunk * L, g * j)


def _gelu_tanh(x):
    return 0.5 * x * (1.0 + jnp.tanh(math.sqrt(2.0 / math.pi) * (x + 0.044715 * (x * x * x))))


def _s5_glu_kernel(h_ref, y_ref, d_ref, w_ref, b_ref, o_ref):
    y = d_ref[...] * h_ref[...].astype(F32) + y_ref[...].astype(F32)
    gl = _gelu_tanh(y).astype(BF16)
    z = jnp.dot(gl, w_ref[...], preferred_element_type=F32) + b_ref[...]
    dd = o_ref.shape[1]
    o_ref[...] = z[:, :dd] * jax.nn.sigmoid(z[:, dd:])


def _s5_glu(h, y, d_skip, glu_w, glu_b, tm=ROW_TILE):
    n, d = h.shape
    return pl.pallas_call(
        _s5_glu_kernel,
        grid=(n // tm,),
        in_specs=[
            pl.BlockSpec((tm, d), lambda i: (i, 0)),
            pl.BlockSpec((tm, d), lambda i: (i, 0)),
            pl.BlockSpec((1, d), lambda i: (0, 0)),
            pl.BlockSpec((d, 2 * d), lambda i: (0, 0)),
            pl.BlockSpec((1, 2 * d), lambda i: (0, 0)),
        ],
        out_specs=pl.BlockSpec((tm, d), lambda i: (i, 0)),
        out_shape=jax.ShapeDtypeStruct((n, d), F32),
        compiler_params=_cp(("parallel",)),
        name="s5_glu",
    )(h, y, d_skip.reshape(1, d).astype(F32), glu_w.astype(BF16), glu_b.reshape(1, 2 * d).astype(F32))


def _s5_mixer(x, mods, g1, prm, geo):
    a_re, a_im, log_dt, b_re, b_im, c_re, c_im, d_skip, glu_w, glu_b = prm
    h = _prenorm(x, mods, g1, geo)
    mats = _s5_matrices(a_re, a_im, log_dt, b_re, b_im, c_re, c_im)
    y = _s5_core(h, mats, geo)
    return _s5_glu(h, y, d_skip, glu_w, glu_b)


SSD_Q = 256
SSD_HEAD_DIM = 64
SSD_STATE = 128
SSD_CONV = 5
CONV_HALO = 16


def _ssd_inproj_kernel(x_ref, mod_ref, g_ref, wz_ref, wx_ref, wd_ref, z_ref, xbc_ref, dt_ref):
    m = mod_ref[0]
    h = (_rms(x_ref[...]) * g_ref[...] * (1.0 + m[1:2, :]) + m[0:1, :]).astype(BF16)
    z_ref[...] = jnp.dot(h, wz_ref[...], preferred_element_type=F32).astype(z_ref.dtype)
    xbc_ref[...] = jnp.dot(h, wx_ref[...], preferred_element_type=F32).astype(xbc_ref.dtype)
    dt_ref[...] = jnp.dot(h, wd_ref[...], preferred_element_type=F32)


def _ssd_inproj(x, mods, g1, in_w, d_inner, n_bc, geo, tm=MOD_TILE):
    n, d = x.shape
    nt, _, mod = _tile_maps(geo, tm, False)
    wz = in_w[:, :d_inner].astype(BF16)
    wx = in_w[:, d_inner:2 * d_inner + n_bc].astype(BF16)
    wd = in_w[:, 2 * d_inner + n_bc:]
    wd = jnp.pad(wd, ((0, 0), (0, LANES - wd.shape[1]))).astype(BF16)
    cx = d_inner + n_bc
    const = lambda i: (0, 0)
    row = lambda i: (i, 0)
    return pl.pallas_call(
        _ssd_inproj_kernel,
        grid=(nt,),
        in_specs=[
            pl.BlockSpec((tm, d), row),
            pl.BlockSpec((1, 8, d), lambda i: (mod(i), 0, 0)),
            pl.BlockSpec((1, d), const),
            pl.BlockSpec((d, d_inner), const),
            pl.BlockSpec((d, cx), const),
            pl.BlockSpec((d, LANES), const),
        ],
        out_specs=[pl.BlockSpec((tm, d_inner), row), pl.BlockSpec((tm, cx), row), pl.BlockSpec((tm, LANES), row)],
        out_shape=[
            jax.ShapeDtypeStruct((n, d_inner), BF16),
            jax.ShapeDtypeStruct((n, cx), BF16),
            jax.ShapeDtypeStruct((n, LANES), F32),
        ],
        compiler_params=_cp(("parallel",)),
        name="ssd_inproj",
    )(x, mods, g1.reshape(1, d), wz, wx, wd)


def _ssd_conv_kernel(prev_ref, cur_ref, next_ref, w_ref, b_ref, o_ref, *, tiles_per_batch, ctx_tiles):
    j = pl.program_id(0) % tiles_per_batch
    first = jnp.logical_or(j == 0, j == ctx_tiles)
    last = jnp.logical_or(j == ctx_tiles - 1, j == tiles_per_batch - 1)
    half = CONV_HALO // 2
    cur = cur_ref[...].astype(F32)
    tt = cur.shape[0]
    prev = jnp.where(first, 0.0, prev_ref[...].astype(F32)[half:, :])
    nxt = jnp.where(last, 0.0, next_ref[...].astype(F32)[:half, :])
    ext = jnp.concatenate([prev, cur, nxt], axis=0)
    w = w_ref[...]
    pad = (SSD_CONV - 1) // 2
    acc = jnp.broadcast_to(b_ref[...], cur.shape)
    for k in range(SSD_CONV):
        off = half + k - pad
        acc = acc + w[k:k + 1, :] * ext[off:off + tt, :]
    o_ref[...] = (acc * jax.nn.sigmoid(acc)).astype(o_ref.dtype)


def _ssd_conv(xbc, conv_w, conv_b, geo, tt=MOD_TILE, tc=512):
    n, cx = xbc.shape
    tpb, ct, _ = geo.tiles(tt)
    nt = n // tt
    hb = tt // CONV_HALO
    nhb = n // CONV_HALO
    wpad = jnp.pad(conv_w, ((0, 8 - conv_w.shape[0]), (0, 0))).astype(F32)
    return pl.pallas_call(
        functools.partial(_ssd_conv_kernel, tiles_per_batch=tpb, ctx_tiles=ct),
        grid=(nt, cx // tc),
        in_specs=[
            pl.BlockSpec((CONV_HALO, tc), lambda i, c: (jnp.maximum(i * hb - 1, 0), c)),
            pl.BlockSpec((tt, tc), lambda i, c: (i, c)),
            pl.BlockSpec((CONV_HALO, tc), lambda i, c: (jnp.minimum((i + 1) * hb, nhb - 1), c)),
            pl.BlockSpec((8, tc), lambda i, c: (0, c)),
            pl.BlockSpec((1, tc), lambda i, c: (0, c)),
        ],
        out_specs=pl.BlockSpec((tt, tc), lambda i, c: (i, c)),
        out_shape=jax.ShapeDtypeStruct((n, cx), BF16),
        compiler_params=_cp(("parallel", "parallel")),
        name="ssd_conv",
    )(xbc, xbc, xbc, wpad, conv_b.reshape(1, cx).astype(F32))


def _ssd_dt_kernel(raw_ref, bias_ref, a_ref, dt_ref, cs_ref, cst_ref, *, n_heads):
    q = raw_ref.shape[0]
    dt = jax.nn.softplus(raw_ref[...] + bias_ref[...])
    dta = dt * a_ref[...]
    r = lax.broadcasted_iota(jnp.int32, (q, q), 0)
    c = lax.broadcasted_iota(jnp.int32, (q, q), 1)
    lower = jnp.where(c <= r, 1.0, 0.0)
    upper = jnp.where(c >= r, 1.0, 0.0)
    cs_f = jnp.dot(lower, dta, preferred_element_type=F32, precision=HP)
    cs_b = jnp.dot(upper, dta, preferred_element_type=F32, precision=HP)
    lane = lax.broadcasted_iota(jnp.int32, dt.shape, 1)
    cs = jnp.where(lane < n_heads, cs_f, cs_b)
    dt_ref[...] = dt
    cs_ref[...] = cs
    cst_ref[0] = cs.T


def _ssd_dt(dt_raw, dt_bias, a_log, n_heads):
    n = dt_raw.shape[0]
    q = SSD_Q
    pad = lambda v: jnp.pad(v.reshape(1, -1).astype(F32), ((0, 0), (0, LANES - v.size)))
    return pl.pallas_call(
        functools.partial(_ssd_dt_kernel, n_heads=n_heads),
        grid=(n // q,),
        in_specs=[
            pl.BlockSpec((q, LANES), lambda i: (i, 0)),
            pl.BlockSpec((1, LANES), lambda i: (0, 0)),
            pl.BlockSpec((1, LANES), lambda i: (0, 0)),
        ],
        out_specs=[
            pl.BlockSpec((q, LANES), lambda i: (i, 0)),
            pl.BlockSpec((q, LANES), lambda i: (i, 0)),
            pl.BlockSpec((1, LANES, q), lambda i: (i, 0, 0)),
        ],
        out_shape=[
            jax.ShapeDtypeStruct((n, LANES), F32),
            jax.ShapeDtypeStruct((n, LANES), F32),
            jax.ShapeDtypeStruct((n // q, LANES, q), F32),
        ],
        compiler_params=_cp(("parallel",)),
        name="ssd_dt",
    )(dt_raw, pad(dt_bias), pad(-jnp.exp(a_log.astype(F32))))


def _ssd_scan_kernel(x_ref, b_ref, c_ref, dt_ref, cs_ref, cst_ref, dsk_ref, y_ref, st_ref, *,
                     n_heads, heads_per_group, ctx_chunks, lat_chunks):
    g = pl.program_id(1)
    q = SSD_Q
    hd = SSD_HEAD_DIM
    width = heads_per_group * hd
    dot = functools.partial(jnp.dot, preferred_element_type=F32)
    lane = lax.broadcasted_iota(jnp.int32, (q, LANES), 1)
    blk = lax.broadcasted_iota(jnp.int32, (q, width), 1) // hd
    r_i = lax.broadcasted_iota(jnp.int32, (q, q), 0)
    c_i = lax.broadcasted_iota(jnp.int32, (q, q), 1)

    def spread(cols):
        out = jnp.broadcast_to(cols[0], (q, width))
        for hh in range(1, heads_per_group):
            out = jnp.where(blk == hh, cols[hh], out)
        return out

    def chunk(ci, direction, first_pass):
        r0 = ci * q if isinstance(ci, int) else pl.multiple_of(ci * q, q)
        rows = pl.ds(r0, q)
        xb = x_ref[rows, :].astype(F32)
        bm = b_ref[rows, :]
        cm = c_ref[rows, :]
        dt = dt_ref[rows, :]
        cs = cs_ref[rows, :]
        col0 = direction * n_heads + g * heads_per_group
        dt_cols, cs_cols, cs_rows = [], [], []
        for hh in range(heads_per_group):
            sel = lane == col0 + hh
            dt_cols.append(jnp.sum(jnp.where(sel, dt, 0.0), axis=1, keepdims=True))
            cs_cols.append(jnp.sum(jnp.where(sel, cs, 0.0), axis=1, keepdims=True))
            cs_rows.append(cst_ref[ci, pl.ds(col0 + hh, 1), :])
        xdt = xb * spread(dt_cols)
        csf = spread(cs_cols)
        edge = csf[q - 1:q, :] if direction == 0 else csf[0:1, :]
        scores = lax.dot_general(cm, bm, (((1,), (1,)), ((), ())), preferred_element_type=F32)
        keep = (c_i <= r_i) if direction == 0 else (c_i >= r_i)
        xdt_b = xdt.astype(BF16)
        state = st_ref[...]
        y = dot(cm, state.astype(BF16)) * jnp.exp(csf)
        for hh in range(heads_per_group):
            seg = jnp.where(keep, cs_cols[hh] - cs_rows[hh], -jnp.inf)
            pm = (scores * jnp.exp(seg)).astype(BF16)
            y = y + dot(pm, jnp.where(blk == hh, xdt_b, jnp.zeros_like(xdt_b)))
        w = (xdt * jnp.exp(edge - csf)).astype(BF16)
        bt = bm.astype(F32).T.astype(BF16)
        st_ref[...] = jnp.exp(edge) * state + dot(bt, w)
        if first_pass:
            y_ref[rows, :] = y + dsk_ref[0] * xb
        else:
            y_ref[rows, :] = y_ref[rows, :] + y

    for direction in range(2):
        st_ref[...] = jnp.zeros_like(st_ref)
        for ci in range(ctx_chunks):
            cc = ci if direction == 0 else ctx_chunks - 1 - ci
            chunk(cc, direction, direction == 0)

        def body(i, carry, direction=direction):
            cc = ctx_chunks + (i if direction == 0 else lat_chunks - 1 - i)
            chunk(cc, direction, direction == 0)
            return carry

        lax.fori_loop(0, lat_chunks, body, 0)


def _ssd_scan(xbc, dt, cs, cst, d_skip, geo, *, n_heads, n_groups):
    n = xbc.shape[0]
    rpb = geo.rows_per_batch
    hpg = n_heads // n_groups
    width = hpg * SSD_HEAD_DIM
    d_inner = n_heads * SSD_HEAD_DIM
    xblocks = d_inner // width
    bblocks = d_inner // SSD_STATE
    cpb = rpb // SSD_Q
    dsk = jnp.repeat(d_skip.astype(F32), SSD_HEAD_DIM).reshape(n_groups, 1, width)
    kern = functools.partial(_ssd_scan_kernel, n_heads=n_heads, heads_per_group=hpg,
                             ctx_chunks=geo.n_ctx // SSD_Q, lat_chunks=geo.n_lat // SSD_Q)
    return pl.pallas_call(
        kern,
        grid=(geo.n_batch, n_groups),
        in_specs=[
            pl.BlockSpec((rpb, width), lambda b, g: (b, g)),
            pl.BlockSpec((rpb, SSD_STATE), lambda b, g: (b, bblocks + g)),
            pl.BlockSpec((rpb, SSD_STATE), lambda b, g: (b, bblocks + n_groups + g)),
            pl.BlockSpec((rpb, LANES), lambda b, g: (b, 0)),
            pl.BlockSpec((rpb, LANES), lambda b, g: (b, 0)),
            pl.BlockSpec((cpb, LANES, SSD_Q), lambda b, g: (b, 0, 0)),
            pl.BlockSpec((1, 1, width), lambda b, g: (g, 0, 0)),
        ],
        out_specs=pl.BlockSpec((rpb, width), lambda b, g: (b, g)),
        out_shape=jax.ShapeDtypeStruct((n, d_inner), F32),
        scratch_shapes=[pltpu.VMEM((SSD_STATE, width), F32)],
        compiler_params=_cp(("parallel", "arbitrary")),
        name="ssd_scan",
    )(xbc, xbc, xbc, dt, cs, cst, dsk)


def _ssd_out_kernel(y_ref, z_ref, g_ref, w_ref, o_ref, *, n_groups):
    z = z_ref[...].astype(F32)
    gy = y_ref[...] * (z * jax.nn.sigmoid(z))
    gw = gy.shape[1] // n_groups
    acc = jnp.zeros(o_ref.shape, F32)
    for gi in range(n_groups):
        sl = slice(gi * gw, (gi + 1) * gw)
        ng = (_rms(gy[:, sl]) * g_ref[:, sl]).astype(BF16)
        acc = acc + jnp.dot(ng, w_ref[sl, :], preferred_element_type=F32)
    o_ref[...] = acc


def _ssd_out(y, z, norm_g, out_w, n_groups, tm=ROW_TILE):
    n, di = y.shape
    d = out_w.shape[1]
    return pl.pallas_call(
        functools.partial(_ssd_out_kernel, n_groups=n_groups),
        grid=(n // tm,),
        in_specs=[
            pl.BlockSpec((tm, di), lambda i: (i, 0)),
            pl.BlockSpec((tm, di), lambda i: (i, 0)),
            pl.BlockSpec((1, di), lambda i: (0, 0)),
            pl.BlockSpec((di, d), lambda i: (0, 0)),
        ],
        out_specs=pl.BlockSpec((tm, d), lambda i: (i, 0)),
        out_shape=jax.ShapeDtypeStruct((n, d), F32),
        compiler_params=_cp(("parallel",)),
        name="ssd_out",
    )(y, z, norm_g.reshape(1, di).astype(F32), out_w.astype(BF16))


def _ssd_mixer(x, mods, g1, prm, geo):
    in_w, conv_w, conv_b, dt_bias, a_log, d_skip, norm_g, out_w = prm
    n_heads = a_log.shape[1]
    d_inner = out_w.shape[0]
    n_bc = conv_w.shape[1] - d_inner
    n_groups = n_bc // (2 * SSD_STATE)
    z, xbc_raw, dt_raw = _ssd_inproj(x, mods, g1, in_w, d_inner, n_bc, geo)
    xbc = _ssd_conv(xbc_raw, conv_w, conv_b, geo)
    dt, cs, cst = _ssd_dt(dt_raw, dt_bias, a_log, n_heads)
    y = _ssd_scan(xbc, dt, cs, cst, d_skip, geo, n_heads=n_heads, n_groups=n_groups)
    return _ssd_out(y, z, norm_g, out_w, n_groups)


DA_HEAD_DIM = 64
ROPE_BASE = 10000.0
GRID_W = 64


def _rope_tables(geo, tm):
    hd = DA_HEAD_DIM
    n_freq = hd // 4
    t = jnp.arange(geo.n_lat, dtype=F32)
    row, col = jnp.floor(t / GRID_W), jnp.mod(t, GRID_W)
    inv = ROPE_BASE ** (-jnp.arange(n_freq, dtype=F32) / n_freq)
    lane = jnp.arange(LANES)
    within = lane % hd
    freq = inv[within % n_freq]
    pos = jnp.where((within < hd // 2)[None, :], row[:, None], col[:, None])
    ang = pos * freq[None, :]
    sign = jnp.where((within % (2 * n_freq)) < n_freq, -1.0, 1.0)
    cos = jnp.concatenate([jnp.cos(ang), jnp.ones((tm, LANES), F32)], axis=0)
    sin = jnp.concatenate([jnp.sin(ang) * sign[None, :], jnp.zeros((tm, LANES), F32)], axis=0)
    return cos, sin


def _da_qkv_kernel(x_ref, mod_ref, g_ref, w_ref, cos_ref, sin_ref, qg_ref, kg_ref, seg_ref,
                   q_ref, k_ref, v_ref):
    m = mod_ref[0]
    h = (_rms(x_ref[...]) * g_ref[...] * (1.0 + m[1:2, :]) + m[0:1, :]).astype(BF16)
    width = q_ref.shape[1]
    tm = h.shape[0]
    cos = cos_ref[...]
    sin = sin_ref[...]
    lane = lax.broadcasted_iota(jnp.int32, (tm, LANES), 1)
    n_freq = DA_HEAD_DIM // 4
    first_half = (lane % (2 * n_freq)) < n_freq
    scale = DA_HEAD_DIM ** -0.5
    v_ref[...] = jnp.dot(h, w_ref[:, 2 * width:], preferred_element_type=F32).astype(v_ref.dtype)
    for o_ref, off, gain, mult in ((q_ref, 0, qg_ref, scale), (k_ref, width, kg_ref, 1.0)):
        for blk in range(width // LANES):
            cols = slice(off + blk * LANES, off + (blk + 1) * LANES)
            a = jnp.dot(h, w_ref[:, cols], preferred_element_type=F32)
            ms = jnp.dot((a * a).astype(BF16), seg_ref[...], preferred_element_type=F32)
            a = a * lax.rsqrt(ms + EPS) * gain[...]
            partner = jnp.where(first_half, pltpu.roll(a, LANES - n_freq, 1), pltpu.roll(a, n_freq, 1))
            a = a * cos + partner * sin
            o_ref[:, blk * LANES:(blk + 1) * LANES] = (a * mult).astype(o_ref.dtype)


def _da_qkv(x, mods, g1, qkv_w, q_g, k_g, geo, tm=MOD_TILE):
    n, d = x.shape
    width = qkv_w.shape[1] // 3
    nt, _, mod = _tile_maps(geo, tm, False)
    tpb, ct, lt = geo.tiles(tm)
    cos, sin = _rope_tables(geo, tm)
    tab = lambda i: (jnp.where(i % tpb < ct, lt, i % tpb - ct), 0)
    lane = jnp.arange(LANES)
    seg = jnp.where((lane[:, None] // DA_HEAD_DIM) == (lane[None, :] // DA_HEAD_DIM), 1.0 / DA_HEAD_DIM, 0.0)
    tile2 = lambda v: jnp.tile(v.astype(F32), LANES // DA_HEAD_DIM).reshape(1, LANES)
    const = lambda i: (0, 0)
    row = lambda i: (i, 0)
    return pl.pallas_call(
        _da_qkv_kernel,
        grid=(nt,),
        in_specs=[
            pl.BlockSpec((tm, d), row),
            pl.BlockSpec((1, 8, d), lambda i: (mod(i), 0, 0)),
            pl.BlockSpec((1, d), const),
            pl.BlockSpec((d, 3 * width), const),
            pl.BlockSpec((tm, LANES), tab),
            pl.BlockSpec((tm, LANES), tab),
            pl.BlockSpec((1, LANES), const),
            pl.BlockSpec((1, LANES), const),
            pl.BlockSpec((LANES, LANES), const),
        ],
        out_specs=[pl.BlockSpec((tm, width), row)] * 3,
        out_shape=[jax.ShapeDtypeStruct((n, width), BF16)] * 3,
        compiler_params=_cp(("parallel",)),
        name="da_qkv",
    )(x, mods, g1.reshape(1, d), qkv_w.astype(BF16), cos, sin, tile2(q_g), tile2(k_g), seg.astype(BF16))


def _da_attn_kernel(lam_ref, q_ref, k_ref, v_ref, sg_ref, o_ref, *, n_ctx, ctx_tiles, out_scale):
    qi = pl.program_id(2)
    lam = lam_ref[0]
    q = q_ref[...]
    lane = lax.broadcasted_iota(jnp.int32, q.shape, 1)
    zero = jnp.zeros_like(q)
    q0 = jnp.where(lane < DA_HEAD_DIM, q, zero)
    q1 = jnp.where(lane < DA_HEAD_DIM, zero, q)
    nt = (((1,), (1,)), ((), ()))

    def attend(k, v):
        def probs(qc):
            s = lax.dot_general(qc, k, nt, preferred_element_type=F32)
            e = jnp.exp(s - jnp.max(s, axis=-1, keepdims=True))
            return e, 1.0 / jnp.sum(e, axis=-1, keepdims=True)

        e0, r0 = probs(q0)
        e1, r1 = probs(q1)
        a = (e0 * r0 - e1 * (lam * r1)).astype(BF16)
        o = jnp.dot(a, v, preferred_element_type=F32)
        o_ref[...] = (_rms(o) * sg_ref[...] * out_scale).astype(o_ref.dtype)

    @pl.when(qi < ctx_tiles)
    def _():
        attend(k_ref[:n_ctx, :], v_ref[:n_ctx, :])

    @pl.when(qi >= ctx_tiles)
    def _():
        attend(k_ref[...], v_ref[...])


def _da_attn(q, k, v, lam, sub_g, lam_init, geo, tq=MOD_TILE):
    n, width = q.shape
    hw = 2 * DA_HEAD_DIM
    n_heads = width // hw
    rpb = geo.rows_per_batch
    tpb, ct, _ = geo.tiles(tq)
    grid_spec = pltpu.PrefetchScalarGridSpec(
        num_scalar_prefetch=0,
        grid=(geo.n_batch, n_heads, tpb),
        in_specs=[
            pl.BlockSpec(memory_space=pltpu.SMEM),
            pl.BlockSpec((tq, hw), lambda b, h, i: (b * tpb + i, h)),
            pl.BlockSpec((rpb, hw), lambda b, h, i: (b, h)),
            pl.BlockSpec((rpb, hw), lambda b, h, i: (b, h)),
            pl.BlockSpec((1, hw), lambda b, h, i: (0, 0)),
        ],
        out_specs=pl.BlockSpec((tq, hw), lambda b, h, i: (b * tpb + i, h)),
    )
    return pl.pallas_call(
        functools.partial(_da_attn_kernel, n_ctx=geo.n_ctx, ctx_tiles=ct, out_scale=1.0 - lam_init),
        grid_spec=grid_spec,
        out_shape=jax.ShapeDtypeStruct((n, width), BF16),
        compiler_params=_cp(("parallel", "parallel", "arbitrary")),
        name="da_attn",
    )(lam.reshape(1).astype(F32), q, k, v, sub_g.reshape(1, hw).astype(F32))


def _matmul_kernel(x_ref, w_ref, o_ref):
    o_ref[...] = jnp.dot(x_ref[...], w_ref[...], preferred_element_type=F32)


def _matmul(x, w, tm=ROW_TILE):
    n, kd = x.shape
    d = w.shape[1]
    return pl.pallas_call(
        _matmul_kernel,
        grid=(n // tm,),
        in_specs=[pl.BlockSpec((tm, kd), lambda i: (i, 0)), pl.BlockSpec((kd, d), lambda i: (0, 0))],
        out_specs=pl.BlockSpec((tm, d), lambda i: (i, 0)),
        out_shape=jax.ShapeDtypeStruct((n, d), F32),
        compiler_params=_cp(("parallel",)),
        name="da_out",
    )(x, w)


def _da_mixer(x, mods, g1, prm, layer_idx, geo):
    qkv_w, q_g, k_g, lam_vec, sub_g, out_w = prm
    lam_init = 0.8 - 0.6 * math.exp(-0.3 * layer_idx)
    lv = lam_vec.astype(F32)
    lam = jnp.exp(jnp.sum(lv[0] * lv[1])) - jnp.exp(jnp.sum(lv[2] * lv[3])) + lam_init
    q, k, v = _da_qkv(x, mods, g1, qkv_w, q_g, k_g, geo)
    o = _da_attn(q, k, v, lam, sub_g, lam_init, geo)
    return _matmul(o, out_w.astype(BF16))


def kernel(x, c, ctx, c_ctx, mod_w, mod_b, norm1_g, norm2_g, s5_a_re, s5_a_im, s5_log_dt, s5_b_re, s5_b_im, s5_c_re, s5_c_im, s5_d, s5_glu_w, s5_glu_b, ssd_in_w, ssd_conv_w, ssd_conv_b, ssd_dt_bias, ssd_a_log, ssd_d, ssd_norm_g, ssd_out_w, da_qkv_w, da_q_g, da_k_g, da_lam, da_sub_g, da_out_w, moe_router_w, moe_router_b, moe_gu_w, moe_gu_b, moe_dn_w, moe_dn_b):
    bsz, n_lat, d = x.shape
    n_ctx = ctx.shape[1]
    depth = mod_w.shape[0]
    geo = Geo(bsz, n_ctx, n_lat)
    xs = jnp.concatenate([ctx, x], axis=1).reshape(geo.rows, d)
    mods_all = _mods(c, c_ctx, mod_w, mod_b)
    n_mixers = 3
    for i in range(depth):
        last = i == depth - 1
        kind, j = i % n_mixers, i // n_mixers
        mods = mods_all[i]
        if kind == 0:
            prm = (s5_a_re[j], s5_a_im[j], s5_log_dt[j], s5_b_re[j], s5_b_im[j], s5_c_re[j], s5_c_im[j],
                   s5_d[j], s5_glu_w[j], s5_glu_b[j])
            y = _s5_mixer(xs, mods, norm1_g[i], prm, geo)
        elif kind == 1:
            prm = (ssd_in_w[j], ssd_conv_w[j], ssd_conv_b[j], ssd_dt_bias[j], ssd_a_log[j], ssd_d[j],
                   ssd_norm_g[j], ssd_out_w[j])
            y = _ssd_mixer(xs, mods, norm1_g[i], prm, geo)
        else:
            prm = (da_qkv_w[j], da_q_g[j], da_k_g[j], da_lam[j], da_sub_g[j], da_out_w[j])
            y = _da_mixer(xs, mods, norm1_g[i], prm, i, geo)
        xs = _moe_layer(xs, y, mods, norm2_g[i], moe_router_w[i], moe_router_b[i], moe_gu_w[i], moe_gu_b[i],
                        moe_dn_w[i], moe_dn_b[i], geo, last)
    return xs.reshape(bsz, n_lat, d)
```

```python
import functools
import math
from typing import NamedTuple

import jax
import jax.numpy as jnp
from jax import lax
from jax.experimental import pallas as pl
from jax.experimental.pallas import tpu as pltpu

F32 = jnp.float32
BF16 = jnp.bfloat16
EPS = 1e-6
HP = lax.Precision.HIGHEST

LANES = 128
VMEM_LIMIT = 56 * 1024 * 1024

N_EXPERTS = 32
TOP_K = 4
SWIGLU_ALPHA = 1.702
SWIGLU_LIMIT = 7.0
MOE_BLK = 512
MOD_TILE = 256
ROW_TILE = 512


class Geo(NamedTuple):
    n_batch: int
    n_ctx: int
    n_lat: int

    @property
    def rows_per_batch(self):
        return self.n_ctx + self.n_lat

    @property
    def rows(self):
        return self.n_batch * self.rows_per_batch

    def tiles(self, tm):
        return self.rows_per_batch // tm, self.n_ctx // tm, self.n_lat // tm


def _cp(sem, vmem=VMEM_LIMIT):
    return pltpu.CompilerParams(dimension_semantics=sem, vmem_limit_bytes=vmem)


def _tile_maps(geo, tm, lat_only):
    tpb, ct, lt = geo.tiles(tm)
    if lat_only:
        src = lambda i: (i // lt) * tpb + ct + i % lt
        mod = lambda i: i // lt
        return geo.n_batch * lt, src, mod
    mod = lambda i: jnp.where(i % tpb < ct, geo.n_batch, i // tpb)
    return geo.n_batch * tpb, (lambda i: i), mod


def _rms(x):
    return x * lax.rsqrt(jnp.mean(x * x, axis=-1, keepdims=True) + EPS)


def _mods_kernel(s_ref, w_ref, b_ref, o_ref):
    s = s_ref[...]
    s = s * jax.nn.sigmoid(s)
    o_ref[0] = jnp.dot(s, w_ref[0], preferred_element_type=F32, precision=HP) + b_ref[0]


def _mods(c, c_ctx, mod_w, mod_b):
    depth, d, d6 = mod_w.shape
    bsz = c.shape[0]
    rows = -(-(bsz + 1) // 8) * 8
    s = jnp.zeros((rows, d), F32).at[:bsz].set(c).at[bsz].set(c_ctx)
    nj = d6 // d
    out = pl.pallas_call(
        _mods_kernel,
        grid=(depth, nj),
        in_specs=[
            pl.BlockSpec((rows, d), lambda l, j: (0, 0)),
            pl.BlockSpec((1, d, d), lambda l, j: (l, 0, j)),
            pl.BlockSpec((1, 1, d), lambda l, j: (l, 0, j)),
        ],
        out_specs=pl.BlockSpec((1, rows, d), lambda l, j: (l, 0, j)),
        out_shape=jax.ShapeDtypeStruct((depth, rows, d6), F32),
        compiler_params=_cp(("arbitrary", "arbitrary")),
        name="mods",
    )(s, mod_w, mod_b.reshape(depth, 1, d6))
    out = out[:, :bsz + 1].reshape(depth, bsz + 1, nj, d)
    return jnp.pad(out, ((0, 0), (0, 0), (0, 8 - nj), (0, 0)))


def _to_row_tiles(ref, val):
    rows, d = val.shape
    sub = d // LANES
    for s in range(sub):
        ref[pl.ds(s, rows, stride=sub), :] = val[:, s * LANES:(s + 1) * LANES]


def _from_row_tiles(ref, rows, sub, dtype=None):
    parts = [ref[pl.ds(s, rows, stride=sub), :] for s in range(sub)]
    if dtype is not None:
        parts = [p.astype(dtype) for p in parts]
    return jnp.concatenate(parts, axis=1)


def _route_kernel(x_ref, y_ref, mod_ref, g_ref, rw_ref, rb_ref, tri_ref,
                  xn_ref, h_ref, route_ref, cnt_ref, base_ref):
    i = pl.program_id(0)

    @pl.when(i == 0)
    def _():
        base_ref[...] = jnp.zeros_like(base_ref)

    m = mod_ref[0]
    x = x_ref[...] + m[2:3, :] * y_ref[...]
    xn_ref[...] = x
    h = _rms(x) * g_ref[...] * (1.0 + m[4:5, :]) + m[3:4, :]
    _to_row_tiles(h_ref, h)
    logits = jnp.dot(h, rw_ref[...], preferred_element_type=F32, precision=HP) + rb_ref[...]
    tm = logits.shape[0]
    lane = lax.broadcasted_iota(jnp.int32, (tm, LANES), 1)
    neg = jnp.float32(-jnp.inf)
    work = jnp.where(lane < N_EXPERTS, logits, neg)
    vals, ohs, idxs = [], [], []
    for _ in range(TOP_K):
        mk = jnp.max(work, axis=-1, keepdims=True)
        ik = jnp.min(jnp.where(work == mk, lane, LANES), axis=-1, keepdims=True)
        oh = lane == ik
        work = jnp.where(oh, neg, work)
        vals.append(mk)
        ohs.append(oh)
        idxs.append(ik)
    es = [jnp.exp(v - vals[0]) for v in vals]
    den = es[0] + es[1] + es[2] + es[3]
    onehot = jnp.zeros((tm, LANES), F32)
    for oh in ohs:
        onehot = onehot + jnp.where(oh, 1.0, 0.0)
    prefix = jnp.dot(tri_ref[...], onehot.astype(BF16), preferred_element_type=F32)
    pos = prefix + base_ref[0:1, :]
    out = jnp.zeros((tm, LANES), F32)
    for k in range(TOP_K):
        rank = jnp.sum(jnp.where(ohs[k], pos, 0.0), axis=-1, keepdims=True)
        out = out + jnp.where(lane == k, idxs[k].astype(F32), 0.0)
        out = out + jnp.where(lane == TOP_K + k, rank, 0.0)
        out = out + jnp.where(lane == 2 * TOP_K + k, es[k] / den, 0.0)
    route_ref[...] = out
    newbase = base_ref[0:1, :] + jnp.sum(onehot, axis=0, keepdims=True)
    base_ref[...] = jnp.broadcast_to(newbase, base_ref.shape)
    cnt_ref[...] = jnp.broadcast_to(newbase, cnt_ref.shape)


def _route(x, y, mods, g2, rw, rb, geo, lat_only, tm=MOD_TILE):
    d = x.shape[1]
    sub = d // LANES
    nt, src, mod = _tile_maps(geo, tm, lat_only)
    n = nt * tm
    rw_p = jnp.zeros((d, LANES), F32).at[:, :N_EXPERTS].set(rw)
    rb_p = jnp.zeros((1, LANES), F32).at[0, :N_EXPERTS].set(rb)
    tri = (jnp.arange(tm)[:, None] > jnp.arange(tm)[None, :]).astype(BF16)
    const = lambda i: (0, 0)
    return pl.pallas_call(
        _route_kernel,
        grid=(nt,),
        in_specs=[
            pl.BlockSpec((tm, d), lambda i: (src(i), 0)),
            pl.BlockSpec((tm, d), lambda i: (src(i), 0)),
            pl.BlockSpec((1, 8, d), lambda i: (mod(i), 0, 0)),
            pl.BlockSpec((1, d), const),
            pl.BlockSpec((d, LANES), const),
            pl.BlockSpec((1, LANES), const),
            pl.BlockSpec((tm, tm), const),
        ],
        out_specs=[
            pl.BlockSpec((tm, d), lambda i: (i, 0)),
            pl.BlockSpec((tm * sub, LANES), lambda i: (i, 0)),
            pl.BlockSpec((tm, LANES), lambda i: (i, 0)),
            pl.BlockSpec((8, LANES), const),
        ],
        out_shape=[
            jax.ShapeDtypeStruct((n, d), F32),
            jax.ShapeDtypeStruct((n * sub, LANES), F32),
            jax.ShapeDtypeStruct((n, LANES), F32),
            jax.ShapeDtypeStruct((8, LANES), F32),
        ],
        scratch_shapes=[pltpu.VMEM((8, LANES), F32)],
        compiler_params=_cp(("arbitrary",)),
        name="route",
    )(x, y, mods, g2.reshape(1, d), rw_p, rb_p, tri)


def _dispatch_kernel(last_ref, dest_ref, h_ref, xs_ref, zero_ref, sem, zsem, *, sub, blk):
    i = pl.program_id(0)
    tm = h_ref.shape[0] // sub

    @pl.when(i == 0)
    def _():
        zero_ref[...] = jnp.zeros_like(zero_ref)

        def zcopy(e):
            return pltpu.make_async_copy(zero_ref, xs_ref.at[pl.ds(last_ref[e] * (blk * sub), blk * sub), :], zsem)

        def zstart(e, c):
            @pl.when(last_ref[e] >= 0)
            def _():
                zcopy(e).start()
            return c

        def zwait(e, c):
            @pl.when(last_ref[e] >= 0)
            def _():
                zcopy(e).wait()
            return c

        lax.fori_loop(0, N_EXPERTS, zstart, 0)
        lax.fori_loop(0, N_EXPERTS, zwait, 0)

    def start(r, c):
        for k in range(TOP_K):
            j = r * TOP_K + k
            d = dest_ref[j // LANES, j % LANES]
            pltpu.make_async_copy(h_ref.at[pl.ds(r * sub, sub), :], xs_ref.at[pl.ds(d * sub, sub), :], sem).start()
        return c

    lax.fori_loop(0, tm, start, 0)
    for k in range(TOP_K):
        pltpu.make_async_copy(h_ref, xs_ref.at[pl.ds(0, tm * sub), :], sem).wait()


def _dispatch(h, dest2d, last_blk, n_slots, sub, tm=ROW_TILE):
    n = h.shape[0] // sub
    nt = n // tm
    db = tm * TOP_K // LANES
    grid_spec = pltpu.PrefetchScalarGridSpec(
        num_scalar_prefetch=1,
        grid=(nt,),
        in_specs=[
            pl.BlockSpec((db, LANES), lambda i, lb: (i, 0), memory_space=pltpu.SMEM),
            pl.BlockSpec((tm * sub, LANES), lambda i, lb: (i, 0)),
        ],
        out_specs=pl.BlockSpec(memory_space=pl.ANY),
        scratch_shapes=[pltpu.VMEM((MOE_BLK * sub, LANES), F32), pltpu.SemaphoreType.DMA(()),
                        pltpu.SemaphoreType.DMA(())],
    )
    return pl.pallas_call(
        functools.partial(_dispatch_kernel, sub=sub, blk=MOE_BLK),
        grid_spec=grid_spec,
        out_shape=jax.ShapeDtypeStruct((n_slots * sub, LANES), F32),
        compiler_params=_cp(("arbitrary",)),
        name="dispatch",
    )(last_blk, dest2d, h)


def _expert_kernel(be_ref, na_ref, xs_ref, guw_ref, gub_ref, dnw_ref, dnb_ref, ys_ref):
    j = pl.program_id(0)

    @pl.when(j < na_ref[0])
    def _():
        f, d = dnw_ref.shape[1], dnw_ref.shape[2]
        sub = d // LANES
        x = _from_row_tiles(xs_ref, xs_ref.shape[0] // sub, sub, BF16)
        gu = jnp.dot(x, guw_ref[0], preferred_element_type=F32) + gub_ref[0]
        gate = jnp.minimum(gu[:, :f], SWIGLU_LIMIT)
        up = jnp.clip(gu[:, f:], -SWIGLU_LIMIT, SWIGLU_LIMIT)
        act = (up + 1.0) * gate * jax.nn.sigmoid(SWIGLU_ALPHA * gate)
        y = jnp.dot(act.astype(BF16), dnw_ref[0], preferred_element_type=F32) + dnb_ref[0]
        _to_row_tiles(ys_ref, y)


def _experts(xs, block_e, n_active, guw, gub, dnw, dnb, blk=MOE_BLK):
    e, d, f2 = guw.shape
    f = f2 // 2
    sub = d // LANES
    nb = xs.shape[0] // (blk * sub)
    act_blk = lambda j, be, na: (jnp.minimum(j, na[0] - 1), 0)
    wsel = lambda j, be, na: (be[j], 0, 0)
    grid_spec = pltpu.PrefetchScalarGridSpec(
        num_scalar_prefetch=2,
        grid=(nb,),
        in_specs=[
            pl.BlockSpec((blk * sub, LANES), act_blk),
            pl.BlockSpec((1, d, f2), wsel),
            pl.BlockSpec((1, 1, f2), wsel),
            pl.BlockSpec((1, f, d), wsel),
            pl.BlockSpec((1, 1, d), wsel),
        ],
        out_specs=pl.BlockSpec((blk * sub, LANES), act_blk),
    )
    return pl.pallas_call(
        _expert_kernel,
        grid_spec=grid_spec,
        out_shape=jax.ShapeDtypeStruct(xs.shape, F32),
        compiler_params=_cp(("arbitrary",)),
        name="experts",
    )(block_e, n_active, xs, guw, gub.reshape(e, 1, f2), dnw, dnb.reshape(e, 1, d))


def _combine_kernel(dest_ref, x_ref, route_ref, mod_ref, ys_ref, o_ref, buf, sem):
    tm, d = x_ref.shape
    sub = d // LANES

    def start(r, c):
        for k in range(TOP_K):
            j = r * TOP_K + k
            dd = dest_ref[j // LANES, j % LANES]
            pltpu.make_async_copy(ys_ref.at[pl.ds(dd * sub, sub), :], buf.at[k, pl.ds(r * sub, sub), :], sem).start()
        return c

    lax.fori_loop(0, tm, start, 0)
    for k in range(TOP_K):
        pltpu.make_async_copy(ys_ref.at[pl.ds(0, tm * sub), :], buf.at[k], sem).wait()
    route = route_ref[...]
    gates = [route[:, 2 * TOP_K + k:2 * TOP_K + k + 1] for k in range(TOP_K)]
    scale = mod_ref[0][5:6, :]
    for s in range(sub):
        cols = slice(s * LANES, (s + 1) * LANES)
        f = gates[0] * buf[0, pl.ds(s, tm, stride=sub), :]
        for k in range(1, TOP_K):
            f = f + gates[k] * buf[k, pl.ds(s, tm, stride=sub), :]
        o_ref[:, cols] = x_ref[:, cols] + scale[:, cols] * f


def _combine(x, route, mods, ys, dest2d, geo, lat_only, tm=MOD_TILE):
    n, d = x.shape
    sub = d // LANES
    nt, _, mod = _tile_maps(geo, tm, lat_only)
    db = tm * TOP_K // LANES
    return pl.pallas_call(
        _combine_kernel,
        grid=(nt,),
        in_specs=[
            pl.BlockSpec((db, LANES), lambda i: (i, 0), memory_space=pltpu.SMEM),
            pl.BlockSpec((tm, d), lambda i: (i, 0)),
            pl.BlockSpec((tm, LANES), lambda i: (i, 0)),
            pl.BlockSpec((1, 8, d), lambda i: (mod(i), 0, 0)),
            pl.BlockSpec(memory_space=pl.ANY),
        ],
        out_specs=pl.BlockSpec((tm, d), lambda i: (i, 0)),
        out_shape=jax.ShapeDtypeStruct((n, d), F32),
        scratch_shapes=[pltpu.VMEM((TOP_K, tm * sub, LANES), F32), pltpu.SemaphoreType.DMA(())],
        compiler_params=_cp(("arbitrary",)),
        name="combine",
    )(dest2d, x, route, mods, ys)


def _moe_layer(x, y, mods, g2, rw, rb, guw, gub, dnw, dnb, geo, lat_only):
    xn, h, route, cnt = _route(x, y, mods, g2, rw, rb, geo, lat_only)
    n, d = xn.shape
    sub = d // LANES
    counts = cnt[0, :N_EXPERTS].astype(jnp.int32)
    nblk = (counts + MOE_BLK - 1) // MOE_BLK
    blk_end = jnp.cumsum(nblk)
    pad_start = (blk_end - nblk) * MOE_BLK
    n_blocks = -(-(n * TOP_K) // MOE_BLK) + N_EXPERTS
    block_e = jnp.minimum(jnp.sum(blk_end[None, :] <= jnp.arange(n_blocks)[:, None], axis=1), N_EXPERTS - 1)
    n_active = blk_end[-1:].astype(jnp.int32)
    last_blk = jnp.where(nblk > 0, blk_end - 1, -1).astype(jnp.int32)
    eid = route[:, :TOP_K].astype(jnp.int32)
    rank = route[:, TOP_K:2 * TOP_K].astype(jnp.int32)
    dest = (pad_start[eid] + rank).astype(jnp.int32).reshape(n * TOP_K // LANES, LANES)
    xs = _dispatch(h, dest, last_blk, n_blocks * MOE_BLK, sub)
    ys = _experts(xs, block_e.astype(jnp.int32), n_active, guw.astype(BF16), gub, dnw.astype(BF16), dnb)
    return _combine(xn, route, mods, ys, dest, geo, lat_only)


def _prenorm_kernel(x_ref, mod_ref, g_ref, h_ref):
    m = mod_ref[0]
    h = _rms(x_ref[...]) * g_ref[...] * (1.0 + m[1:2, :]) + m[0:1, :]
    h_ref[...] = h.astype(h_ref.dtype)


def _prenorm(x, mods, g1, geo, tm=MOD_TILE):
    n, d = x.shape
    nt, _, mod = _tile_maps(geo, tm, False)
    return pl.pallas_call(
        _prenorm_kernel,
        grid=(nt,),
        in_specs=[
            pl.BlockSpec((tm, d), lambda i: (i, 0)),
            pl.BlockSpec((1, 8, d), lambda i: (mod(i), 0, 0)),
            pl.BlockSpec((1, d), lambda i: (0, 0)),
        ],
        out_specs=pl.BlockSpec((tm, d), lambda i: (i, 0)),
        out_shape=jax.ShapeDtypeStruct((n, d), BF16),
        compiler_params=_cp(("parallel",)),
        name="prenorm",
    )(x, mods, g1.reshape(1, d))


S5_CHUNK = 16


def _s5_matrices(a_re, a_im, log_dt, b_re, b_im, c_re, c_im):
    L = S5_CHUNK
    a_re = a_re.astype(F32)
    a_im = a_im.astype(F32)
    dt = jnp.exp(log_dt.astype(F32))[..., None]
    mag = jnp.exp(dt * a_re)
    ang = dt * a_im
    ab_re, ab_im = mag * jnp.cos(ang), mag * jnp.sin(ang)
    den = a_re * a_re + a_im * a_im
    num_re = ab_re - 1.0
    co_re = (num_re * a_re + ab_im * a_im) / den
    co_im = (ab_im * a_re - num_re * a_im) / den
    bb_re = co_re[..., None] * b_re - co_im[..., None] * b_im
    bb_im = co_re[..., None] * b_im + co_im[..., None] * b_re
    taus = jnp.arange(L + 1, dtype=F32)[:, None, None, None]
    pmag = jnp.exp(taus * dt * a_re)
    pang = taus * dt * a_im
    pw_re, pw_im = pmag * jnp.cos(pang), pmag * jnp.sin(pang)
    cp_re = c_re[None] * pw_re[:, :, :, None, :] - c_im[None] * pw_im[:, :, :, None, :]
    cp_im = c_re[None] * pw_im[:, :, :, None, :] + c_im[None] * pw_re[:, :, :, None, :]
    kk = (jnp.einsum('tdgjp,dgpi->tdgji', cp_re, bb_re, precision=HP)
          - jnp.einsum('tdgjp,dgpi->tdgji', cp_im, bb_im, precision=HP))
    t_idx = jnp.arange(L)
    lag = t_idx[None, :] - t_idx[:, None]

    def toeplitz(kd, lagm):
        kt = kd[jnp.clip(lagm, 0, L)]
        kt = jnp.where((lagm >= 0)[:, :, None, None, None], kt, 0.0)
        return kt.transpose(2, 0, 4, 1, 3)

    g = a_re.shape[1]
    p = a_re.shape[2]
    j = b_re.shape[3]
    m = (toeplitz(kk[:, 0], lag) + toeplitz(kk[:, 1], -lag)).reshape(g, L * j, L * j)

    def inject(d, steps):
        pr, pi = pw_re[steps, d], pw_im[steps, d]
        re = pr[:, :, :, None] * bb_re[d][None] - pi[:, :, :, None] * bb_im[d][None]
        im = pr[:, :, :, None] * bb_im[d][None] + pi[:, :, :, None] * bb_re[d][None]
        f = lambda z: z.transpose(1, 0, 3, 2).reshape(g, L * j, p)
        return f(re), f(im)

    inj = inject(0, L - 1 - t_idx) + inject(1, t_idx)

    def readout(d, steps):
        re = cp_re[steps, d]
        im = cp_im[steps, d]
        f = lambda z: z.transpose(1, 3, 0, 2).reshape(g, p, L * j)
        return f(re), f(-im)

    rd = readout(0, t_idx + 1) + readout(1, L - t_idx)
    adv = jnp.stack([pw_re[L, 0], pw_im[L, 0], pw_re[L, 1], pw_im[L, 1]], axis=1)
    adv = jnp.pad(adv, ((0, 0), (0, 4), (0, 0)))
    bf = lambda z: z.astype(BF16)
    return bf(m), [bf(z) for z in inj], [bf(z) for z in rd], adv


def _s5_kernel(u_ref, m_ref, if_re, if_im, ib_re, ib_im, rf_re, rf_im, rb_re, rb_im, adv_ref, y_ref,
               sf_re, sf_im, sb_re, sb_im, *, n_batch, ctx_chunks, lat_chunks):
    u = u_ref[0]
    dot = functools.partial(jnp.dot, preferred_element_type=F32)
    sf_re[...] = dot(u, if_re[0])
    sf_im[...] = dot(u, if_im[0])
    sb_re[...] = dot(u, ib_re[0])
    sb_im[...] = dot(u, ib_im[0])
    adv = adv_ref[0]
    p = adv.shape[-1]
    nb = n_batch

    def scan(s_re_ref, s_im_ref, a_re, a_im, order):
        a_re = jnp.broadcast_to(a_re, (nb, p))
        a_im = jnp.broadcast_to(a_im, (nb, p))

        def step(row, carry):
            c_re, c_im = carry
            row = pl.multiple_of(row, nb)
            l_re = s_re_ref[pl.ds(row, nb), :]
            l_im = s_im_ref[pl.ds(row, nb), :]
            s_re_ref[pl.ds(row, nb), :] = c_re
            s_im_ref[pl.ds(row, nb), :] = c_im
            return (a_re * c_re - a_im * c_im + l_re, a_re * c_im + a_im * c_re + l_im)

        carry = (jnp.zeros((nb, p), F32), jnp.zeros((nb, p), F32))
        for start, count, rev in order:
            def body(i, carry, start=start, count=count, rev=rev):
                c = (count - 1 - i) if rev else i
                return step((start + c) * nb, carry)
            carry = lax.fori_loop(0, count, body, carry)

    scan(sf_re, sf_im, adv[0:1, :], adv[1:2, :], [(0, ctx_chunks + lat_chunks, False)])
    scan(sb_re, sb_im, adv[2:3, :], adv[3:4, :], [(0, ctx_chunks, True), (ctx_chunks, lat_chunks, True)])
    y = dot(u, m_ref[0])
    y = y + dot(sf_re[...].astype(BF16), rf_re[0]) + dot(sf_im[...].astype(BF16), rf_im[0])
    y = y + dot(sb_re[...].astype(BF16), rb_re[0]) + dot(sb_im[...].astype(BF16), rb_im[0])
    y_ref[0] = y.astype(y_ref.dtype)


def _s5_core(h, mats, geo):
    m, inj, rd, adv = mats
    g, lj, _ = m.shape
    p = adv.shape[-1]
    L = S5_CHUNK
    j = lj // L
    nb = geo.n_batch
    nchunk = geo.rows_per_batch // L
    rows = nchunk * nb
    u = h.reshape(nb, nchunk, L, g, j).transpose(3, 1, 0, 2, 4).reshape(g, rows, lj)
    gsel = lambda i: (i, 0, 0)
    y = pl.pallas_call(
        functools.partial(_s5_kernel, n_batch=nb, ctx_chunks=geo.n_ctx // L, lat_chunks=geo.n_lat // L),
        grid=(g,),
        in_specs=[pl.BlockSpec((1, rows, lj), gsel), pl.BlockSpec((1, lj, lj), gsel)]
        + [pl.BlockSpec((1, lj, p), gsel)] * 4 + [pl.BlockSpec((1, p, lj), gsel)] * 4
        + [pl.BlockSpec((1, 8, p), gsel)],
        out_specs=pl.BlockSpec((1, rows, lj), gsel),
        out_shape=jax.ShapeDtypeStruct((g, rows, lj), BF16),
        scratch_shapes=[pltpu.VMEM((rows, p), F32)] * 4,
        compiler_params=_cp(("parallel",)),
        name="s5_core",
    )(u, m, *inj, *rd, adv)
    return y.reshape(g, nchunk, nb, L, j).transpose(2, 1, 3, 0, 4).reshape(nb * nchunk * L, g * j)


def _gelu_tanh(x):
    return 0.5 * x * (1.0 + jnp.tanh(math.sqrt(2.0 / math.pi) * (x + 0.044715 * (x * x * x))))


def _s5_glu_kernel(h_ref, y_ref, d_ref, w_ref, b_ref, o_ref):
    y = d_ref[...] * h_ref[...].astype(F32) + y_ref[...].astype(F32)
    gl = _gelu_tanh(y).astype(BF16)
    z = jnp.dot(gl, w_ref[...], preferred_element_type=F32) + b_ref[...]
    dd = o_ref.shape[1]
    o_ref[...] = z[:, :dd] * jax.nn.sigmoid(z[:, dd:])


def _s5_glu(h, y, d_skip, glu_w, glu_b, tm=ROW_TILE):
    n, d = h.shape
    return pl.pallas_call(
        _s5_glu_kernel,
        grid=(n // tm,),
        in_specs=[
            pl.BlockSpec((tm, d), lambda i: (i, 0)),
            pl.BlockSpec((tm, d), lambda i: (i, 0)),
            pl.BlockSpec((1, d), lambda i: (0, 0)),
            pl.BlockSpec((d, 2 * d), lambda i: (0, 0)),
            pl.BlockSpec((1, 2 * d), lambda i: (0, 0)),
        ],
        out_specs=pl.BlockSpec((tm, d), lambda i: (i, 0)),
        out_shape=jax.ShapeDtypeStruct((n, d), F32),
        compiler_params=_cp(("parallel",)),
        name="s5_glu",
    )(h, y, d_skip.reshape(1, d).astype(F32), glu_w.astype(BF16), glu_b.reshape(1, 2 * d).astype(F32))


def _s5_mixer(x, mods, g1, prm, geo):
    a_re, a_im, log_dt, b_re, b_im, c_re, c_im, d_skip, glu_w, glu_b = prm
    h = _prenorm(x, mods, g1, geo)
    mats = _s5_matrices(a_re, a_im, log_dt, b_re, b_im, c_re, c_im)
    y = _s5_core(h, mats, geo)
    return _s5_glu(h, y, d_skip, glu_w, glu_b)


SSD_Q = 256
SSD_HEAD_DIM = 64
SSD_STATE = 128
SSD_CONV = 5
CONV_HALO = 16


def _ssd_inproj_kernel(x_ref, mod_ref, g_ref, wz_ref, wx_ref, wd_ref, z_ref, xbc_ref, dt_ref):
    m = mod_ref[0]
    h = (_rms(x_ref[...]) * g_ref[...] * (1.0 + m[1:2, :]) + m[0:1, :]).astype(BF16)
    z_ref[...] = jnp.dot(h, wz_ref[...], preferred_element_type=F32).astype(z_ref.dtype)
    xbc_ref[...] = jnp.dot(h, wx_ref[...], preferred_element_type=F32).astype(xbc_ref.dtype)
    dt_ref[...] = jnp.dot(h, wd_ref[...], preferred_element_type=F32)


def _ssd_inproj(x, mods, g1, in_w, d_inner, n_bc, geo, tm=MOD_TILE):
    n, d = x.shape
    nt, _, mod = _tile_maps(geo, tm, False)
    wz = in_w[:, :d_inner].astype(BF16)
    wx = in_w[:, d_inner:2 * d_inner + n_bc].astype(BF16)
    wd = in_w[:, 2 * d_inner + n_bc:]
    wd = jnp.pad(wd, ((0, 0), (0, LANES - wd.shape[1]))).astype(BF16)
    cx = d_inner + n_bc
    const = lambda i: (0, 0)
    row = lambda i: (i, 0)
    return pl.pallas_call(
        _ssd_inproj_kernel,
        grid=(nt,),
        in_specs=[
            pl.BlockSpec((tm, d), row),
            pl.BlockSpec((1, 8, d), lambda i: (mod(i), 0, 0)),
            pl.BlockSpec((1, d), const),
            pl.BlockSpec((d, d_inner), const),
            pl.BlockSpec((d, cx), const),
            pl.BlockSpec((d, LANES), const),
        ],
        out_specs=[pl.BlockSpec((tm, d_inner), row), pl.BlockSpec((tm, cx), row), pl.BlockSpec((tm, LANES), row)],
        out_shape=[
            jax.ShapeDtypeStruct((n, d_inner), BF16),
            jax.ShapeDtypeStruct((n, cx), BF16),
            jax.ShapeDtypeStruct((n, LANES), F32),
        ],
        compiler_params=_cp(("parallel",)),
        name="ssd_inproj",
    )(x, mods, g1.reshape(1, d), wz, wx, wd)


def _ssd_conv_kernel(prev_ref, cur_ref, next_ref, w_ref, b_ref, o_ref, *, tiles_per_batch, ctx_tiles):
    j = pl.program_id(0) % tiles_per_batch
    first = jnp.logical_or(j == 0, j == ctx_tiles)
    last = jnp.logical_or(j == ctx_tiles - 1, j == tiles_per_batch - 1)
    half = CONV_HALO // 2
    cur = cur_ref[...].astype(F32)
    tt = cur.shape[0]
    prev = jnp.where(first, 0.0, prev_ref[...].astype(F32)[half:, :])
    nxt = jnp.where(last, 0.0, next_ref[...].astype(F32)[:half, :])
    ext = jnp.concatenate([prev, cur, nxt], axis=0)
    w = w_ref[...]
    pad = (SSD_CONV - 1) // 2
    acc = jnp.broadcast_to(b_ref[...], cur.shape)
    for k in range(SSD_CONV):
        off = half + k - pad
        acc = acc + w[k:k + 1, :] * ext[off:off + tt, :]
    o_ref[...] = (acc * jax.nn.sigmoid(acc)).astype(o_ref.dtype)


def _ssd_conv(xbc, conv_w, conv_b, geo, tt=MOD_TILE, tc=2048):
    n, cx = xbc.shape
    tpb, ct, _ = geo.tiles(tt)
    nt = n // tt
    hb = tt // CONV_HALO
    nhb = n // CONV_HALO
    wpad = jnp.pad(conv_w, ((0, 8 - conv_w.shape[0]), (0, 0))).astype(F32)
    return pl.pallas_call(
        functools.partial(_ssd_conv_kernel, tiles_per_batch=tpb, ctx_tiles=ct),
        grid=(nt, cx // tc),
        in_specs=[
            pl.BlockSpec((CONV_HALO, tc), lambda i, c: (jnp.maximum(i * hb - 1, 0), c)),
            pl.BlockSpec((tt, tc), lambda i, c: (i, c)),
            pl.BlockSpec((CONV_HALO, tc), lambda i, c: (jnp.minimum((i + 1) * hb, nhb - 1), c)),
            pl.BlockSpec((8, tc), lambda i, c: (0, c)),
            pl.BlockSpec((1, tc), lambda i, c: (0, c)),
        ],
        out_specs=pl.BlockSpec((tt, tc), lambda i, c: (i, c)),
        out_shape=jax.ShapeDtypeStruct((n, cx), BF16),
        compiler_params=_cp(("parallel", "parallel")),
        name="ssd_conv",
    )(xbc, xbc, xbc, wpad, conv_b.reshape(1, cx).astype(F32))


def _ssd_dt_kernel(raw_ref, bias_ref, a_ref, dt_ref, cs_ref, cst_ref, *, n_heads):
    q = raw_ref.shape[0]
    dt = jax.nn.softplus(raw_ref[...] + bias_ref[...])
    dta = dt * a_ref[...]
    r = lax.broadcasted_iota(jnp.int32, (q, q), 0)
    c = lax.broadcasted_iota(jnp.int32, (q, q), 1)
    lower = jnp.where(c <= r, 1.0, 0.0)
    upper = jnp.where(c >= r, 1.0, 0.0)
    cs_f = jnp.dot(lower, dta, preferred_element_type=F32, precision=HP)
    cs_b = jnp.dot(upper, dta, preferred_element_type=F32, precision=HP)
    lane = lax.broadcasted_iota(jnp.int32, dt.shape, 1)
    cs = jnp.where(lane < n_heads, cs_f, cs_b)
    dt_ref[...] = dt
    cs_ref[...] = cs
    cst_ref[0] = cs.T


def _ssd_dt(dt_raw, dt_bias, a_log, n_heads):
    n = dt_raw.shape[0]
    q = SSD_Q
    pad = lambda v: jnp.pad(v.reshape(1, -1).astype(F32), ((0, 0), (0, LANES - v.size)))
    return pl.pallas_call(
        functools.partial(_ssd_dt_kernel, n_heads=n_heads),
        grid=(n // q,),
        in_specs=[
            pl.BlockSpec((q, LANES), lambda i: (i, 0)),
            pl.BlockSpec((1, LANES), lambda i: (0, 0)),
            pl.BlockSpec((1, LANES), lambda i: (0, 0)),
        ],
        out_specs=[
            pl.BlockSpec((q, LANES), lambda i: (i, 0)),
            pl.BlockSpec((q, LANES), lambda i: (i, 0)),
            pl.BlockSpec((1, LANES, q), lambda i: (i, 0, 0)),
        ],
        out_shape=[
            jax.ShapeDtypeStruct((n, LANES), F32),
            jax.ShapeDtypeStruct((n, LANES), F32),
            jax.ShapeDtypeStruct((n // q, LANES, q), F32),
        ],
        compiler_params=_cp(("parallel",)),
        name="ssd_dt",
    )(dt_raw, pad(dt_bias), pad(-jnp.exp(a_log.astype(F32))))


def _ssd_scan_kernel(x_ref, b_ref, c_ref, dt_ref, cs_ref, cst_ref, dsk_ref, y_ref, st_ref, *,
                     n_heads, heads_per_group, ctx_chunks, lat_chunks):
    g = pl.program_id(1)
    q = SSD_Q
    hd = SSD_HEAD_DIM
    width = heads_per_group * hd
    dot = functools.partial(jnp.dot, preferred_element_type=F32)
    lane = lax.broadcasted_iota(jnp.int32, (q, LANES), 1)
    blk = lax.broadcasted_iota(jnp.int32, (q, width), 1) // hd
    r_i = lax.broadcasted_iota(jnp.int32, (q, q), 0)
    c_i = lax.broadcasted_iota(jnp.int32, (q, q), 1)

    def spread(cols):
        out = jnp.broadcast_to(cols[0], (q, width))
        for hh in range(1, heads_per_group):
            out = jnp.where(blk == hh, cols[hh], out)
        return out

    def chunk(ci, direction, first_pass):
        r0 = ci * q if isinstance(ci, int) else pl.multiple_of(ci * q, q)
        rows = pl.ds(r0, q)
        xb = x_ref[rows, :].astype(F32)
        bm = b_ref[rows, :]
        cm = c_ref[rows, :]
        dt = dt_ref[rows, :]
        cs = cs_ref[rows, :]
        col0 = direction * n_heads + g * heads_per_group
        dt_cols, cs_cols, cs_rows = [], [], []
        for hh in range(heads_per_group):
            sel = lane == col0 + hh
            dt_cols.append(jnp.sum(jnp.where(sel, dt, 0.0), axis=1, keepdims=True))
            cs_cols.append(jnp.sum(jnp.where(sel, cs, 0.0), axis=1, keepdims=True))
            cs_rows.append(cst_ref[ci, pl.ds(col0 + hh, 1), :])
        xdt = xb * spread(dt_cols)
        csf = spread(cs_cols)
        edge = csf[q - 1:q, :] if direction == 0 else csf[0:1, :]
        scores = lax.dot_general(cm, bm, (((1,), (1,)), ((), ())), preferred_element_type=F32)
        keep = (c_i <= r_i) if direction == 0 else (c_i >= r_i)
        xdt_b = xdt.astype(BF16)
        state = st_ref[...]
        y = dot(cm, state.astype(BF16)) * jnp.exp(csf)
        for hh in range(heads_per_group):
            seg = jnp.where(keep, cs_cols[hh] - cs_rows[hh], -jnp.inf)
            pm = (scores * jnp.exp(seg)).astype(BF16)
            y = y + dot(pm, jnp.where(blk == hh, xdt_b, jnp.zeros_like(xdt_b)))
        w = (xdt * jnp.exp(edge - csf)).astype(BF16)
        bt = bm.astype(F32).T.astype(BF16)
        st_ref[...] = jnp.exp(edge) * state + dot(bt, w)
        if first_pass:
            y_ref[rows, :] = y + dsk_ref[0] * xb
        else:
            y_ref[rows, :] = y_ref[rows, :] + y

    for direction in range(2):
        st_ref[...] = jnp.zeros_like(st_ref)
        for ci in range(ctx_chunks):
            cc = ci if direction == 0 else ctx_chunks - 1 - ci
            chunk(cc, direction, direction == 0)

        def body(i, carry, direction=direction):
            cc = ctx_chunks + (i if direction == 0 else lat_chunks - 1 - i)
            chunk(cc, direction, direction == 0)
            return carry

        lax.fori_loop(0, lat_chunks, body, 0)


def _ssd_scan(xbc, dt, cs, cst, d_skip, geo, *, n_heads, n_groups):
    n = xbc.shape[0]
    rpb = geo.rows_per_batch
    hpg = n_heads // n_groups
    width = hpg * SSD_HEAD_DIM
    d_inner = n_heads * SSD_HEAD_DIM
    xblocks = d_inner // width
    bblocks = d_inner // SSD_STATE
    cpb = rpb // SSD_Q
    dsk = jnp.repeat(d_skip.astype(F32), SSD_HEAD_DIM).reshape(n_groups, 1, width)
    kern = functools.partial(_ssd_scan_kernel, n_heads=n_heads, heads_per_group=hpg,
                             ctx_chunks=geo.n_ctx // SSD_Q, lat_chunks=geo.n_lat // SSD_Q)
    return pl.pallas_call(
        kern,
        grid=(geo.n_batch, n_groups),
        in_specs=[
            pl.BlockSpec((rpb, width), lambda b, g: (b, g)),
            pl.BlockSpec((rpb, SSD_STATE), lambda b, g: (b, bblocks + g)),
            pl.BlockSpec((rpb, SSD_STATE), lambda b, g: (b, bblocks + n_groups + g)),
            pl.BlockSpec((rpb, LANES), lambda b, g: (b, 0)),
            pl.BlockSpec((rpb, LANES), lambda b, g: (b, 0)),
            pl.BlockSpec((cpb, LANES, SSD_Q), lambda b, g: (b, 0, 0)),
            pl.BlockSpec((1, 1, width), lambda b, g: (g, 0, 0)),
        ],
        out_specs=pl.BlockSpec((rpb, width), lambda b, g: (b, g)),
        out_shape=jax.ShapeDtypeStruct((n, d_inner), F32),
        scratch_shapes=[pltpu.VMEM((SSD_STATE, width), F32)],
        compiler_params=_cp(("parallel", "arbitrary")),
        name="ssd_scan",
    )(xbc, xbc, xbc, dt, cs, cst, dsk)


def _ssd_out_kernel(y_ref, z_ref, g_ref, w_ref, o_ref, *, n_groups):
    z = z_ref[...].astype(F32)
    gy = y_ref[...] * (z * jax.nn.sigmoid(z))
    gw = gy.shape[1] // n_groups
    acc = jnp.zeros(o_ref.shape, F32)
    for gi in range(n_groups):
        sl = slice(gi * gw, (gi + 1) * gw)
        ng = (_rms(gy[:, sl]) * g_ref[:, sl]).astype(BF16)
        acc = acc + jnp.dot(ng, w_ref[sl, :], preferred_element_type=F32)
    o_ref[...] = acc


def _ssd_out(y, z, norm_g, out_w, n_groups, tm=ROW_TILE):
    n, di = y.shape
    d = out_w.shape[1]
    return pl.pallas_call(
        functools.partial(_ssd_out_kernel, n_groups=n_groups),
        grid=(n // tm,),
        in_specs=[
            pl.BlockSpec((tm, di), lambda i: (i, 0)),
            pl.BlockSpec((tm, di), lambda i: (i, 0)),
            pl.BlockSpec((1, di), lambda i: (0, 0)),
            pl.BlockSpec((di, d), lambda i: (0, 0)),
        ],
        out_specs=pl.BlockSpec((tm, d), lambda i: (i, 0)),
        out_shape=jax.ShapeDtypeStruct((n, d), F32),
        compiler_params=_cp(("parallel",)),
        name="ssd_out",
    )(y, z, norm_g.reshape(1, di).astype(F32), out_w.astype(BF16))


def _ssd_mixer(x, mods, g1, prm, geo):
    in_w, conv_w, conv_b, dt_bias, a_log, d_skip, norm_g, out_w = prm
    n_heads = a_log.shape[1]
    d_inner = out_w.shape[0]
    n_bc = conv_w.shape[1] - d_inner
    n_groups = n_bc // (2 * SSD_STATE)
    z, xbc_raw, dt_raw = _ssd_inproj(x, mods, g1, in_w, d_inner, n_bc, geo)
    xbc = _ssd_conv(xbc_raw, conv_w, conv_b, geo)
    dt, cs, cst = _ssd_dt(dt_raw, dt_bias, a_log, n_heads)
    y = _ssd_scan(xbc, dt, cs, cst, d_skip, geo, n_heads=n_heads, n_groups=n_groups)
    return _ssd_out(y, z, norm_g, out_w, n_groups)


DA_HEAD_DIM = 64
ROPE_BASE = 10000.0
GRID_W = 64


def _rope_tables(geo, tm):
    hd = DA_HEAD_DIM
    n_freq = hd // 4
    t = jnp.arange(geo.n_lat, dtype=F32)
    row, col = jnp.floor(t / GRID_W), jnp.mod(t, GRID_W)
    inv = ROPE_BASE ** (-jnp.arange(n_freq, dtype=F32) / n_freq)
    lane = jnp.arange(LANES)
    within = lane % hd
    freq = inv[within % n_freq]
    pos = jnp.where((within < hd // 2)[None, :], row[:, None], col[:, None])
    ang = pos * freq[None, :]
    sign = jnp.where((within % (2 * n_freq)) < n_freq, -1.0, 1.0)
    cos = jnp.concatenate([jnp.cos(ang), jnp.ones((tm, LANES), F32)], axis=0)
    sin = jnp.concatenate([jnp.sin(ang) * sign[None, :], jnp.zeros((tm, LANES), F32)], axis=0)
    return cos, sin


def _da_qkv_kernel(x_ref, mod_ref, g_ref, w_ref, cos_ref, sin_ref, qg_ref, kg_ref, seg_ref,
                   q_ref, k_ref, v_ref):
    m = mod_ref[0]
    h = (_rms(x_ref[...]) * g_ref[...] * (1.0 + m[1:2, :]) + m[0:1, :]).astype(BF16)
    width = q_ref.shape[1]
    tm = h.shape[0]
    cos = cos_ref[...]
    sin = sin_ref[...]
    lane = lax.broadcasted_iota(jnp.int32, (tm, LANES), 1)
    n_freq = DA_HEAD_DIM // 4
    first_half = (lane % (2 * n_freq)) < n_freq
    scale = DA_HEAD_DIM ** -0.5
    v_ref[...] = jnp.dot(h, w_ref[:, 2 * width:], preferred_element_type=F32).astype(v_ref.dtype)
    for o_ref, off, gain, mult in ((q_ref, 0, qg_ref, scale), (k_ref, width, kg_ref, 1.0)):
        for blk in range(width // LANES):
            cols = slice(off + blk * LANES, off + (blk + 1) * LANES)
            a = jnp.dot(h, w_ref[:, cols], preferred_element_type=F32)
            ms = jnp.dot((a * a).astype(BF16), seg_ref[...], preferred_element_type=F32)
            a = a * lax.rsqrt(ms + EPS) * gain[...]
            partner = jnp.where(first_half, pltpu.roll(a, LANES - n_freq, 1), pltpu.roll(a, n_freq, 1))
            a = a * cos + partner * sin
            o_ref[:, blk * LANES:(blk + 1) * LANES] = (a * mult).astype(o_ref.dtype)


def _da_qkv(x, mods, g1, qkv_w, q_g, k_g, geo, tm=MOD_TILE):
    n, d = x.shape
    width = qkv_w.shape[1] // 3
    nt, _, mod = _tile_maps(geo, tm, False)
    tpb, ct, lt = geo.tiles(tm)
    cos, sin = _rope_tables(geo, tm)
    tab = lambda i: (jnp.where(i % tpb < ct, lt, i % tpb - ct), 0)
    lane = jnp.arange(LANES)
    seg = jnp.where((lane[:, None] // DA_HEAD_DIM) == (lane[None, :] // DA_HEAD_DIM), 1.0 / DA_HEAD_DIM, 0.0)
    tile2 = lambda v: jnp.tile(v.astype(F32), LANES // DA_HEAD_DIM).reshape(1, LANES)
    const = lambda i: (0, 0)
    row = lambda i: (i, 0)
    return pl.pallas_call(
        _da_qkv_kernel,
        grid=(nt,),
        in_specs=[
            pl.BlockSpec((tm, d), row),
            pl.BlockSpec((1, 8, d), lambda i: (mod(i), 0, 0)),
            pl.BlockSpec((1, d), const),
            pl.BlockSpec((d, 3 * width), const),
            pl.BlockSpec((tm, LANES), tab),
            pl.BlockSpec((tm, LANES), tab),
            pl.BlockSpec((1, LANES), const),
            pl.BlockSpec((1, LANES), const),
            pl.BlockSpec((LANES, LANES), const),
        ],
        out_specs=[pl.BlockSpec((tm, width), row)] * 3,
        out_shape=[jax.ShapeDtypeStruct((n, width), BF16)] * 3,
        compiler_params=_cp(("parallel",)),
        name="da_qkv",
    )(x, mods, g1.reshape(1, d), qkv_w.astype(BF16), cos, sin, tile2(q_g), tile2(k_g), seg.astype(BF16))


def _da_attn_kernel(lam_ref, q_ref, k_ref, v_ref, sg_ref, o_ref, *, n_ctx, ctx_tiles, out_scale):
    qi = pl.program_id(2)
    lam = lam_ref[0]
    q = q_ref[...]
    lane = lax.broadcasted_iota(jnp.int32, q.shape, 1)
    zero = jnp.zeros_like(q)
    q0 = jnp.where(lane < DA_HEAD_DIM, q, zero)
    q1 = jnp.where(lane < DA_HEAD_DIM, zero, q)
    nt = (((1,), (1,)), ((), ()))

    def attend(k, v):
        def probs(qc):
            s = lax.dot_general(qc, k, nt, preferred_element_type=F32)
            e = jnp.exp(s - jnp.max(s, axis=-1, keepdims=True))
            return e, 1.0 / jnp.sum(e, axis=-1, keepdims=True)

        e0, r0 = probs(q0)
        e1, r1 = probs(q1)
        a = (e0 * r0 - e1 * (lam * r1)).astype(BF16)
        o = jnp.dot(a, v, preferred_element_type=F32)
        o_ref[...] = (_rms(o) * sg_ref[...] * out_scale).astype(o_ref.dtype)

    @pl.when(qi < ctx_tiles)
    def _():
        attend(k_ref[:n_ctx, :], v_ref[:n_ctx, :])

    @pl.when(qi >= ctx_tiles)
    def _():
        attend(k_ref[...], v_ref[...])


def _da_attn(q, k, v, lam, sub_g, lam_init, geo, tq=MOD_TILE):
    n, width = q.shape
    hw = 2 * DA_HEAD_DIM
    n_heads = width // hw
    rpb = geo.rows_per_batch
    tpb, ct, _ = geo.tiles(tq)
    grid_spec = pltpu.PrefetchScalarGridSpec(
        num_scalar_prefetch=0,
        grid=(geo.n_batch, n_heads, tpb),
        in_specs=[
            pl.BlockSpec(memory_space=pltpu.SMEM),
            pl.BlockSpec((tq, hw), lambda b, h, i: (b * tpb + i, h)),
            pl.BlockSpec((rpb, hw), lambda b, h, i: (b, h)),
            pl.BlockSpec((rpb, hw), lambda b, h, i: (b, h)),
            pl.BlockSpec((1, hw), lambda b, h, i: (0, 0)),
        ],
        out_specs=pl.BlockSpec((tq, hw), lambda b, h, i: (b * tpb + i, h)),
    )
    return pl.pallas_call(
        functools.partial(_da_attn_kernel, n_ctx=geo.n_ctx, ctx_tiles=ct, out_scale=1.0 - lam_init),
        grid_spec=grid_spec,
        out_shape=jax.ShapeDtypeStruct((n, width), BF16),
        compiler_params=_cp(("parallel", "parallel", "arbitrary")),
        name="da_attn",
    )(lam.reshape(1).astype(F32), q, k, v, sub_g.reshape(1, hw).astype(F32))


def _matmul_kernel(x_ref, w_ref, o_ref):
    o_ref[...] = jnp.dot(x_ref[...], w_ref[...], preferred_element_type=F32)


def _matmul(x, w, tm=ROW_TILE):
    n, kd = x.shape
    d = w.shape[1]
    return pl.pallas_call(
        _matmul_kernel,
        grid=(n // tm,),
        in_specs=[pl.BlockSpec((tm, kd), lambda i: (i, 0)), pl.BlockSpec((kd, d), lambda i: (0, 0))],
        out_specs=pl.BlockSpec((tm, d), lambda i: (i, 0)),
        out_shape=jax.ShapeDtypeStruct((n, d), F32),
        compiler_params=_cp(("parallel",)),
        name="da_out",
    )(x, w)


def _da_mixer(x, mods, g1, prm, layer_idx, geo):
    qkv_w, q_g, k_g, lam_vec, sub_g, out_w = prm
    lam_init = 0.8 - 0.6 * math.exp(-0.3 * layer_idx)
    lv = lam_vec.astype(F32)
    lam = jnp.exp(jnp.sum(lv[0] * lv[1])) - jnp.exp(jnp.sum(lv[2] * lv[3])) + lam_init
    q, k, v = _da_qkv(x, mods, g1, qkv_w, q_g, k_g, geo)
    o = _da_attn(q, k, v, lam, sub_g, lam_init, geo)
    return _matmul(o, out_w.astype(BF16))


def kernel(x, c, ctx, c_ctx, mod_w, mod_b, norm1_g, norm2_g, s5_a_re, s5_a_im, s5_log_dt, s5_b_re, s5_b_im, s5_c_re, s5_c_im, s5_d, s5_glu_w, s5_glu_b, ssd_in_w, ssd_conv_w, ssd_conv_b, ssd_dt_bias, ssd_a_log, ssd_d, ssd_norm_g, ssd_out_w, da_qkv_w, da_q_g, da_k_g, da_lam, da_sub_g, da_out_w, moe_router_w, moe_router_b, moe_gu_w, moe_gu_b, moe_dn_w, moe_dn_b):
    bsz, n_lat, d = x.shape
    n_ctx = ctx.shape[1]
    depth = mod_w.shape[0]
    geo = Geo(bsz, n_ctx, n_lat)
    xs = jnp.concatenate([ctx, x], axis=1).reshape(geo.rows, d)
    mods_all = _mods(c, c_ctx, mod_w, mod_b)
    n_mixers = 3
    for i in range(depth):
        last = i == depth - 1
        kind, j = i % n_mixers, i // n_mixers
        mods = mods_all[i]
        if kind == 0:
            prm = (s5_a_re[j], s5_a_im[j], s5_log_dt[j], s5_b_re[j], s5_b_im[j], s5_c_re[j], s5_c_im[j],
                   s5_d[j], s5_glu_w[j], s5_glu_b[j])
            y = _s5_mixer(xs, mods, norm1_g[i], prm, geo)
        elif kind == 1:
            prm = (ssd_in_w[j], ssd_conv_w[j], ssd_conv_b[j], ssd_dt_bias[j], ssd_a_log[j], ssd_d[j],
                   ssd_norm_g[j], ssd_out_w[j])
            y = _ssd_mixer(xs, mods, norm1_g[i], prm, geo)
        else:
            prm = (da_qkv_w[j], da_q_g[j], da_k_g[j], da_lam[j], da_sub_g[j], da_out_w[j])
            y = _da_mixer(xs, mods, norm1_g[i], prm, i, geo)
        xs = _moe_layer(xs, y, mods, norm2_g[i], moe_router_w[i], moe_router_b[i], moe_gu_w[i], moe_gu_b[i],
                        moe_dn_w[i], moe_dn_b[i], geo, last)
    return xs.reshape(bsz, n_lat, d)
```

```python
import functools
import math
from typing import NamedTuple

import jax
import jax.numpy as jnp
from jax import lax
from jax.experimental import pallas as pl
from jax.experimental.pallas import tpu as pltpu

F32 = jnp.float32
BF16 = jnp.bfloat16
EPS = 1e-6
HP = lax.Precision.HIGHEST

LANES = 128
VMEM_LIMIT = 56 * 1024 * 1024

N_EXPERTS = 32
TOP_K = 4
SWIGLU_ALPHA = 1.702
SWIGLU_LIMIT = 7.0
MOE_BLK = 512
MOD_TILE = 256
ROW_TILE = 512


class Geo(NamedTuple):
    n_batch: int
    n_ctx: int
    n_lat: int

    @property
    def rows_per_batch(self):
        return self.n_ctx + self.n_lat

    @property
    def rows(self):
        return self.n_batch * self.rows_per_batch

    def tiles(self, tm):
        return self.rows_per_batch // tm, self.n_ctx // tm, self.n_lat // tm


def _cp(sem, vmem=VMEM_LIMIT):
    return pltpu.CompilerParams(dimension_semantics=sem, vmem_limit_bytes=vmem)


def _tile_maps(geo, tm, lat_only):
    tpb, ct, lt = geo.tiles(tm)
    if lat_only:
        src = lambda i: (i // lt) * tpb + ct + i % lt
        mod = lambda i: i // lt
        return geo.n_batch * lt, src, mod
    mod = lambda i: jnp.where(i % tpb < ct, geo.n_batch, i // tpb)
    return geo.n_batch * tpb, (lambda i: i), mod


def _rms(x):
    return x * lax.rsqrt(jnp.mean(x * x, axis=-1, keepdims=True) + EPS)


def _mods_kernel(s_ref, w_ref, b_ref, o_ref):
    s = s_ref[...]
    s = s * jax.nn.sigmoid(s)
    o_ref[0] = jnp.dot(s, w_ref[0], preferred_element_type=F32, precision=HP) + b_ref[0]


def _mods(c, c_ctx, mod_w, mod_b):
    depth, d, d6 = mod_w.shape
    bsz = c.shape[0]
    rows = -(-(bsz + 1) // 8) * 8
    s = jnp.zeros((rows, d), F32).at[:bsz].set(c).at[bsz].set(c_ctx)
    nj = d6 // d
    out = pl.pallas_call(
        _mods_kernel,
        grid=(depth, nj),
        in_specs=[
            pl.BlockSpec((rows, d), lambda l, j: (0, 0)),
            pl.BlockSpec((1, d, d), lambda l, j: (l, 0, j)),
            pl.BlockSpec((1, 1, d), lambda l, j: (l, 0, j)),
        ],
        out_specs=pl.BlockSpec((1, rows, d), lambda l, j: (l, 0, j)),
        out_shape=jax.ShapeDtypeStruct((depth, rows, d6), F32),
        compiler_params=_cp(("arbitrary", "arbitrary")),
        name="mods",
    )(s, mod_w, mod_b.reshape(depth, 1, d6))
    out = out[:, :bsz + 1].reshape(depth, bsz + 1, nj, d)
    return jnp.pad(out, ((0, 0), (0, 0), (0, 8 - nj), (0, 0)))


def _to_row_tiles(ref, val):
    rows, d = val.shape
    sub = d // LANES
    for s in range(sub):
        ref[pl.ds(s, rows, stride=sub), :] = val[:, s * LANES:(s + 1) * LANES]


def _from_row_tiles(ref, rows, sub, dtype=None):
    parts = [ref[pl.ds(s, rows, stride=sub), :] for s in range(sub)]
    if dtype is not None:
        parts = [p.astype(dtype) for p in parts]
    return jnp.concatenate(parts, axis=1)


def _route_kernel(x_ref, y_ref, mod_ref, g_ref, rw_ref, rb_ref, tri_ref,
                  xn_ref, h_ref, route_ref, cnt_ref, base_ref):
    i = pl.program_id(0)

    @pl.when(i == 0)
    def _():
        base_ref[...] = jnp.zeros_like(base_ref)

    m = mod_ref[0]
    x = x_ref[...] + m[2:3, :] * y_ref[...]
    xn_ref[...] = x
    h = _rms(x) * g_ref[...] * (1.0 + m[4:5, :]) + m[3:4, :]
    _to_row_tiles(h_ref, h)
    logits = jnp.dot(h, rw_ref[...], preferred_element_type=F32, precision=HP) + rb_ref[...]
    tm = logits.shape[0]
    lane = lax.broadcasted_iota(jnp.int32, (tm, LANES), 1)
    neg = jnp.float32(-jnp.inf)
    work = jnp.where(lane < N_EXPERTS, logits, neg)
    vals, ohs, idxs = [], [], []
    for _ in range(TOP_K):
        mk = jnp.max(work, axis=-1, keepdims=True)
        ik = jnp.min(jnp.where(work == mk, lane, LANES), axis=-1, keepdims=True)
        oh = lane == ik
        work = jnp.where(oh, neg, work)
        vals.append(mk)
        ohs.append(oh)
        idxs.append(ik)
    es = [jnp.exp(v - vals[0]) for v in vals]
    den = es[0] + es[1] + es[2] + es[3]
    onehot = jnp.zeros((tm, LANES), F32)
    for oh in ohs:
        onehot = onehot + jnp.where(oh, 1.0, 0.0)
    prefix = jnp.dot(tri_ref[...], onehot.astype(BF16), preferred_element_type=F32)
    pos = prefix + base_ref[0:1, :]
    out = jnp.zeros((tm, LANES), F32)
    for k in range(TOP_K):
        rank = jnp.sum(jnp.where(ohs[k], pos, 0.0), axis=-1, keepdims=True)
        out = out + jnp.where(lane == k, idxs[k].astype(F32), 0.0)
        out = out + jnp.where(lane == TOP_K + k, rank, 0.0)
        out = out + jnp.where(lane == 2 * TOP_K + k, es[k] / den, 0.0)
    route_ref[...] = out
    newbase = base_ref[0:1, :] + jnp.sum(onehot, axis=0, keepdims=True)
    base_ref[...] = jnp.broadcast_to(newbase, base_ref.shape)
    cnt_ref[...] = jnp.broadcast_to(newbase, cnt_ref.shape)


def _route(x, y, mods, g2, rw, rb, geo, lat_only, tm=MOD_TILE):
    d = x.shape[1]
    sub = d // LANES
    nt, src, mod = _tile_maps(geo, tm, lat_only)
    n = nt * tm
    rw_p = jnp.zeros((d, LANES), F32).at[:, :N_EXPERTS].set(rw)
    rb_p = jnp.zeros((1, LANES), F32).at[0, :N_EXPERTS].set(rb)
    tri = (jnp.arange(tm)[:, None] > jnp.arange(tm)[None, :]).astype(BF16)
    const = lambda i: (0, 0)
    return pl.pallas_call(
        _route_kernel,
        grid=(nt,),
        in_specs=[
            pl.BlockSpec((tm, d), lambda i: (src(i), 0)),
            pl.BlockSpec((tm, d), lambda i: (src(i), 0)),
            pl.BlockSpec((1, 8, d), lambda i: (mod(i), 0, 0)),
            pl.BlockSpec((1, d), const),
            pl.BlockSpec((d, LANES), const),
            pl.BlockSpec((1, LANES), const),
            pl.BlockSpec((tm, tm), const),
        ],
        out_specs=[
            pl.BlockSpec((tm, d), lambda i: (i, 0)),
            pl.BlockSpec((tm * sub, LANES), lambda i: (i, 0)),
            pl.BlockSpec((tm, LANES), lambda i: (i, 0)),
            pl.BlockSpec((8, LANES), const),
        ],
        out_shape=[
            jax.ShapeDtypeStruct((n, d), F32),
            jax.ShapeDtypeStruct((n * sub, LANES), F32),
            jax.ShapeDtypeStruct((n, LANES), F32),
            jax.ShapeDtypeStruct((8, LANES), F32),
        ],
        scratch_shapes=[pltpu.VMEM((8, LANES), F32)],
        compiler_params=_cp(("arbitrary",)),
        name="route",
    )(x, y, mods, g2.reshape(1, d), rw_p, rb_p, tri)


def _dispatch_kernel(last_ref, dest_ref, h_ref, xs_ref, zero_ref, sem, zsem, *, sub, blk):
    i = pl.program_id(0)
    tm = h_ref.shape[0] // sub

    @pl.when(i == 0)
    def _():
        zero_ref[...] = jnp.zeros_like(zero_ref)

        def zcopy(e):
            return pltpu.make_async_copy(zero_ref, xs_ref.at[pl.ds(last_ref[e] * (blk * sub), blk * sub), :], zsem)

        def zstart(e, c):
            @pl.when(last_ref[e] >= 0)
            def _():
                zcopy(e).start()
            return c

        def zwait(e, c):
            @pl.when(last_ref[e] >= 0)
            def _():
                zcopy(e).wait()
            return c

        lax.fori_loop(0, N_EXPERTS, zstart, 0)
        lax.fori_loop(0, N_EXPERTS, zwait, 0)

    tok_per_row = LANES // TOP_K

    def start(row, c):
        for ln in range(LANES):
            src0 = pl.multiple_of(row * (tok_per_row * sub) + (ln // TOP_K) * sub, sub)
            dst0 = pl.multiple_of(dest_ref[row, ln] * sub, sub)
            pltpu.make_async_copy(h_ref.at[pl.ds(src0, sub), :], xs_ref.at[pl.ds(dst0, sub), :], sem).start(
                priority=ln % 2)
        return c

    lax.fori_loop(0, tm // tok_per_row, start, 0)
    for k in range(TOP_K):
        pltpu.make_async_copy(h_ref, xs_ref.at[pl.ds(0, tm * sub), :], sem).wait()


def _dispatch(h, dest2d, last_blk, n_slots, sub, tm=ROW_TILE):
    n = h.shape[0] // sub
    nt = n // tm
    db = tm * TOP_K // LANES
    grid_spec = pltpu.PrefetchScalarGridSpec(
        num_scalar_prefetch=1,
        grid=(nt,),
        in_specs=[
            pl.BlockSpec((db, LANES), lambda i, lb: (i, 0), memory_space=pltpu.SMEM),
            pl.BlockSpec((tm * sub, LANES), lambda i, lb: (i, 0)),
        ],
        out_specs=pl.BlockSpec(memory_space=pl.ANY),
        scratch_shapes=[pltpu.VMEM((MOE_BLK * sub, LANES), F32), pltpu.SemaphoreType.DMA(()),
                        pltpu.SemaphoreType.DMA(())],
    )
    return pl.pallas_call(
        functools.partial(_dispatch_kernel, sub=sub, blk=MOE_BLK),
        grid_spec=grid_spec,
        out_shape=jax.ShapeDtypeStruct((n_slots * sub, LANES), F32),
        compiler_params=_cp(("arbitrary",)),
        name="dispatch",
    )(last_blk, dest2d, h)


def _expert_kernel(be_ref, na_ref, xs_ref, guw_ref, gub_ref, dnw_ref, dnb_ref, ys_ref):
    j = pl.program_id(0)

    @pl.when(j < na_ref[0])
    def _():
        f, d = dnw_ref.shape[1], dnw_ref.shape[2]
        sub = d // LANES
        x = _from_row_tiles(xs_ref, xs_ref.shape[0] // sub, sub, BF16)
        gu = jnp.dot(x, guw_ref[0], preferred_element_type=F32) + gub_ref[0]
        gate = jnp.minimum(gu[:, :f], SWIGLU_LIMIT)
        up = jnp.clip(gu[:, f:], -SWIGLU_LIMIT, SWIGLU_LIMIT)
        act = (up + 1.0) * gate * jax.nn.sigmoid(SWIGLU_ALPHA * gate)
        y = jnp.dot(act.astype(BF16), dnw_ref[0], preferred_element_type=F32) + dnb_ref[0]
        _to_row_tiles(ys_ref, y)


def _experts(xs, block_e, n_active, guw, gub, dnw, dnb, blk=MOE_BLK):
    e, d, f2 = guw.shape
    f = f2 // 2
    sub = d // LANES
    nb = xs.shape[0] // (blk * sub)
    act_blk = lambda j, be, na: (jnp.minimum(j, na[0] - 1), 0)
    wsel = lambda j, be, na: (be[j], 0, 0)
    grid_spec = pltpu.PrefetchScalarGridSpec(
        num_scalar_prefetch=2,
        grid=(nb,),
        in_specs=[
            pl.BlockSpec((blk * sub, LANES), act_blk),
            pl.BlockSpec((1, d, f2), wsel),
            pl.BlockSpec((1, 1, f2), wsel),
            pl.BlockSpec((1, f, d), wsel),
            pl.BlockSpec((1, 1, d), wsel),
        ],
        out_specs=pl.BlockSpec((blk * sub, LANES), act_blk),
    )
    return pl.pallas_call(
        _expert_kernel,
        grid_spec=grid_spec,
        out_shape=jax.ShapeDtypeStruct(xs.shape, F32),
        compiler_params=_cp(("arbitrary",)),
        name="experts",
    )(block_e, n_active, xs, guw, gub.reshape(e, 1, f2), dnw, dnb.reshape(e, 1, d))


def _combine_kernel(dest_ref, x_ref, route_ref, mod_ref, ys_ref, o_ref, buf, sem):
    tm, d = x_ref.shape
    sub = d // LANES

    tok_per_row = LANES // TOP_K

    def start(row, c):
        for ln in range(LANES):
            src0 = pl.multiple_of(dest_ref[row, ln] * sub, sub)
            dst0 = pl.multiple_of(row * (tok_per_row * sub) + (ln // TOP_K) * sub, sub)
            pltpu.make_async_copy(ys_ref.at[pl.ds(src0, sub), :], buf.at[ln % TOP_K, pl.ds(dst0, sub), :], sem).start(
                priority=ln % 2)
        return c

    lax.fori_loop(0, tm // tok_per_row, start, 0)
    for k in range(TOP_K):
        pltpu.make_async_copy(ys_ref.at[pl.ds(0, tm * sub), :], buf.at[k], sem).wait()
    route = route_ref[...]
    gates = [route[:, 2 * TOP_K + k:2 * TOP_K + k + 1] for k in range(TOP_K)]
    scale = mod_ref[0][5:6, :]
    for s in range(sub):
        cols = slice(s * LANES, (s + 1) * LANES)
        f = gates[0] * buf[0, pl.ds(s, tm, stride=sub), :]
        for k in range(1, TOP_K):
            f = f + gates[k] * buf[k, pl.ds(s, tm, stride=sub), :]
        o_ref[:, cols] = x_ref[:, cols] + scale[:, cols] * f


def _combine(x, route, mods, ys, dest2d, geo, lat_only, tm=MOD_TILE):
    n, d = x.shape
    sub = d // LANES
    nt, _, mod = _tile_maps(geo, tm, lat_only)
    db = tm * TOP_K // LANES
    return pl.pallas_call(
        _combine_kernel,
        grid=(nt,),
        in_specs=[
            pl.BlockSpec((db, LANES), lambda i: (i, 0), memory_space=pltpu.SMEM),
            pl.BlockSpec((tm, d), lambda i: (i, 0)),
            pl.BlockSpec((tm, LANES), lambda i: (i, 0)),
            pl.BlockSpec((1, 8, d), lambda i: (mod(i), 0, 0)),
            pl.BlockSpec(memory_space=pl.ANY),
        ],
        out_specs=pl.BlockSpec((tm, d), lambda i: (i, 0)),
        out_shape=jax.ShapeDtypeStruct((n, d), F32),
        scratch_shapes=[pltpu.VMEM((TOP_K, tm * sub, LANES), F32), pltpu.SemaphoreType.DMA(())],
        compiler_params=_cp(("arbitrary",)),
        name="combine",
    )(dest2d, x, route, mods, ys)


def _moe_layer(x, y, mods, g2, rw, rb, guw, gub, dnw, dnb, geo, lat_only):
    xn, h, route, cnt = _route(x, y, mods, g2, rw, rb, geo, lat_only)
    n, d = xn.shape
    sub = d // LANES
    counts = cnt[0, :N_EXPERTS].astype(jnp.int32)
    nblk = (counts + MOE_BLK - 1) // MOE_BLK
    blk_end = jnp.cumsum(nblk)
    pad_start = (blk_end - nblk) * MOE_BLK
    n_blocks = -(-(n * TOP_K) // MOE_BLK) + N_EXPERTS
    block_e = jnp.minimum(jnp.sum(blk_end[None, :] <= jnp.arange(n_blocks)[:, None], axis=1), N_EXPERTS - 1)
    n_active = blk_end[-1:].astype(jnp.int32)
    last_blk = jnp.where(nblk > 0, blk_end - 1, -1).astype(jnp.int32)
    eid = route[:, :TOP_K].astype(jnp.int32)
    rank = route[:, TOP_K:2 * TOP_K].astype(jnp.int32)
    dest = (pad_start[eid] + rank).astype(jnp.int32).reshape(n * TOP_K // LANES, LANES)
    xs = _dispatch(h, dest, last_blk, n_blocks * MOE_BLK, sub)
    ys = _experts(xs, block_e.astype(jnp.int32), n_active, guw.astype(BF16), gub, dnw.astype(BF16), dnb)
    return _combine(xn, route, mods, ys, dest, geo, lat_only)


def _prenorm_kernel(x_ref, mod_ref, g_ref, h_ref):
    m = mod_ref[0]
    h = _rms(x_ref[...]) * g_ref[...] * (1.0 + m[1:2, :]) + m[0:1, :]
    h_ref[...] = h.astype(h_ref.dtype)


def _prenorm(x, mods, g1, geo, tm=MOD_TILE):
    n, d = x.shape
    nt, _, mod = _tile_maps(geo, tm, False)
    return pl.pallas_call(
        _prenorm_kernel,
        grid=(nt,),
        in_specs=[
            pl.BlockSpec((tm, d), lambda i: (i, 0)),
            pl.BlockSpec((1, 8, d), lambda i: (mod(i), 0, 0)),
            pl.BlockSpec((1, d), lambda i: (0, 0)),
        ],
        out_specs=pl.BlockSpec((tm, d), lambda i: (i, 0)),
        out_shape=jax.ShapeDtypeStruct((n, d), BF16),
        compiler_params=_cp(("parallel",)),
        name="prenorm",
    )(x, mods, g1.reshape(1, d))


S5_CHUNK = 16


def _s5_matrices(a_re, a_im, log_dt, b_re, b_im, c_re, c_im):
    L = S5_CHUNK
    a_re = a_re.astype(F32)
    a_im = a_im.astype(F32)
    dt = jnp.exp(log_dt.astype(F32))[..., None]
    mag = jnp.exp(dt * a_re)
    ang = dt * a_im
    ab_re, ab_im = mag * jnp.cos(ang), mag * jnp.sin(ang)
    den = a_re * a_re + a_im * a_im
    num_re = ab_re - 1.0
    co_re = (num_re * a_re + ab_im * a_im) / den
    co_im = (ab_im * a_re - num_re * a_im) / den
    bb_re = co_re[..., None] * b_re - co_im[..., None] * b_im
    bb_im = co_re[..., None] * b_im + co_im[..., None] * b_re
    taus = jnp.arange(L + 1, dtype=F32)[:, None, None, None]
    pmag = jnp.exp(taus * dt * a_re)
    pang = taus * dt * a_im
    pw_re, pw_im = pmag * jnp.cos(pang), pmag * jnp.sin(pang)
    cp_re = c_re[None] * pw_re[:, :, :, None, :] - c_im[None] * pw_im[:, :, :, None, :]
    cp_im = c_re[None] * pw_im[:, :, :, None, :] + c_im[None] * pw_re[:, :, :, None, :]
    kk = (jnp.einsum('tdgjp,dgpi->tdgji', cp_re, bb_re, precision=HP)
          - jnp.einsum('tdgjp,dgpi->tdgji', cp_im, bb_im, precision=HP))
    t_idx = jnp.arange(L)
    lag = t_idx[None, :] - t_idx[:, None]

    def toeplitz(kd, lagm):
        kt = kd[jnp.clip(lagm, 0, L)]
        kt = jnp.where((lagm >= 0)[:, :, None, None, None], kt, 0.0)
        return kt.transpose(2, 0, 4, 1, 3)

    g = a_re.shape[1]
    p = a_re.shape[2]
    j = b_re.shape[3]
    m = (toeplitz(kk[:, 0], lag) + toeplitz(kk[:, 1], -lag)).reshape(g, L * j, L * j)

    def inject(d, steps):
        pr, pi = pw_re[steps, d], pw_im[steps, d]
        re = pr[:, :, :, None] * bb_re[d][None] - pi[:, :, :, None] * bb_im[d][None]
        im = pr[:, :, :, None] * bb_im[d][None] + pi[:, :, :, None] * bb_re[d][None]
        f = lambda z: z.transpose(1, 0, 3, 2).reshape(g, L * j, p)
        return f(re), f(im)

    inj = inject(0, L - 1 - t_idx) + inject(1, t_idx)

    def readout(d, steps):
        re = cp_re[steps, d]
        im = cp_im[steps, d]
        f = lambda z: z.transpose(1, 3, 0, 2).reshape(g, p, L * j)
        return f(re), f(-im)

    rd = readout(0, t_idx + 1) + readout(1, L - t_idx)
    adv = jnp.stack([pw_re[L, 0], pw_im[L, 0], pw_re[L, 1], pw_im[L, 1]], axis=1)
    adv = jnp.pad(adv, ((0, 0), (0, 4), (0, 0)))
    bf = lambda z: z.astype(BF16)
    return bf(m), [bf(z) for z in inj], [bf(z) for z in rd], adv


def _s5_kernel(u_ref, m_ref, if_re, if_im, ib_re, ib_im, rf_re, rf_im, rb_re, rb_im, adv_ref, y_ref,
               sf_re, sf_im, sb_re, sb_im, *, n_batch, ctx_chunks, lat_chunks):
    u = u_ref[0]
    dot = functools.partial(jnp.dot, preferred_element_type=F32)
    sf_re[...] = dot(u, if_re[0])
    sf_im[...] = dot(u, if_im[0])
    sb_re[...] = dot(u, ib_re[0])
    sb_im[...] = dot(u, ib_im[0])
    adv = adv_ref[0]
    p = adv.shape[-1]
    nb = n_batch

    def scan(s_re_ref, s_im_ref, a_re, a_im, order):
        a_re = jnp.broadcast_to(a_re, (nb, p))
        a_im = jnp.broadcast_to(a_im, (nb, p))

        def step(row, carry):
            c_re, c_im = carry
            row = pl.multiple_of(row, nb)
            l_re = s_re_ref[pl.ds(row, nb), :]
            l_im = s_im_ref[pl.ds(row, nb), :]
            s_re_ref[pl.ds(row, nb), :] = c_re
            s_im_ref[pl.ds(row, nb), :] = c_im
            return (a_re * c_re - a_im * c_im + l_re, a_re * c_im + a_im * c_re + l_im)

        carry = (jnp.zeros((nb, p), F32), jnp.zeros((nb, p), F32))
        for start, count, rev in order:
            def body(i, carry, start=start, count=count, rev=rev):
                c = (count - 1 - i) if rev else i
                return step((start + c) * nb, carry)
            carry = lax.fori_loop(0, count, body, carry)

    scan(sf_re, sf_im, adv[0:1, :], adv[1:2, :], [(0, ctx_chunks + lat_chunks, False)])
    scan(sb_re, sb_im, adv[2:3, :], adv[3:4, :], [(0, ctx_chunks, True), (ctx_chunks, lat_chunks, True)])
    y = dot(u, m_ref[0])
    y = y + dot(sf_re[...].astype(BF16), rf_re[0]) + dot(sf_im[...].astype(BF16), rf_im[0])
    y = y + dot(sb_re[...].astype(BF16), rb_re[0]) + dot(sb_im[...].astype(BF16), rb_im[0])
    y_ref[0] = y.astype(y_ref.dtype)


def _s5_core(h, mats, geo):
    m, inj, rd, adv = mats
    g, lj, _ = m.shape
    p = adv.shape[-1]
    L = S5_CHUNK
    j = lj // L
    nb = geo.n_batch
    nchunk = geo.rows_per_batch // L
    rows = nchunk * nb
    u = h.reshape(nb, nchunk, L, g, j).transpose(3, 1, 0, 2, 4).reshape(g, rows, lj)
    gsel = lambda i: (i, 0, 0)
    y = pl.pallas_call(
        functools.partial(_s5_kernel, n_batch=nb, ctx_chunks=geo.n_ctx // L, lat_chunks=geo.n_lat // L),
        grid=(g,),
        in_specs=[pl.BlockSpec((1, rows, lj), gsel), pl.BlockSpec((1, lj, lj), gsel)]
        + [pl.BlockSpec((1, lj, p), gsel)] * 4 + [pl.BlockSpec((1, p, lj), gsel)] * 4
        + [pl.BlockSpec((1, 8, p), gsel)],
        out_specs=pl.BlockSpec((1, rows, lj), gsel),
        out_shape=jax.ShapeDtypeStruct((g, rows, lj), BF16),
        scratch_shapes=[pltpu.VMEM((rows, p), F32)] * 4,
        compiler_params=_cp(("parallel",)),
        name="s5_core",
    )(u, m, *inj, *rd, adv)
    return y.reshape(g, nchunk, nb, L, j).transpose(2, 1, 3, 0, 4).reshape(nb * nchunk * L, g * j)


def _gelu_tanh(x):
    return 0.5 * x * (1.0 + jnp.tanh(math.sqrt(2.0 / math.pi) * (x + 0.044715 * (x * x * x))))


def _s5_glu_kernel(h_ref, y_ref, d_ref, w_ref, b_ref, o_ref):
    y = d_ref[...] * h_ref[...].astype(F32) + y_ref[...].astype(F32)
    gl = _gelu_tanh(y).astype(BF16)
    z = jnp.dot(gl, w_ref[...], preferred_element_type=F32) + b_ref[...]
    dd = o_ref.shape[1]
    o_ref[...] = z[:, :dd] * jax.nn.sigmoid(z[:, dd:])


def _s5_glu(h, y, d_skip, glu_w, glu_b, tm=ROW_TILE):
    n, d = h.shape
    return pl.pallas_call(
        _s5_glu_kernel,
        grid=(n // tm,),
        in_specs=[
            pl.BlockSpec((tm, d), lambda i: (i, 0)),
            pl.BlockSpec((tm, d), lambda i: (i, 0)),
            pl.BlockSpec((1, d), lambda i: (0, 0)),
            pl.BlockSpec((d, 2 * d), lambda i: (0, 0)),
            pl.BlockSpec((1, 2 * d), lambda i: (0, 0)),
        ],
        out_specs=pl.BlockSpec((tm, d), lambda i: (i, 0)),
        out_shape=jax.ShapeDtypeStruct((n, d), F32),
        compiler_params=_cp(("parallel",)),
        name="s5_glu",
    )(h, y, d_skip.reshape(1, d).astype(F32), glu_w.astype(BF16), glu_b.reshape(1, 2 * d).astype(F32))


def _s5_mixer(x, mods, g1, prm, geo):
    a_re, a_im, log_dt, b_re, b_im, c_re, c_im, d_skip, glu_w, glu_b = prm
    h = _prenorm(x, mods, g1, geo)
    mats = _s5_matrices(a_re, a_im, log_dt, b_re, b_im, c_re, c_im)
    y = _s5_core(h, mats, geo)
    return _s5_glu(h, y, d_skip, glu_w, glu_b)


SSD_Q = 256
SSD_HEAD_DIM = 64
SSD_STATE = 128
SSD_CONV = 5
CONV_HALO = 16


def _ssd_inproj_kernel(x_ref, mod_ref, g_ref, wz_ref, wx_ref, wd_ref, z_ref, xbc_ref, dt_ref):
    m = mod_ref[0]
    h = (_rms(x_ref[...]) * g_ref[...] * (1.0 + m[1:2, :]) + m[0:1, :]).astype(BF16)
    z_ref[...] = jnp.dot(h, wz_ref[...], preferred_element_type=F32).astype(z_ref.dtype)
    xbc_ref[...] = jnp.dot(h, wx_ref[...], preferred_element_type=F32).astype(xbc_ref.dtype)
    dt_ref[...] = jnp.dot(h, wd_ref[...], preferred_element_type=F32)


def _ssd_inproj(x, mods, g1, in_w, d_inner, n_bc, geo, tm=MOD_TILE):
    n, d = x.shape
    nt, _, mod = _tile_maps(geo, tm, False)
    wz = in_w[:, :d_inner].astype(BF16)
    wx = in_w[:, d_inner:2 * d_inner + n_bc].astype(BF16)
    wd = in_w[:, 2 * d_inner + n_bc:]
    wd = jnp.pad(wd, ((0, 0), (0, LANES - wd.shape[1]))).astype(BF16)
    cx = d_inner + n_bc
    const = lambda i: (0, 0)
    row = lambda i: (i, 0)
    return pl.pallas_call(
        _ssd_inproj_kernel,
        grid=(nt,),
        in_specs=[
            pl.BlockSpec((tm, d), row),
            pl.BlockSpec((1, 8, d), lambda i: (mod(i), 0, 0)),
            pl.BlockSpec((1, d), const),
            pl.BlockSpec((d, d_inner), const),
            pl.BlockSpec((d, cx), const),
            pl.BlockSpec((d, LANES), const),
        ],
        out_specs=[pl.BlockSpec((tm, d_inner), row), pl.BlockSpec((tm, cx), row), pl.BlockSpec((tm, LANES), row)],
        out_shape=[
            jax.ShapeDtypeStruct((n, d_inner), BF16),
            jax.ShapeDtypeStruct((n, cx), BF16),
            jax.ShapeDtypeStruct((n, LANES), F32),
        ],
        compiler_params=_cp(("parallel",)),
        name="ssd_inproj",
    )(x, mods, g1.reshape(1, d), wz, wx, wd)


def _ssd_conv_kernel(prev_ref, cur_ref, next_ref, w_ref, b_ref, o_ref, *, tiles_per_batch, ctx_tiles):
    j = pl.program_id(0) % tiles_per_batch
    first = jnp.logical_or(j == 0, j == ctx_tiles)
    last = jnp.logical_or(j == ctx_tiles - 1, j == tiles_per_batch - 1)
    half = CONV_HALO // 2
    cur = cur_ref[...].astype(F32)
    tt = cur.shape[0]
    prev = jnp.where(first, 0.0, prev_ref[...].astype(F32)[half:, :])
    nxt = jnp.where(last, 0.0, next_ref[...].astype(F32)[:half, :])
    ext = jnp.concatenate([prev, cur, nxt], axis=0)
    w = w_ref[...]
    pad = (SSD_CONV - 1) // 2
    acc = jnp.broadcast_to(b_ref[...], cur.shape)
    for k in range(SSD_CONV):
        off = half + k - pad
        acc = acc + w[k:k + 1, :] * ext[off:off + tt, :]
    o_ref[...] = (acc * jax.nn.sigmoid(acc)).astype(o_ref.dtype)


def _ssd_conv(xbc, conv_w, conv_b, geo, tt=MOD_TILE, tc=2048):
    n, cx = xbc.shape
    tpb, ct, _ = geo.tiles(tt)
    nt = n // tt
    hb = tt // CONV_HALO
    nhb = n // CONV_HALO
    wpad = jnp.pad(conv_w, ((0, 8 - conv_w.shape[0]), (0, 0))).astype(F32)
    return pl.pallas_call(
        functools.partial(_ssd_conv_kernel, tiles_per_batch=tpb, ctx_tiles=ct),
        grid=(nt, cx // tc),
        in_specs=[
            pl.BlockSpec((CONV_HALO, tc), lambda i, c: (jnp.maximum(i * hb - 1, 0), c)),
            pl.BlockSpec((tt, tc), lambda i, c: (i, c)),
            pl.BlockSpec((CONV_HALO, tc), lambda i, c: (jnp.minimum((i + 1) * hb, nhb - 1), c)),
            pl.BlockSpec((8, tc), lambda i, c: (0, c)),
            pl.BlockSpec((1, tc), lambda i, c: (0, c)),
        ],
        out_specs=pl.BlockSpec((tt, tc), lambda i, c: (i, c)),
        out_shape=jax.ShapeDtypeStruct((n, cx), BF16),
        compiler_params=_cp(("parallel", "parallel")),
        name="ssd_conv",
    )(xbc, xbc, xbc, wpad, conv_b.reshape(1, cx).astype(F32))


def _ssd_dt_kernel(raw_ref, bias_ref, a_ref, dt_ref, cs_ref, cst_ref, *, n_heads):
    q = raw_ref.shape[0]
    dt = jax.nn.softplus(raw_ref[...] + bias_ref[...])
    dta = dt * a_ref[...]
    r = lax.broadcasted_iota(jnp.int32, (q, q), 0)
    c = lax.broadcasted_iota(jnp.int32, (q, q), 1)
    lower = jnp.where(c <= r, 1.0, 0.0)
    upper = jnp.where(c >= r, 1.0, 0.0)
    cs_f = jnp.dot(lower, dta, preferred_element_type=F32, precision=HP)
    cs_b = jnp.dot(upper, dta, preferred_element_type=F32, precision=HP)
    lane = lax.broadcasted_iota(jnp.int32, dt.shape, 1)
    cs = jnp.where(lane < n_heads, cs_f, cs_b)
    dt_ref[...] = dt
    cs_ref[...] = cs
    cst_ref[0] = cs.T


def _ssd_dt(dt_raw, dt_bias, a_log, n_heads):
    n = dt_raw.shape[0]
    q = SSD_Q
    pad = lambda v: jnp.pad(v.reshape(1, -1).astype(F32), ((0, 0), (0, LANES - v.size)))
    return pl.pallas_call(
        functools.partial(_ssd_dt_kernel, n_heads=n_heads),
        grid=(n // q,),
        in_specs=[
            pl.BlockSpec((q, LANES), lambda i: (i, 0)),
            pl.BlockSpec((1, LANES), lambda i: (0, 0)),
            pl.BlockSpec((1, LANES), lambda i: (0, 0)),
        ],
        out_specs=[
            pl.BlockSpec((q, LANES), lambda i: (i, 0)),
            pl.BlockSpec((q, LANES), lambda i: (i, 0)),
            pl.BlockSpec((1, LANES, q), lambda i: (i, 0, 0)),
        ],
        out_shape=[
            jax.ShapeDtypeStruct((n, LANES), F32),
            jax.ShapeDtypeStruct((n, LANES), F32),
            jax.ShapeDtypeStruct((n // q, LANES, q), F32),
        ],
        compiler_params=_cp(("parallel",)),
        name="ssd_dt",
    )(dt_raw, pad(dt_bias), pad(-jnp.exp(a_log.astype(F32))))


def _ssd_scan_kernel(x_ref, b_ref, c_ref, dt_ref, cs_ref, cst_ref, dsk_ref, y_ref, st_ref, *,
                     n_heads, heads_per_group, ctx_chunks, lat_chunks):
    g = pl.program_id(1)
    q = SSD_Q
    hd = SSD_HEAD_DIM
    width = heads_per_group * hd
    dot = functools.partial(jnp.dot, preferred_element_type=F32)
    lane = lax.broadcasted_iota(jnp.int32, (q, LANES), 1)
    blk = lax.broadcasted_iota(jnp.int32, (q, width), 1) // hd
    r_i = lax.broadcasted_iota(jnp.int32, (q, q), 0)
    c_i = lax.broadcasted_iota(jnp.int32, (q, q), 1)

    def spread(cols):
        out = jnp.broadcast_to(cols[0], (q, width))
        for hh in range(1, heads_per_group):
            out = jnp.where(blk == hh, cols[hh], out)
        return out

    def chunk(ci, direction, first_pass):
        r0 = ci * q if isinstance(ci, int) else pl.multiple_of(ci * q, q)
        rows = pl.ds(r0, q)
        xb = x_ref[rows, :].astype(F32)
        bm = b_ref[rows, :]
        cm = c_ref[rows, :]
        dt = dt_ref[rows, :]
        cs = cs_ref[rows, :]
        col0 = direction * n_heads + g * heads_per_group
        dt_cols, cs_cols, cs_rows = [], [], []
        for hh in range(heads_per_group):
            sel = lane == col0 + hh
            dt_cols.append(jnp.sum(jnp.where(sel, dt, 0.0), axis=1, keepdims=True))
            cs_cols.append(jnp.sum(jnp.where(sel, cs, 0.0), axis=1, keepdims=True))
            cs_rows.append(cst_ref[ci, pl.ds(col0 + hh, 1), :])
        xdt = xb * spread(dt_cols)
        csf = spread(cs_cols)
        edge = csf[q - 1:q, :] if direction == 0 else csf[0:1, :]
        scores = lax.dot_general(cm, bm, (((1,), (1,)), ((), ())), preferred_element_type=F32)
        keep = (c_i <= r_i) if direction == 0 else (c_i >= r_i)
        xdt_b = xdt.astype(BF16)
        state = st_ref[...]
        y = dot(cm, state.astype(BF16)) * jnp.exp(csf)
        for hh in range(heads_per_group):
            seg = jnp.where(keep, cs_cols[hh] - cs_rows[hh], -jnp.inf)
            pm = (scores * jnp.exp(seg)).astype(BF16)
            y = y + dot(pm, jnp.where(blk == hh, xdt_b, jnp.zeros_like(xdt_b)))
        w = (xdt * jnp.exp(edge - csf)).astype(BF16)
        bt = bm.astype(F32).T.astype(BF16)
        st_ref[...] = jnp.exp(edge) * state + dot(bt, w)
        if first_pass:
            y_ref[rows, :] = y + dsk_ref[0] * xb
        else:
            y_ref[rows, :] = y_ref[rows, :] + y

    for direction in range(2):
        st_ref[...] = jnp.zeros_like(st_ref)
        for ci in range(ctx_chunks):
            cc = ci if direction == 0 else ctx_chunks - 1 - ci
            chunk(cc, direction, direction == 0)

        def body(i, carry, direction=direction):
            cc = ctx_chunks + (i if direction == 0 else lat_chunks - 1 - i)
            chunk(cc, direction, direction == 0)
            return carry

        lax.fori_loop(0, lat_chunks, body, 0)


def _ssd_scan(xbc, dt, cs, cst, d_skip, geo, *, n_heads, n_groups):
    n = xbc.shape[0]
    rpb = geo.rows_per_batch
    hpg = n_heads // n_groups
    width = hpg * SSD_HEAD_DIM
    d_inner = n_heads * SSD_HEAD_DIM
    xblocks = d_inner // width
    bblocks = d_inner // SSD_STATE
    cpb = rpb // SSD_Q
    dsk = jnp.repeat(d_skip.astype(F32), SSD_HEAD_DIM).reshape(n_groups, 1, width)
    kern = functools.partial(_ssd_scan_kernel, n_heads=n_heads, heads_per_group=hpg,
                             ctx_chunks=geo.n_ctx // SSD_Q, lat_chunks=geo.n_lat // SSD_Q)
    return pl.pallas_call(
        kern,
        grid=(geo.n_batch, n_groups),
        in_specs=[
            pl.BlockSpec((rpb, width), lambda b, g: (b, g)),
            pl.BlockSpec((rpb, SSD_STATE), lambda b, g: (b, bblocks + g)),
            pl.BlockSpec((rpb, SSD_STATE), lambda b, g: (b, bblocks + n_groups + g)),
            pl.BlockSpec((rpb, LANES), lambda b, g: (b, 0)),
            pl.BlockSpec((rpb, LANES), lambda b, g: (b, 0)),
            pl.BlockSpec((cpb, LANES, SSD_Q), lambda b, g: (b, 0, 0)),
            pl.BlockSpec((1, 1, width), lambda b, g: (g, 0, 0)),
        ],
        out_specs=pl.BlockSpec((rpb, width), lambda b, g: (b, g)),
        out_shape=jax.ShapeDtypeStruct((n, d_inner), F32),
        scratch_shapes=[pltpu.VMEM((SSD_STATE, width), F32)],
        compiler_params=_cp(("parallel", "arbitrary")),
        name="ssd_scan",
    )(xbc, xbc, xbc, dt, cs, cst, dsk)


def _ssd_out_kernel(y_ref, z_ref, g_ref, w_ref, o_ref, *, n_groups):
    z = z_ref[...].astype(F32)
    gy = y_ref[...] * (z * jax.nn.sigmoid(z))
    gw = gy.shape[1] // n_groups
    acc = jnp.zeros(o_ref.shape, F32)
    for gi in range(n_groups):
        sl = slice(gi * gw, (gi + 1) * gw)
        ng = (_rms(gy[:, sl]) * g_ref[:, sl]).astype(BF16)
        acc = acc + jnp.dot(ng, w_ref[sl, :], preferred_element_type=F32)
    o_ref[...] = acc


def _ssd_out(y, z, norm_g, out_w, n_groups, tm=ROW_TILE):
    n, di = y.shape
    d = out_w.shape[1]
    return pl.pallas_call(
        functools.partial(_ssd_out_kernel, n_groups=n_groups),
        grid=(n // tm,),
        in_specs=[
            pl.BlockSpec((tm, di), lambda i: (i, 0)),
            pl.BlockSpec((tm, di), lambda i: (i, 0)),
            pl.BlockSpec((1, di), lambda i: (0, 0)),
            pl.BlockSpec((di, d), lambda i: (0, 0)),
        ],
        out_specs=pl.BlockSpec((tm, d), lambda i: (i, 0)),
        out_shape=jax.ShapeDtypeStruct((n, d), F32),
        compiler_params=_cp(("parallel",)),
        name="ssd_out",
    )(y, z, norm_g.reshape(1, di).astype(F32), out_w.astype(BF16))


def _ssd_mixer(x, mods, g1, prm, geo):
    in_w, conv_w, conv_b, dt_bias, a_log, d_skip, norm_g, out_w = prm
    n_heads = a_log.shape[1]
    d_inner = out_w.shape[0]
    n_bc = conv_w.shape[1] - d_inner
    n_groups = n_bc // (2 * SSD_STATE)
    z, xbc_raw, dt_raw = _ssd_inproj(x, mods, g1, in_w, d_inner, n_bc, geo)
    xbc = _ssd_conv(xbc_raw, conv_w, conv_b, geo)
    dt, cs, cst = _ssd_dt(dt_raw, dt_bias, a_log, n_heads)
    y = _ssd_scan(xbc, dt, cs, cst, d_skip, geo, n_heads=n_heads, n_groups=n_groups)
    return _ssd_out(y, z, norm_g, out_w, n_groups)


DA_HEAD_DIM = 64
ROPE_BASE = 10000.0
GRID_W = 64


def _rope_tables(geo, tm):
    hd = DA_HEAD_DIM
    n_freq = hd // 4
    t = jnp.arange(geo.n_lat, dtype=F32)
    row, col = jnp.floor(t / GRID_W), jnp.mod(t, GRID_W)
    inv = ROPE_BASE ** (-jnp.arange(n_freq, dtype=F32) / n_freq)
    lane = jnp.arange(LANES)
    within = lane % hd
    freq = inv[within % n_freq]
    pos = jnp.where((within < hd // 2)[None, :], row[:, None], col[:, None])
    ang = pos * freq[None, :]
    sign = jnp.where((within % (2 * n_freq)) < n_freq, -1.0, 1.0)
    cos = jnp.concatenate([jnp.cos(ang), jnp.ones((tm, LANES), F32)], axis=0)
    sin = jnp.concatenate([jnp.sin(ang) * sign[None, :], jnp.zeros((tm, LANES), F32)], axis=0)
    return cos, sin


def _da_qkv_kernel(x_ref, mod_ref, g_ref, w_ref, cos_ref, sin_ref, qg_ref, kg_ref, seg_ref,
                   q_ref, k_ref, v_ref):
    m = mod_ref[0]
    h = (_rms(x_ref[...]) * g_ref[...] * (1.0 + m[1:2, :]) + m[0:1, :]).astype(BF16)
    width = q_ref.shape[1]
    tm = h.shape[0]
    cos = cos_ref[...]
    sin = sin_ref[...]
    lane = lax.broadcasted_iota(jnp.int32, (tm, LANES), 1)
    n_freq = DA_HEAD_DIM // 4
    first_half = (lane % (2 * n_freq)) < n_freq
    scale = DA_HEAD_DIM ** -0.5 * math.log2(math.e)
    v_ref[...] = jnp.dot(h, w_ref[:, 2 * width:], preferred_element_type=F32).astype(v_ref.dtype)
    for o_ref, off, gain, mult in ((q_ref, 0, qg_ref, scale), (k_ref, width, kg_ref, 1.0)):
        for blk in range(width // LANES):
            cols = slice(off + blk * LANES, off + (blk + 1) * LANES)
            a = jnp.dot(h, w_ref[:, cols], preferred_element_type=F32)
            ms = jnp.dot((a * a).astype(BF16), seg_ref[...], preferred_element_type=F32)
            a = a * lax.rsqrt(ms + EPS) * gain[...]
            partner = jnp.where(first_half, pltpu.roll(a, LANES - n_freq, 1), pltpu.roll(a, n_freq, 1))
            a = a * cos + partner * sin
            o_ref[:, blk * LANES:(blk + 1) * LANES] = (a * mult).astype(o_ref.dtype)


def _da_qkv(x, mods, g1, qkv_w, q_g, k_g, geo, tm=MOD_TILE):
    n, d = x.shape
    width = qkv_w.shape[1] // 3
    nt, _, mod = _tile_maps(geo, tm, False)
    tpb, ct, lt = geo.tiles(tm)
    cos, sin = _rope_tables(geo, tm)
    tab = lambda i: (jnp.where(i % tpb < ct, lt, i % tpb - ct), 0)
    lane = jnp.arange(LANES)
    seg = jnp.where((lane[:, None] // DA_HEAD_DIM) == (lane[None, :] // DA_HEAD_DIM), 1.0 / DA_HEAD_DIM, 0.0)
    tile2 = lambda v: jnp.tile(v.astype(F32), LANES // DA_HEAD_DIM).reshape(1, LANES)
    const = lambda i: (0, 0)
    row = lambda i: (i, 0)
    return pl.pallas_call(
        _da_qkv_kernel,
        grid=(nt,),
        in_specs=[
            pl.BlockSpec((tm, d), row),
            pl.BlockSpec((1, 8, d), lambda i: (mod(i), 0, 0)),
            pl.BlockSpec((1, d), const),
            pl.BlockSpec((d, 3 * width), const),
            pl.BlockSpec((tm, LANES), tab),
            pl.BlockSpec((tm, LANES), tab),
            pl.BlockSpec((1, LANES), const),
            pl.BlockSpec((1, LANES), const),
            pl.BlockSpec((LANES, LANES), const),
        ],
        out_specs=[pl.BlockSpec((tm, width), row)] * 3,
        out_shape=[jax.ShapeDtypeStruct((n, width), BF16)] * 3,
        compiler_params=_cp(("parallel",)),
        name="da_qkv",
    )(x, mods, g1.reshape(1, d), qkv_w.astype(BF16), cos, sin, tile2(q_g), tile2(k_g), seg.astype(BF16))


def _da_attn_kernel(lam_ref, q_ref, k_ref, v_ref, sg_ref, o_ref, vx_ref, *, n_ctx, ctx_tiles, out_scale):
    qi = pl.program_id(2)
    hw = v_ref.shape[1]

    @pl.when(qi == 0)
    def _():
        vx_ref[:, :hw] = v_ref[...]
        vx_ref[:, hw:] = jnp.ones((vx_ref.shape[0], hw), vx_ref.dtype)

    lam = lam_ref[0]
    q = q_ref[...]
    lane = lax.broadcasted_iota(jnp.int32, q.shape, 1)
    zero = jnp.zeros_like(q)
    q0 = jnp.where(lane < DA_HEAD_DIM, q, zero)
    q1 = jnp.where(lane < DA_HEAD_DIM, zero, q)
    nt = (((1,), (1,)), ((), ()))

    def attend(k, vx):
        def softmax_times_v(qc):
            s = lax.dot_general(qc, k, nt, preferred_element_type=F32)
            e = jnp.exp2(s - jnp.max(s, axis=-1, keepdims=True)).astype(BF16)
            ov = jnp.dot(e, vx, preferred_element_type=F32)
            return ov[:, :hw] / ov[:, hw:]

        o = softmax_times_v(q0) - lam * softmax_times_v(q1)
        o_ref[...] = (_rms(o) * sg_ref[...] * out_scale).astype(o_ref.dtype)

    @pl.when(qi < ctx_tiles)
    def _():
        attend(k_ref[:n_ctx, :], vx_ref[:n_ctx, :])

    @pl.when(qi >= ctx_tiles)
    def _():
        attend(k_ref[...], vx_ref[...])


def _da_attn(q, k, v, lam, sub_g, lam_init, geo, tq=MOD_TILE):
    n, width = q.shape
    hw = 2 * DA_HEAD_DIM
    n_heads = width // hw
    rpb = geo.rows_per_batch
    tpb, ct, _ = geo.tiles(tq)
    grid_spec = pltpu.PrefetchScalarGridSpec(
        num_scalar_prefetch=0,
        grid=(geo.n_batch, n_heads, tpb),
        in_specs=[
            pl.BlockSpec(memory_space=pltpu.SMEM),
            pl.BlockSpec((tq, hw), lambda b, h, i: (b * tpb + i, h)),
            pl.BlockSpec((rpb, hw), lambda b, h, i: (b, h)),
            pl.BlockSpec((rpb, hw), lambda b, h, i: (b, h)),
            pl.BlockSpec((1, hw), lambda b, h, i: (0, 0)),
        ],
        out_specs=pl.BlockSpec((tq, hw), lambda b, h, i: (b * tpb + i, h)),
        scratch_shapes=[pltpu.VMEM((rpb, 2 * hw), BF16)],
    )
    return pl.pallas_call(
        functools.partial(_da_attn_kernel, n_ctx=geo.n_ctx, ctx_tiles=ct, out_scale=1.0 - lam_init),
        grid_spec=grid_spec,
        out_shape=jax.ShapeDtypeStruct((n, width), BF16),
        compiler_params=_cp(("parallel", "parallel", "arbitrary")),
        name="da_attn",
    )(lam.reshape(1).astype(F32), q, k, v, sub_g.reshape(1, hw).astype(F32))


def _matmul_kernel(x_ref, w_ref, o_ref):
    o_ref[...] = jnp.dot(x_ref[...], w_ref[...], preferred_element_type=F32)


def _matmul(x, w, tm=ROW_TILE):
    n, kd = x.shape
    d = w.shape[1]
    return pl.pallas_call(
        _matmul_kernel,
        grid=(n // tm,),
        in_specs=[pl.BlockSpec((tm, kd), lambda i: (i, 0)), pl.BlockSpec((kd, d), lambda i: (0, 0))],
        out_specs=pl.BlockSpec((tm, d), lambda i: (i, 0)),
        out_shape=jax.ShapeDtypeStruct((n, d), F32),
        compiler_params=_cp(("parallel",)),
        name="da_out",
    )(x, w)


def _da_mixer(x, mods, g1, prm, layer_idx, geo):
    qkv_w, q_g, k_g, lam_vec, sub_g, out_w = prm
    lam_init = 0.8 - 0.6 * math.exp(-0.3 * layer_idx)
    lv = lam_vec.astype(F32)
    lam = jnp.exp(jnp.sum(lv[0] * lv[1])) - jnp.exp(jnp.sum(lv[2] * lv[3])) + lam_init
    q, k, v = _da_qkv(x, mods, g1, qkv_w, q_g, k_g, geo)
    o = _da_attn(q, k, v, lam, sub_g, lam_init, geo)
    return _matmul(o, out_w.astype(BF16))


def kernel(x, c, ctx, c_ctx, mod_w, mod_b, norm1_g, norm2_g, s5_a_re, s5_a_im, s5_log_dt, s5_b_re, s5_b_im, s5_c_re, s5_c_im, s5_d, s5_glu_w, s5_glu_b, ssd_in_w, ssd_conv_w, ssd_conv_b, ssd_dt_bias, ssd_a_log, ssd_d, ssd_norm_g, ssd_out_w, da_qkv_w, da_q_g, da_k_g, da_lam, da_sub_g, da_out_w, moe_router_w, moe_router_b, moe_gu_w, moe_gu_b, moe_dn_w, moe_dn_b):
    bsz, n_lat, d = x.shape
    n_ctx = ctx.shape[1]
    depth = mod_w.shape[0]
    geo = Geo(bsz, n_ctx, n_lat)
    xs = jnp.concatenate([ctx, x], axis=1).reshape(geo.rows, d)
    mods_all = _mods(c, c_ctx, mod_w, mod_b)
    n_mixers = 3
    for i in range(depth):
        last = i == depth - 1
        kind, j = i % n_mixers, i // n_mixers
        mods = mods_all[i]
        if kind == 0:
            prm = (s5_a_re[j], s5_a_im[j], s5_log_dt[j], s5_b_re[j], s5_b_im[j], s5_c_re[j], s5_c_im[j],
                   s5_d[j], s5_glu_w[j], s5_glu_b[j])
            y = _s5_mixer(xs, mods, norm1_g[i], prm, geo)
        elif kind == 1:
            prm = (ssd_in_w[j], ssd_conv_w[j], ssd_conv_b[j], ssd_dt_bias[j], ssd_a_log[j], ssd_d[j],
                   ssd_norm_g[j], ssd_out_w[j])
            y = _ssd_mixer(xs, mods, norm1_g[i], prm, geo)
        else:
            prm = (da_qkv_w[j], da_q_g[j], da_k_g[j], da_lam[j], da_sub_g[j], da_out_w[j])
            y = _da_mixer(xs, mods, norm1_g[i], prm, i, geo)
        xs = _moe_layer(xs, y, mods, norm2_g[i], moe_router_w[i], moe_router_b[i], moe_gu_w[i], moe_gu_b[i],
                        moe_dn_w[i], moe_dn_b[i], geo, last)
    return xs.reshape(bsz, n_lat, d)
```

```python
import functools
import math
from typing import NamedTuple

import jax
import jax.numpy as jnp
from jax import lax
from jax.experimental import pallas as pl
from jax.experimental.pallas import tpu as pltpu

F32 = jnp.float32
BF16 = jnp.bfloat16
EPS = 1e-6
HP = lax.Precision.HIGHEST

LANES = 128
VMEM_LIMIT = 56 * 1024 * 1024

N_EXPERTS = 32
TOP_K = 4
SWIGLU_ALPHA = 1.702
SWIGLU_LIMIT = 7.0
MOE_BLK = 512
MOD_TILE = 256
ROW_TILE = 512


class Geo(NamedTuple):
    n_batch: int
    n_ctx: int
    n_lat: int

    @property
    def rows_per_batch(self):
        return self.n_ctx + self.n_lat

    @property
    def rows(self):
        return self.n_batch * self.rows_per_batch

    def tiles(self, tm):
        return self.rows_per_batch // tm, self.n_ctx // tm, self.n_lat // tm


def _cp(sem, vmem=VMEM_LIMIT):
    return pltpu.CompilerParams(dimension_semantics=sem, vmem_limit_bytes=vmem)


def _tile_maps(geo, tm, lat_only):
    tpb, ct, lt = geo.tiles(tm)
    if lat_only:
        src = lambda i: (i // lt) * tpb + ct + i % lt
        mod = lambda i: i // lt
        return geo.n_batch * lt, src, mod
    mod = lambda i: jnp.where(i % tpb < ct, geo.n_batch, i // tpb)
    return geo.n_batch * tpb, (lambda i: i), mod


def _rms(x):
    return x * lax.rsqrt(jnp.mean(x * x, axis=-1, keepdims=True) + EPS)


def _mods_kernel(s_ref, w_ref, b_ref, o_ref):
    s = s_ref[...]
    s = s * jax.nn.sigmoid(s)
    o_ref[0] = jnp.dot(s, w_ref[0], preferred_element_type=F32, precision=HP) + b_ref[0]


def _mods(c, c_ctx, mod_w, mod_b):
    depth, d, d6 = mod_w.shape
    bsz = c.shape[0]
    rows = -(-(bsz + 1) // 8) * 8
    s = jnp.zeros((rows, d), F32).at[:bsz].set(c).at[bsz].set(c_ctx)
    nj = d6 // d
    out = pl.pallas_call(
        _mods_kernel,
        grid=(depth, nj),
        in_specs=[
            pl.BlockSpec((rows, d), lambda l, j: (0, 0)),
            pl.BlockSpec((1, d, d), lambda l, j: (l, 0, j)),
            pl.BlockSpec((1, 1, d), lambda l, j: (l, 0, j)),
        ],
        out_specs=pl.BlockSpec((1, rows, d), lambda l, j: (l, 0, j)),
        out_shape=jax.ShapeDtypeStruct((depth, rows, d6), F32),
        compiler_params=_cp(("arbitrary", "arbitrary")),
        name="mods",
    )(s, mod_w, mod_b.reshape(depth, 1, d6))
    out = out[:, :bsz + 1].reshape(depth, bsz + 1, nj, d)
    return jnp.pad(out, ((0, 0), (0, 0), (0, 8 - nj), (0, 0)))


def _to_row_tiles(ref, val):
    rows, d = val.shape
    sub = d // LANES
    for s in range(sub):
        ref[pl.ds(s, rows, stride=sub), :] = val[:, s * LANES:(s + 1) * LANES]


def _from_row_tiles(ref, rows, sub, dtype=None):
    parts = [ref[pl.ds(s, rows, stride=sub), :] for s in range(sub)]
    if dtype is not None:
        parts = [p.astype(dtype) for p in parts]
    return jnp.concatenate(parts, axis=1)


def _route_kernel(x_ref, y_ref, mod_ref, g_ref, rw_ref, rb_ref, tri_ref,
                  xn_ref, h_ref, route_ref, cnt_ref, base_ref):
    i = pl.program_id(0)

    @pl.when(i == 0)
    def _():
        base_ref[...] = jnp.zeros_like(base_ref)

    m = mod_ref[0]
    x = x_ref[...] + m[2:3, :] * y_ref[...]
    xn_ref[...] = x
    h = _rms(x) * g_ref[...] * (1.0 + m[4:5, :]) + m[3:4, :]
    _to_row_tiles(h_ref, h)
    logits = jnp.dot(h, rw_ref[...], preferred_element_type=F32, precision=HP) + rb_ref[...]
    tm = logits.shape[0]
    lane = lax.broadcasted_iota(jnp.int32, (tm, LANES), 1)
    neg = jnp.float32(-jnp.inf)
    work = jnp.where(lane < N_EXPERTS, logits, neg)
    vals, ohs, idxs = [], [], []
    for _ in range(TOP_K):
        mk = jnp.max(work, axis=-1, keepdims=True)
        ik = jnp.min(jnp.where(work == mk, lane, LANES), axis=-1, keepdims=True)
        oh = lane == ik
        work = jnp.where(oh, neg, work)
        vals.append(mk)
        ohs.append(oh)
        idxs.append(ik)
    es = [jnp.exp(v - vals[0]) for v in vals]
    den = es[0] + es[1] + es[2] + es[3]
    onehot = jnp.zeros((tm, LANES), F32)
    for oh in ohs:
        onehot = onehot + jnp.where(oh, 1.0, 0.0)
    prefix = jnp.dot(tri_ref[...], onehot.astype(BF16), preferred_element_type=F32)
    pos = prefix + base_ref[0:1, :]
    out = jnp.zeros((tm, LANES), F32)
    for k in range(TOP_K):
        rank = jnp.sum(jnp.where(ohs[k], pos, 0.0), axis=-1, keepdims=True)
        out = out + jnp.where(lane == k, idxs[k].astype(F32), 0.0)
        out = out + jnp.where(lane == TOP_K + k, rank, 0.0)
        out = out + jnp.where(lane == 2 * TOP_K + k, es[k] / den, 0.0)
    route_ref[...] = out
    newbase = base_ref[0:1, :] + jnp.sum(onehot, axis=0, keepdims=True)
    base_ref[...] = jnp.broadcast_to(newbase, base_ref.shape)
    cnt_ref[...] = jnp.broadcast_to(newbase, cnt_ref.shape)


def _route(x, y, mods, g2, rw, rb, geo, lat_only, tm=MOD_TILE):
    d = x.shape[1]
    sub = d // LANES
    nt, src, mod = _tile_maps(geo, tm, lat_only)
    n = nt * tm
    rw_p = jnp.zeros((d, LANES), F32).at[:, :N_EXPERTS].set(rw)
    rb_p = jnp.zeros((1, LANES), F32).at[0, :N_EXPERTS].set(rb)
    tri = (jnp.arange(tm)[:, None] > jnp.arange(tm)[None, :]).astype(BF16)
    const = lambda i: (0, 0)
    return pl.pallas_call(
        _route_kernel,
        grid=(nt,),
        in_specs=[
            pl.BlockSpec((tm, d), lambda i: (src(i), 0)),
            pl.BlockSpec((tm, d), lambda i: (src(i), 0)),
            pl.BlockSpec((1, 8, d), lambda i: (mod(i), 0, 0)),
            pl.BlockSpec((1, d), const),
            pl.BlockSpec((d, LANES), const),
            pl.BlockSpec((1, LANES), const),
            pl.BlockSpec((tm, tm), const),
        ],
        out_specs=[
            pl.BlockSpec((tm, d), lambda i: (i, 0)),
            pl.BlockSpec((tm * sub, LANES), lambda i: (i, 0)),
            pl.BlockSpec((tm, LANES), lambda i: (i, 0)),
            pl.BlockSpec((8, LANES), const),
        ],
        out_shape=[
            jax.ShapeDtypeStruct((n, d), F32),
            jax.ShapeDtypeStruct((n * sub, LANES), F32),
            jax.ShapeDtypeStruct((n, LANES), F32),
            jax.ShapeDtypeStruct((8, LANES), F32),
        ],
        scratch_shapes=[pltpu.VMEM((8, LANES), F32)],
        compiler_params=_cp(("arbitrary",)),
        name="route",
    )(x, y, mods, g2.reshape(1, d), rw_p, rb_p, tri)


def _dispatch_kernel(last_ref, dest_ref, h_ref, xs_ref, zero_ref, sem, zsem, *, sub, blk):
    i = pl.program_id(0)
    tm = h_ref.shape[0] // sub

    @pl.when(i == 0)
    def _():
        zero_ref[...] = jnp.zeros_like(zero_ref)

        def zcopy(e):
            return pltpu.make_async_copy(zero_ref, xs_ref.at[pl.ds(last_ref[e] * (blk * sub), blk * sub), :], zsem)

        def zstart(e, c):
            @pl.when(last_ref[e] >= 0)
            def _():
                zcopy(e).start()
            return c

        def zwait(e, c):
            @pl.when(last_ref[e] >= 0)
            def _():
                zcopy(e).wait()
            return c

        lax.fori_loop(0, N_EXPERTS, zstart, 0)
        lax.fori_loop(0, N_EXPERTS, zwait, 0)

    tok_per_row = LANES // TOP_K

    def start(row, c):
        for ln in range(LANES):
            src0 = pl.multiple_of(row * (tok_per_row * sub) + (ln // TOP_K) * sub, sub)
            dst0 = pl.multiple_of(dest_ref[row, ln] * sub, sub)
            pltpu.make_async_copy(h_ref.at[pl.ds(src0, sub), :], xs_ref.at[pl.ds(dst0, sub), :], sem).start(
                priority=ln % 2)
        return c

    lax.fori_loop(0, tm // tok_per_row, start, 0)
    for k in range(TOP_K):
        pltpu.make_async_copy(h_ref, xs_ref.at[pl.ds(0, tm * sub), :], sem).wait()


def _dispatch(h, dest2d, last_blk, n_slots, sub, tm=ROW_TILE):
    n = h.shape[0] // sub
    nt = n // tm
    db = tm * TOP_K // LANES
    grid_spec = pltpu.PrefetchScalarGridSpec(
        num_scalar_prefetch=1,
        grid=(nt,),
        in_specs=[
            pl.BlockSpec((db, LANES), lambda i, lb: (i, 0), memory_space=pltpu.SMEM),
            pl.BlockSpec((tm * sub, LANES), lambda i, lb: (i, 0)),
        ],
        out_specs=pl.BlockSpec(memory_space=pl.ANY),
        scratch_shapes=[pltpu.VMEM((MOE_BLK * sub, LANES), F32), pltpu.SemaphoreType.DMA(()),
                        pltpu.SemaphoreType.DMA(())],
    )
    return pl.pallas_call(
        functools.partial(_dispatch_kernel, sub=sub, blk=MOE_BLK),
        grid_spec=grid_spec,
        out_shape=jax.ShapeDtypeStruct((n_slots * sub, LANES), F32),
        compiler_params=_cp(("arbitrary",)),
        name="dispatch",
    )(last_blk, dest2d, h)


def _expert_kernel(be_ref, na_ref, xs_ref, guw_ref, gub_ref, dnw_ref, dnb_ref, ys_ref):
    j = pl.program_id(0)

    @pl.when(j < na_ref[0])
    def _():
        f, d = dnw_ref.shape[1], dnw_ref.shape[2]
        sub = d // LANES
        x = _from_row_tiles(xs_ref, xs_ref.shape[0] // sub, sub, BF16)
        gu = jnp.dot(x, guw_ref[0], preferred_element_type=F32) + gub_ref[0]
        gate = jnp.minimum(gu[:, :f], SWIGLU_LIMIT)
        up = jnp.clip(gu[:, f:], -SWIGLU_LIMIT, SWIGLU_LIMIT)
        act = (up + 1.0) * gate * jax.nn.sigmoid(SWIGLU_ALPHA * gate)
        y = jnp.dot(act.astype(BF16), dnw_ref[0], preferred_element_type=F32) + dnb_ref[0]
        _to_row_tiles(ys_ref, y)


def _experts(xs, block_e, n_active, guw, gub, dnw, dnb, blk=MOE_BLK):
    e, d, f2 = guw.shape
    f = f2 // 2
    sub = d // LANES
    nb = xs.shape[0] // (blk * sub)
    act_blk = lambda j, be, na: (jnp.minimum(j, na[0] - 1), 0)
    wsel = lambda j, be, na: (be[j], 0, 0)
    grid_spec = pltpu.PrefetchScalarGridSpec(
        num_scalar_prefetch=2,
        grid=(nb,),
        in_specs=[
            pl.BlockSpec((blk * sub, LANES), act_blk),
            pl.BlockSpec((1, d, f2), wsel),
            pl.BlockSpec((1, 1, f2), wsel),
            pl.BlockSpec((1, f, d), wsel),
            pl.BlockSpec((1, 1, d), wsel),
        ],
        out_specs=pl.BlockSpec((blk * sub, LANES), act_blk),
    )
    return pl.pallas_call(
        _expert_kernel,
        grid_spec=grid_spec,
        out_shape=jax.ShapeDtypeStruct(xs.shape, F32),
        compiler_params=_cp(("arbitrary",)),
        name="experts",
    )(block_e, n_active, xs, guw, gub.reshape(e, 1, f2), dnw, dnb.reshape(e, 1, d))


def _combine_kernel(dest_ref, x_ref, route_ref, mod_ref, ys_ref, o_ref, buf, sem):
    tm, d = x_ref.shape
    sub = d // LANES

    tok_per_row = LANES // TOP_K

    def start(row, c):
        for ln in range(LANES):
            src0 = pl.multiple_of(dest_ref[row, ln] * sub, sub)
            dst0 = pl.multiple_of(row * (tok_per_row * sub) + (ln // TOP_K) * sub, sub)
            pltpu.make_async_copy(ys_ref.at[pl.ds(src0, sub), :], buf.at[ln % TOP_K, pl.ds(dst0, sub), :], sem).start(
                priority=ln % 2)
        return c

    lax.fori_loop(0, tm // tok_per_row, start, 0)
    for k in range(TOP_K):
        pltpu.make_async_copy(ys_ref.at[pl.ds(0, tm * sub), :], buf.at[k], sem).wait()
    route = route_ref[...]
    gates = [route[:, 2 * TOP_K + k:2 * TOP_K + k + 1] for k in range(TOP_K)]
    scale = mod_ref[0][5:6, :]
    for s in range(sub):
        cols = slice(s * LANES, (s + 1) * LANES)
        f = gates[0] * buf[0, pl.ds(s, tm, stride=sub), :]
        for k in range(1, TOP_K):
            f = f + gates[k] * buf[k, pl.ds(s, tm, stride=sub), :]
        o_ref[:, cols] = x_ref[:, cols] + scale[:, cols] * f


def _combine(x, route, mods, ys, dest2d, geo, lat_only, tm=MOD_TILE):
    n, d = x.shape
    sub = d // LANES
    nt, _, mod = _tile_maps(geo, tm, lat_only)
    db = tm * TOP_K // LANES
    return pl.pallas_call(
        _combine_kernel,
        grid=(nt,),
        in_specs=[
            pl.BlockSpec((db, LANES), lambda i: (i, 0), memory_space=pltpu.SMEM),
            pl.BlockSpec((tm, d), lambda i: (i, 0)),
            pl.BlockSpec((tm, LANES), lambda i: (i, 0)),
            pl.BlockSpec((1, 8, d), lambda i: (mod(i), 0, 0)),
            pl.BlockSpec(memory_space=pl.ANY),
        ],
        out_specs=pl.BlockSpec((tm, d), lambda i: (i, 0)),
        out_shape=jax.ShapeDtypeStruct((n, d), F32),
        scratch_shapes=[pltpu.VMEM((TOP_K, tm * sub, LANES), F32), pltpu.SemaphoreType.DMA(())],
        compiler_params=_cp(("arbitrary",)),
        name="combine",
    )(dest2d, x, route, mods, ys)


def _moe_layer(x, y, mods, g2, rw, rb, guw, gub, dnw, dnb, geo, lat_only):
    xn, h, route, cnt = _route(x, y, mods, g2, rw, rb, geo, lat_only)
    n, d = xn.shape
    sub = d // LANES
    counts = cnt[0, :N_EXPERTS].astype(jnp.int32)
    nblk = (counts + MOE_BLK - 1) // MOE_BLK
    blk_end = jnp.cumsum(nblk)
    pad_start = (blk_end - nblk) * MOE_BLK
    n_blocks = -(-(n * TOP_K) // MOE_BLK) + N_EXPERTS
    block_e = jnp.minimum(jnp.sum(blk_end[None, :] <= jnp.arange(n_blocks)[:, None], axis=1), N_EXPERTS - 1)
    n_active = blk_end[-1:].astype(jnp.int32)
    last_blk = jnp.where(nblk > 0, blk_end - 1, -1).astype(jnp.int32)
    eid = route[:, :TOP_K].astype(jnp.int32)
    rank = route[:, TOP_K:2 * TOP_K].astype(jnp.int32)
    dest = (pad_start[eid] + rank).astype(jnp.int32).reshape(n * TOP_K // LANES, LANES)
    xs = _dispatch(h, dest, last_blk, n_blocks * MOE_BLK, sub)
    ys = _experts(xs, block_e.astype(jnp.int32), n_active, guw.astype(BF16), gub, dnw.astype(BF16), dnb)
    return _combine(xn, route, mods, ys, dest, geo, lat_only)


def _prenorm_kernel(x_ref, mod_ref, g_ref, h_ref):
    m = mod_ref[0]
    h = _rms(x_ref[...]) * g_ref[...] * (1.0 + m[1:2, :]) + m[0:1, :]
    h_ref[...] = h.astype(h_ref.dtype)


def _prenorm(x, mods, g1, geo, dtype, tm=MOD_TILE):
    n, d = x.shape
    nt, _, mod = _tile_maps(geo, tm, False)
    return pl.pallas_call(
        _prenorm_kernel,
        grid=(nt,),
        in_specs=[
            pl.BlockSpec((tm, d), lambda i: (i, 0)),
            pl.BlockSpec((1, 8, d), lambda i: (mod(i), 0, 0)),
            pl.BlockSpec((1, d), lambda i: (0, 0)),
        ],
        out_specs=pl.BlockSpec((tm, d), lambda i: (i, 0)),
        out_shape=jax.ShapeDtypeStruct((n, d), dtype),
        compiler_params=_cp(("parallel",)),
        name="prenorm",
    )(x, mods, g1.reshape(1, d))


S5_CHUNK = 16
S5_GB = 8


def _s5_matrices(a_re, a_im, log_dt, b_re, b_im, c_re, c_im):
    L = S5_CHUNK
    a_re = a_re.astype(F32)
    a_im = a_im.astype(F32)
    dt = jnp.exp(log_dt.astype(F32))[..., None]
    mag = jnp.exp(dt * a_re)
    ang = dt * a_im
    ab_re, ab_im = mag * jnp.cos(ang), mag * jnp.sin(ang)
    den = a_re * a_re + a_im * a_im
    num_re = ab_re - 1.0
    co_re = (num_re * a_re + ab_im * a_im) / den
    co_im = (ab_im * a_re - num_re * a_im) / den
    bb_re = co_re[..., None] * b_re - co_im[..., None] * b_im
    bb_im = co_re[..., None] * b_im + co_im[..., None] * b_re
    taus = jnp.arange(L + 1, dtype=F32)[:, None, None, None]
    pmag = jnp.exp(taus * dt * a_re)
    pang = taus * dt * a_im
    pw_re, pw_im = pmag * jnp.cos(pang), pmag * jnp.sin(pang)
    cp_re = c_re[None] * pw_re[:, :, :, None, :] - c_im[None] * pw_im[:, :, :, None, :]
    cp_im = c_re[None] * pw_im[:, :, :, None, :] + c_im[None] * pw_re[:, :, :, None, :]
    kk = (jnp.einsum('tdgjp,dgpi->tdgji', cp_re, bb_re, precision=HP)
          - jnp.einsum('tdgjp,dgpi->tdgji', cp_im, bb_im, precision=HP))
    t_idx = jnp.arange(L)
    lag = t_idx[None, :] - t_idx[:, None]

    def toeplitz(kd, lagm):
        kt = kd[jnp.clip(lagm, 0, L)]
        kt = jnp.where((lagm >= 0)[:, :, None, None, None], kt, 0.0)
        return kt.transpose(2, 0, 4, 1, 3)

    g = a_re.shape[1]
    p = a_re.shape[2]
    j = b_re.shape[3]
    m = (toeplitz(kk[:, 0], lag) + toeplitz(kk[:, 1], -lag)).reshape(g, L * j, L * j)

    def inject(d, steps):
        pr, pi = pw_re[steps, d], pw_im[steps, d]
        re = pr[:, :, :, None] * bb_re[d][None] - pi[:, :, :, None] * bb_im[d][None]
        im = pr[:, :, :, None] * bb_im[d][None] + pi[:, :, :, None] * bb_re[d][None]
        f = lambda z: z.transpose(1, 0, 3, 2).reshape(g, L * j, p)
        return f(re), f(im)

    inj = inject(0, L - 1 - t_idx) + inject(1, t_idx)

    def readout(d, steps):
        re = cp_re[steps, d]
        im = cp_im[steps, d]
        f = lambda z: z.transpose(1, 3, 0, 2).reshape(g, p, L * j)
        return f(re), f(-im)

    rd = readout(0, t_idx + 1) + readout(1, L - t_idx)

    gb = S5_GB
    nblk = g // gb
    eye = jnp.eye(gb, dtype=F32)
    m6 = m.reshape(nblk, gb, L, j, L, j)
    m_blk = jnp.einsum('bgkitj,gh->bkgithj', m6, eye).reshape(nblk, L * gb * j, L * gb * j)

    def inj_blk(re, im):
        f = lambda z: jnp.einsum('bgkip,gh->bkgihp', z.reshape(nblk, gb, L, j, p), eye).reshape(nblk, L * gb * j, gb * p)
        return jnp.concatenate([f(re), f(im)], axis=2)

    def rd_blk(re, im):
        f = lambda z: jnp.einsum('bgptj,gh->bgpthj', z.reshape(nblk, gb, p, L, j), eye).reshape(nblk, gb * p, L * gb * j)
        return jnp.concatenate([f(re), f(im)], axis=1)

    def adv_blk(d):
        return jnp.concatenate([pw_re[L, d].reshape(nblk, 1, gb * p), pw_im[L, d].reshape(nblk, 1, gb * p)], axis=2)

    adv = jnp.pad(jnp.concatenate([adv_blk(0), adv_blk(1)], axis=1), ((0, 0), (0, 6), (0, 0)))
    bf = lambda z: z.astype(BF16)
    return (bf(m_blk), bf(inj_blk(inj[0], inj[1])), bf(inj_blk(inj[2], inj[3])),
            bf(rd_blk(rd[0], rd[1])), bf(rd_blk(rd[2], rd[3])), adv)


def _s5_chunk_vectors(x_ref):
    nb, rows, _ = x_ref.shape
    c = rows // S5_CHUNK
    parts = [x_ref[:, pl.ds(l, c, stride=S5_CHUNK), :].reshape(nb * c, LANES).astype(BF16) for l in range(S5_CHUNK)]
    return jnp.concatenate(parts, axis=1)


def _s5_put_states(sloc_ref, val):
    for k in range(sloc_ref.shape[0]):
        sloc_ref[k] = val[:, k * LANES:(k + 1) * LANES]


def _s5_get_states(sloc_ref):
    return jnp.concatenate([sloc_ref[k] for k in range(sloc_ref.shape[0])], axis=1)


def _s5_chunk_scan(sloc_ref, carry_ref, adv, nb, c, reverse):
    nk = sloc_ref.shape[0]
    half = nk // 2
    a = [jnp.broadcast_to(adv[:, k * LANES:(k + 1) * LANES], (nb, LANES)) for k in range(nk)]

    def step(i, st):
        ci = (c - 1 - i) if reverse else i
        rows = pl.ds(ci, nb, stride=c)
        new_re, new_im = [], []
        for k in range(half):
            l_re, l_im = sloc_ref[k, rows, :], sloc_ref[half + k, rows, :]
            s_re, s_im = st[k], st[half + k]
            sloc_ref[k, rows, :] = s_re
            sloc_ref[half + k, rows, :] = s_im
            new_re.append(a[k] * s_re - a[half + k] * s_im + l_re)
            new_im.append(a[k] * s_im + a[half + k] * s_re + l_im)
        return tuple(new_re + new_im)

    st = lax.fori_loop(0, c, step, tuple(carry_ref[k] for k in range(nk)))
    for k in range(nk):
        carry_ref[k] = st[k]


def _s5_bwd_kernel(x_ref, inj_ref, adv_ref, sn_ref, sloc_ref, carry_ref):
    @pl.when(pl.program_id(1) == 0)
    def _():
        carry_ref[...] = jnp.zeros_like(carry_ref)

    nb = x_ref.shape[0]
    c = x_ref.shape[1] // S5_CHUNK
    z = _s5_chunk_vectors(x_ref)
    _s5_put_states(sloc_ref, jnp.dot(z, inj_ref[0], preferred_element_type=F32))
    _s5_chunk_scan(sloc_ref, carry_ref, adv_ref[0][1:2, :], nb, c, True)
    sn_ref[0, 0] = _s5_get_states(sloc_ref).astype(sn_ref.dtype)


def _s5_fwd_kernel(x_ref, sn_ref, m_ref, inj_ref, rdf_ref, rdb_ref, adv_ref, y_ref, sloc_ref, carry_ref):
    @pl.when(pl.program_id(1) == 0)
    def _():
        carry_ref[...] = jnp.zeros_like(carry_ref)

    nb = x_ref.shape[0]
    c = x_ref.shape[1] // S5_CHUNK
    dot = functools.partial(jnp.dot, preferred_element_type=F32)
    z = _s5_chunk_vectors(x_ref)
    _s5_put_states(sloc_ref, dot(z, inj_ref[0]))
    _s5_chunk_scan(sloc_ref, carry_ref, adv_ref[0][0:1, :], nb, c, False)
    y = dot(z, m_ref[0]) + dot(_s5_get_states(sloc_ref).astype(BF16), rdf_ref[0]) + dot(sn_ref[0, 0], rdb_ref[0])
    for t in range(S5_CHUNK):
        y_ref[:, pl.ds(t, c, stride=S5_CHUNK), :] = y[:, t * LANES:(t + 1) * LANES].reshape(nb, c, LANES)


def _s5_core(h, mats, geo, tile=MOD_TILE):
    m, inj_f, inj_b, rd_f, rd_b, adv = mats
    nblk, kdim, sdim = inj_f.shape
    nb, rpb = geo.n_batch, geo.rows_per_batch
    d = h.shape[1]
    tpb, ct, _ = geo.tiles(tile)
    c = tile // S5_CHUNK
    h3 = h.reshape(nb, rpb, d)
    bwd_tile = lambda s: jnp.where(s < ct, ct - 1 - s, tpb - 1 - (s - ct))
    once = pl.Buffered(1)
    wspec = lambda shape: pl.BlockSpec((1,) + shape, lambda g, s: (g, 0, 0), pipeline_mode=once)
    scratch = [pltpu.VMEM((sdim // LANES, nb * c, LANES), F32), pltpu.VMEM((sdim // LANES, nb, LANES), F32)]
    sn = pl.pallas_call(
        _s5_bwd_kernel,
        grid=(nblk, tpb),
        in_specs=[
            pl.BlockSpec((nb, tile, LANES), lambda g, s: (0, bwd_tile(s), g)),
            wspec((kdim, sdim)),
            wspec((8, sdim)),
        ],
        out_specs=pl.BlockSpec((1, 1, nb * c, sdim), lambda g, s: (g, bwd_tile(s), 0, 0)),
        out_shape=jax.ShapeDtypeStruct((nblk, tpb, nb * c, sdim), BF16),
        scratch_shapes=scratch,
        compiler_params=_cp(("parallel", "arbitrary")),
        name="s5_bwd",
    )(h3, inj_b, adv)
    y = pl.pallas_call(
        _s5_fwd_kernel,
        grid=(nblk, tpb),
        in_specs=[
            pl.BlockSpec((nb, tile, LANES), lambda g, s: (0, s, g)),
            pl.BlockSpec((1, 1, nb * c, sdim), lambda g, s: (g, s, 0, 0)),
            wspec((kdim, kdim)),
            wspec((kdim, sdim)),
            wspec((sdim, kdim)),
            wspec((sdim, kdim)),
            wspec((8, sdim)),
        ],
        out_specs=pl.BlockSpec((nb, tile, LANES), lambda g, s: (0, s, g)),
        out_shape=jax.ShapeDtypeStruct((nb, rpb, d), F32),
        scratch_shapes=scratch,
        compiler_params=_cp(("parallel", "arbitrary")),
        name="s5_fwd",
    )(h3, sn, m, inj_f, rd_f, rd_b, adv)
    return y.reshape(nb * rpb, d)


def _gelu_tanh(x):
    return 0.5 * x * (1.0 + jnp.tanh(math.sqrt(2.0 / math.pi) * (x + 0.044715 * (x * x * x))))


def _s5_glu_kernel(h_ref, y_ref, d_ref, w_ref, b_ref, o_ref):
    y = d_ref[...] * h_ref[...].astype(F32) + y_ref[...].astype(F32)
    gl = _gelu_tanh(y).astype(BF16)
    z = jnp.dot(gl, w_ref[...], preferred_element_type=F32) + b_ref[...]
    dd = o_ref.shape[1]
    o_ref[...] = z[:, :dd] * jax.nn.sigmoid(z[:, dd:])


def _s5_glu(h, y, d_skip, glu_w, glu_b, tm=ROW_TILE):
    n, d = h.shape
    return pl.pallas_call(
        _s5_glu_kernel,
        grid=(n // tm,),
        in_specs=[
            pl.BlockSpec((tm, d), lambda i: (i, 0)),
            pl.BlockSpec((tm, d), lambda i: (i, 0)),
            pl.BlockSpec((1, d), lambda i: (0, 0)),
            pl.BlockSpec((d, 2 * d), lambda i: (0, 0)),
            pl.BlockSpec((1, 2 * d), lambda i: (0, 0)),
        ],
        out_specs=pl.BlockSpec((tm, d), lambda i: (i, 0)),
        out_shape=jax.ShapeDtypeStruct((n, d), F32),
        compiler_params=_cp(("parallel",)),
        name="s5_glu",
    )(h, y, d_skip.reshape(1, d).astype(F32), glu_w.astype(BF16), glu_b.reshape(1, 2 * d).astype(F32))


def _s5_mixer(x, mods, g1, prm, geo):
    a_re, a_im, log_dt, b_re, b_im, c_re, c_im, d_skip, glu_w, glu_b = prm
    h = _prenorm(x, mods, g1, geo, F32)
    mats = _s5_matrices(a_re, a_im, log_dt, b_re, b_im, c_re, c_im)
    y = _s5_core(h, mats, geo)
    return _s5_glu(h, y, d_skip, glu_w, glu_b)


SSD_Q = 256
SSD_HEAD_DIM = 64
SSD_STATE = 128
SSD_CONV = 5
CONV_HALO = 16


def _ssd_inproj_kernel(x_ref, mod_ref, g_ref, wz_ref, wx_ref, wd_ref, z_ref, xbc_ref, dt_ref):
    m = mod_ref[0]
    h = (_rms(x_ref[...]) * g_ref[...] * (1.0 + m[1:2, :]) + m[0:1, :]).astype(BF16)
    z_ref[...] = jnp.dot(h, wz_ref[...], preferred_element_type=F32).astype(z_ref.dtype)
    xbc_ref[...] = jnp.dot(h, wx_ref[...], preferred_element_type=F32).astype(xbc_ref.dtype)
    dt_ref[...] = jnp.dot(h, wd_ref[...], preferred_element_type=F32)


def _ssd_inproj(x, mods, g1, in_w, d_inner, n_bc, geo, tm=MOD_TILE):
    n, d = x.shape
    nt, _, mod = _tile_maps(geo, tm, False)
    wz = in_w[:, :d_inner].astype(BF16)
    wx = in_w[:, d_inner:2 * d_inner + n_bc].astype(BF16)
    wd = in_w[:, 2 * d_inner + n_bc:]
    wd = jnp.pad(wd, ((0, 0), (0, LANES - wd.shape[1]))).astype(BF16)
    cx = d_inner + n_bc
    const = lambda i: (0, 0)
    row = lambda i: (i, 0)
    return pl.pallas_call(
        _ssd_inproj_kernel,
        grid=(nt,),
        in_specs=[
            pl.BlockSpec((tm, d), row),
            pl.BlockSpec((1, 8, d), lambda i: (mod(i), 0, 0)),
            pl.BlockSpec((1, d), const),
            pl.BlockSpec((d, d_inner), const),
            pl.BlockSpec((d, cx), const),
            pl.BlockSpec((d, LANES), const),
        ],
        out_specs=[pl.BlockSpec((tm, d_inner), row), pl.BlockSpec((tm, cx), row), pl.BlockSpec((tm, LANES), row)],
        out_shape=[
            jax.ShapeDtypeStruct((n, d_inner), BF16),
            jax.ShapeDtypeStruct((n, cx), BF16),
            jax.ShapeDtypeStruct((n, LANES), F32),
        ],
        compiler_params=_cp(("parallel",)),
        name="ssd_inproj",
    )(x, mods, g1.reshape(1, d), wz, wx, wd)


def _ssd_conv_kernel(prev_ref, cur_ref, next_ref, w_ref, b_ref, o_ref, *, tiles_per_batch, ctx_tiles):
    j = pl.program_id(0) % tiles_per_batch
    first = jnp.logical_or(j == 0, j == ctx_tiles)
    last = jnp.logical_or(j == ctx_tiles - 1, j == tiles_per_batch - 1)
    half = CONV_HALO // 2
    cur = cur_ref[...].astype(F32)
    tt = cur.shape[0]
    prev = jnp.where(first, 0.0, prev_ref[...].astype(F32)[half:, :])
    nxt = jnp.where(last, 0.0, next_ref[...].astype(F32)[:half, :])
    ext = jnp.concatenate([prev, cur, nxt], axis=0)
    w = w_ref[...]
    pad = (SSD_CONV - 1) // 2
    acc = jnp.broadcast_to(b_ref[...], cur.shape)
    for k in range(SSD_CONV):
        off = half + k - pad
        acc = acc + w[k:k + 1, :] * ext[off:off + tt, :]
    o_ref[...] = (acc * jax.nn.sigmoid(acc)).astype(o_ref.dtype)


def _ssd_conv(xbc, conv_w, conv_b, geo, tt=MOD_TILE, tc=2048):
    n, cx = xbc.shape
    tpb, ct, _ = geo.tiles(tt)
    nt = n // tt
    hb = tt // CONV_HALO
    nhb = n // CONV_HALO
    wpad = jnp.pad(conv_w, ((0, 8 - conv_w.shape[0]), (0, 0))).astype(F32)
    return pl.pallas_call(
        functools.partial(_ssd_conv_kernel, tiles_per_batch=tpb, ctx_tiles=ct),
        grid=(nt, cx // tc),
        in_specs=[
            pl.BlockSpec((CONV_HALO, tc), lambda i, c: (jnp.maximum(i * hb - 1, 0), c)),
            pl.BlockSpec((tt, tc), lambda i, c: (i, c)),
            pl.BlockSpec((CONV_HALO, tc), lambda i, c: (jnp.minimum((i + 1) * hb, nhb - 1), c)),
            pl.BlockSpec((8, tc), lambda i, c: (0, c)),
            pl.BlockSpec((1, tc), lambda i, c: (0, c)),
        ],
        out_specs=pl.BlockSpec((tt, tc), lambda i, c: (i, c)),
        out_shape=jax.ShapeDtypeStruct((n, cx), BF16),
        compiler_params=_cp(("parallel", "parallel")),
        name="ssd_conv",
    )(xbc, xbc, xbc, wpad, conv_b.reshape(1, cx).astype(F32))


def _ssd_dt_kernel(raw_ref, bias_ref, a_ref, dt_ref, cs_ref, cst_ref, *, n_heads):
    q = raw_ref.shape[0]
    dt = jax.nn.softplus(raw_ref[...] + bias_ref[...])
    dta = dt * a_ref[...]
    r = lax.broadcasted_iota(jnp.int32, (q, q), 0)
    c = lax.broadcasted_iota(jnp.int32, (q, q), 1)
    lower = jnp.where(c <= r, 1.0, 0.0)
    upper = jnp.where(c >= r, 1.0, 0.0)
    cs_f = jnp.dot(lower, dta, preferred_element_type=F32, precision=HP)
    cs_b = jnp.dot(upper, dta, preferred_element_type=F32, precision=HP)
    lane = lax.broadcasted_iota(jnp.int32, dt.shape, 1)
    cs = jnp.where(lane < n_heads, cs_f, cs_b)
    dt_ref[...] = dt
    cs_ref[...] = cs
    cst_ref[0] = cs.T


def _ssd_dt(dt_raw, dt_bias, a_log, n_heads):
    n = dt_raw.shape[0]
    q = SSD_Q
    pad = lambda v: jnp.pad(v.reshape(1, -1).astype(F32), ((0, 0), (0, LANES - v.size)))
    return pl.pallas_call(
        functools.partial(_ssd_dt_kernel, n_heads=n_heads),
        grid=(n // q,),
        in_specs=[
            pl.BlockSpec((q, LANES), lambda i: (i, 0)),
            pl.BlockSpec((1, LANES), lambda i: (0, 0)),
            pl.BlockSpec((1, LANES), lambda i: (0, 0)),
        ],
        out_specs=[
            pl.BlockSpec((q, LANES), lambda i: (i, 0)),
            pl.BlockSpec((q, LANES), lambda i: (i, 0)),
            pl.BlockSpec((1, LANES, q), lambda i: (i, 0, 0)),
        ],
        out_shape=[
            jax.ShapeDtypeStruct((n, LANES), F32),
            jax.ShapeDtypeStruct((n, LANES), F32),
            jax.ShapeDtypeStruct((n // q, LANES, q), F32),
        ],
        compiler_params=_cp(("parallel",)),
        name="ssd_dt",
    )(dt_raw, pad(dt_bias), pad(-jnp.exp(a_log.astype(F32))))


def _ssd_scan_kernel(x_ref, b_ref, c_ref, dt_ref, cs_ref, cst_ref, dsk_ref, y_ref, st_ref, *,
                     n_heads, heads_per_group, ctx_chunks, lat_chunks):
    g = pl.program_id(1)
    q = SSD_Q
    hd = SSD_HEAD_DIM
    width = heads_per_group * hd
    dot = functools.partial(jnp.dot, preferred_element_type=F32)
    lane = lax.broadcasted_iota(jnp.int32, (q, LANES), 1)
    blk = lax.broadcasted_iota(jnp.int32, (q, width), 1) // hd
    r_i = lax.broadcasted_iota(jnp.int32, (q, q), 0)
    c_i = lax.broadcasted_iota(jnp.int32, (q, q), 1)

    def spread(cols):
        out = jnp.broadcast_to(cols[0], (q, width))
        for hh in range(1, heads_per_group):
            out = jnp.where(blk == hh, cols[hh], out)
        return out

    def chunk(ci, direction, first_pass):
        r0 = ci * q if isinstance(ci, int) else pl.multiple_of(ci * q, q)
        rows = pl.ds(r0, q)
        xb = x_ref[rows, :].astype(F32)
        bm = b_ref[rows, :]
        cm = c_ref[rows, :]
        dt = dt_ref[rows, :]
        cs = cs_ref[rows, :]
        col0 = direction * n_heads + g * heads_per_group
        dt_cols, cs_cols, cs_rows = [], [], []
        for hh in range(heads_per_group):
            sel = lane == col0 + hh
            dt_cols.append(jnp.sum(jnp.where(sel, dt, 0.0), axis=1, keepdims=True))
            cs_cols.append(jnp.sum(jnp.where(sel, cs, 0.0), axis=1, keepdims=True))
            cs_rows.append(cst_ref[ci, pl.ds(col0 + hh, 1), :])
        xdt = xb * spread(dt_cols)
        csf = spread(cs_cols)
        edge = csf[q - 1:q, :] if direction == 0 else csf[0:1, :]
        scores = lax.dot_general(cm, bm, (((1,), (1,)), ((), ())), preferred_element_type=F32)
        keep = (c_i <= r_i) if direction == 0 else (c_i >= r_i)
        xdt_b = xdt.astype(BF16)
        state = st_ref[...]
        y = dot(cm, state.astype(BF16)) * jnp.exp(csf)
        for hh in range(heads_per_group):
            seg = jnp.where(keep, cs_cols[hh] - cs_rows[hh], -jnp.inf)
            pm = (scores * jnp.exp(seg)).astype(BF16)
            y = y + dot(pm, jnp.where(blk == hh, xdt_b, jnp.zeros_like(xdt_b)))
        w = (xdt * jnp.exp(edge - csf)).astype(BF16)
        bt = bm.astype(F32).T.astype(BF16)
        st_ref[...] = jnp.exp(edge) * state + dot(bt, w)
        if first_pass:
            y_ref[rows, :] = y + dsk_ref[0] * xb
        else:
            y_ref[rows, :] = y_ref[rows, :] + y

    for direction in range(2):
        st_ref[...] = jnp.zeros_like(st_ref)
        for ci in range(ctx_chunks):
            cc = ci if direction == 0 else ctx_chunks - 1 - ci
            chunk(cc, direction, direction == 0)

        def body(i, carry, direction=direction):
            cc = ctx_chunks + (i if direction == 0 else lat_chunks - 1 - i)
            chunk(cc, direction, direction == 0)
            return carry

        lax.fori_loop(0, lat_chunks, body, 0)


def _ssd_scan(xbc, dt, cs, cst, d_skip, geo, *, n_heads, n_groups):
    n = xbc.shape[0]
    rpb = geo.rows_per_batch
    hpg = n_heads // n_groups
    width = hpg * SSD_HEAD_DIM
    d_inner = n_heads * SSD_HEAD_DIM
    xblocks = d_inner // width
    bblocks = d_inner // SSD_STATE
    cpb = rpb // SSD_Q
    dsk = jnp.repeat(d_skip.astype(F32), SSD_HEAD_DIM).reshape(n_groups, 1, width)
    kern = functools.partial(_ssd_scan_kernel, n_heads=n_heads, heads_per_group=hpg,
                             ctx_chunks=geo.n_ctx // SSD_Q, lat_chunks=geo.n_lat // SSD_Q)
    return pl.pallas_call(
        kern,
        grid=(geo.n_batch, n_groups),
        in_specs=[
            pl.BlockSpec((rpb, width), lambda b, g: (b, g)),
            pl.BlockSpec((rpb, SSD_STATE), lambda b, g: (b, bblocks + g)),
            pl.BlockSpec((rpb, SSD_STATE), lambda b, g: (b, bblocks + n_groups + g)),
            pl.BlockSpec((rpb, LANES), lambda b, g: (b, 0)),
            pl.BlockSpec((rpb, LANES), lambda b, g: (b, 0)),
            pl.BlockSpec((cpb, LANES, SSD_Q), lambda b, g: (b, 0, 0)),
            pl.BlockSpec((1, 1, width), lambda b, g: (g, 0, 0)),
        ],
        out_specs=pl.BlockSpec((rpb, width), lambda b, g: (b, g)),
        out_shape=jax.ShapeDtypeStruct((n, d_inner), F32),
        scratch_shapes=[pltpu.VMEM((SSD_STATE, width), F32)],
        compiler_params=_cp(("parallel", "arbitrary")),
        name="ssd_scan",
    )(xbc, xbc, xbc, dt, cs, cst, dsk)


def _ssd_out_kernel(y_ref, z_ref, g_ref, w_ref, o_ref, *, n_groups):
    z = z_ref[...].astype(F32)
    gy = y_ref[...] * (z * jax.nn.sigmoid(z))
    gw = gy.shape[1] // n_groups
    acc = jnp.zeros(o_ref.shape, F32)
    for gi in range(n_groups):
        sl = slice(gi * gw, (gi + 1) * gw)
        ng = (_rms(gy[:, sl]) * g_ref[:, sl]).astype(BF16)
        acc = acc + jnp.dot(ng, w_ref[sl, :], preferred_element_type=F32)
    o_ref[...] = acc


def _ssd_out(y, z, norm_g, out_w, n_groups, tm=ROW_TILE):
    n, di = y.shape
    d = out_w.shape[1]
    return pl.pallas_call(
        functools.partial(_ssd_out_kernel, n_groups=n_groups),
        grid=(n // tm,),
        in_specs=[
            pl.BlockSpec((tm, di), lambda i: (i, 0)),
            pl.BlockSpec((tm, di), lambda i: (i, 0)),
            pl.BlockSpec((1, di), lambda i: (0, 0)),
            pl.BlockSpec((di, d), lambda i: (0, 0)),
        ],
        out_specs=pl.BlockSpec((tm, d), lambda i: (i, 0)),
        out_shape=jax.ShapeDtypeStruct((n, d), F32),
        compiler_params=_cp(("parallel",)),
        name="ssd_out",
    )(y, z, norm_g.reshape(1, di).astype(F32), out_w.astype(BF16))


def _ssd_mixer(x, mods, g1, prm, geo):
    in_w, conv_w, conv_b, dt_bias, a_log, d_skip, norm_g, out_w = prm
    n_heads = a_log.shape[1]
    d_inner = out_w.shape[0]
    n_bc = conv_w.shape[1] - d_inner
    n_groups = n_bc // (2 * SSD_STATE)
    z, xbc_raw, dt_raw = _ssd_inproj(x, mods, g1, in_w, d_inner, n_bc, geo)
    xbc = _ssd_conv(xbc_raw, conv_w, conv_b, geo)
    dt, cs, cst = _ssd_dt(dt_raw, dt_bias, a_log, n_heads)
    y = _ssd_scan(xbc, dt, cs, cst, d_skip, geo, n_heads=n_heads, n_groups=n_groups)
    return _ssd_out(y, z, norm_g, out_w, n_groups)


DA_HEAD_DIM = 64
ROPE_BASE = 10000.0
GRID_W = 64


def _rope_tables(geo, tm):
    hd = DA_HEAD_DIM
    n_freq = hd // 4
    t = jnp.arange(geo.n_lat, dtype=F32)
    row, col = jnp.floor(t / GRID_W), jnp.mod(t, GRID_W)
    inv = ROPE_BASE ** (-jnp.arange(n_freq, dtype=F32) / n_freq)
    lane = jnp.arange(LANES)
    within = lane % hd
    freq = inv[within % n_freq]
    pos = jnp.where((within < hd // 2)[None, :], row[:, None], col[:, None])
    ang = pos * freq[None, :]
    sign = jnp.where((within % (2 * n_freq)) < n_freq, -1.0, 1.0)
    cos = jnp.concatenate([jnp.cos(ang), jnp.ones((tm, LANES), F32)], axis=0)
    sin = jnp.concatenate([jnp.sin(ang) * sign[None, :], jnp.zeros((tm, LANES), F32)], axis=0)
    return cos, sin


def _da_qkv_kernel(x_ref, mod_ref, g_ref, w_ref, cos_ref, sin_ref, qg_ref, kg_ref, seg_ref,
                   q_ref, k_ref, v_ref):
    m = mod_ref[0]
    h = (_rms(x_ref[...]) * g_ref[...] * (1.0 + m[1:2, :]) + m[0:1, :]).astype(BF16)
    width = q_ref.shape[1]
    tm = h.shape[0]
    cos = cos_ref[...]
    sin = sin_ref[...]
    lane = lax.broadcasted_iota(jnp.int32, (tm, LANES), 1)
    n_freq = DA_HEAD_DIM // 4
    first_half = (lane % (2 * n_freq)) < n_freq
    scale = DA_HEAD_DIM ** -0.5 * math.log2(math.e)
    v_ref[...] = jnp.dot(h, w_ref[:, 2 * width:], preferred_element_type=F32).astype(v_ref.dtype)
    for o_ref, off, gain, mult in ((q_ref, 0, qg_ref, scale), (k_ref, width, kg_ref, 1.0)):
        for blk in range(width // LANES):
            cols = slice(off + blk * LANES, off + (blk + 1) * LANES)
            a = jnp.dot(h, w_ref[:, cols], preferred_element_type=F32)
            ms = jnp.dot((a * a).astype(BF16), seg_ref[...], preferred_element_type=F32)
            a = a * lax.rsqrt(ms + EPS) * gain[...]
            partner = jnp.where(first_half, pltpu.roll(a, LANES - n_freq, 1), pltpu.roll(a, n_freq, 1))
            a = a * cos + partner * sin
            o_ref[:, blk * LANES:(blk + 1) * LANES] = (a * mult).astype(o_ref.dtype)


def _da_qkv(x, mods, g1, qkv_w, q_g, k_g, geo, tm=MOD_TILE):
    n, d = x.shape
    width = qkv_w.shape[1] // 3
    nt, _, mod = _tile_maps(geo, tm, False)
    tpb, ct, lt = geo.tiles(tm)
    cos, sin = _rope_tables(geo, tm)
    tab = lambda i: (jnp.where(i % tpb < ct, lt, i % tpb - ct), 0)
    lane = jnp.arange(LANES)
    seg = jnp.where((lane[:, None] // DA_HEAD_DIM) == (lane[None, :] // DA_HEAD_DIM), 1.0 / DA_HEAD_DIM, 0.0)
    tile2 = lambda v: jnp.tile(v.astype(F32), LANES // DA_HEAD_DIM).reshape(1, LANES)
    const = lambda i: (0, 0)
    row = lambda i: (i, 0)
    return pl.pallas_call(
        _da_qkv_kernel,
        grid=(nt,),
        in_specs=[
            pl.BlockSpec((tm, d), row),
            pl.BlockSpec((1, 8, d), lambda i: (mod(i), 0, 0)),
            pl.BlockSpec((1, d), const),
            pl.BlockSpec((d, 3 * width), const),
            pl.BlockSpec((tm, LANES), tab),
            pl.BlockSpec((tm, LANES), tab),
            pl.BlockSpec((1, LANES), const),
            pl.BlockSpec((1, LANES), const),
            pl.BlockSpec((LANES, LANES), const),
        ],
        out_specs=[pl.BlockSpec((tm, width), row)] * 3,
        out_shape=[jax.ShapeDtypeStruct((n, width), BF16)] * 3,
        compiler_params=_cp(("parallel",)),
        name="da_qkv",
    )(x, mods, g1.reshape(1, d), qkv_w.astype(BF16), cos, sin, tile2(q_g), tile2(k_g), seg.astype(BF16))


DA_KEY_CHUNK = 256


def _da_attn_kernel(lam_ref, q_ref, k_ref, v_ref, sg_ref, o_ref, vx_ref, acc_ref, *, ctx_tiles, ctx_chunks,
                    all_chunks, out_scale):
    qi = pl.program_id(2)
    hw = v_ref.shape[1]
    kc = DA_KEY_CHUNK

    @pl.when(qi == 0)
    def _():
        vx_ref[:, :hw] = v_ref[...]
        vx_ref[:, hw:] = jnp.ones((vx_ref.shape[0], hw), vx_ref.dtype)

    q = q_ref[...]
    tq = q.shape[0]
    lane = lax.broadcasted_iota(jnp.int32, q.shape, 1)
    zero = jnp.zeros_like(q)
    qs = (jnp.where(lane < DA_HEAD_DIM, q, zero), jnp.where(lane < DA_HEAD_DIM, zero, q))
    nt = (((1,), (1,)), ((), ()))
    acc_ref[...] = jnp.zeros_like(acc_ref)

    def step(j, ms):
        r0 = j * kc if isinstance(j, int) else pl.multiple_of(j * kc, kc)
        kb = k_ref[pl.ds(r0, kc), :]
        vb = vx_ref[pl.ds(r0, kc), :]
        new = []
        for c in range(2):
            s = lax.dot_general(qs[c], kb, nt, preferred_element_type=F32)
            mn = jnp.maximum(ms[c], jnp.max(s, axis=-1, keepdims=True))
            e = jnp.exp2(s - mn).astype(BF16)
            acc_ref[c] = jnp.exp2(ms[c] - mn) * acc_ref[c] + jnp.dot(e, vb, preferred_element_type=F32)
            new.append(mn)
        return tuple(new)

    neg = jnp.full((tq, 1), -jnp.inf, F32)
    ms = (neg, neg)
    for j in range(ctx_chunks):
        ms = step(j, ms)

    @pl.when(qi >= ctx_tiles)
    def _():
        mm = ms
        for j in range(ctx_chunks, all_chunks):
            mm = step(j, mm)

    o = acc_ref[0][:, :hw] / acc_ref[0][:, hw:] - lam_ref[0] * (acc_ref[1][:, :hw] / acc_ref[1][:, hw:])
    o_ref[...] = (_rms(o) * sg_ref[...] * out_scale).astype(o_ref.dtype)


def _da_attn(q, k, v, lam, sub_g, lam_init, geo, tq=MOD_TILE):
    n, width = q.shape
    hw = 2 * DA_HEAD_DIM
    n_heads = width // hw
    rpb = geo.rows_per_batch
    tpb, ct, _ = geo.tiles(tq)
    grid_spec = pltpu.PrefetchScalarGridSpec(
        num_scalar_prefetch=0,
        grid=(geo.n_batch, n_heads, tpb),
        in_specs=[
            pl.BlockSpec(memory_space=pltpu.SMEM),
            pl.BlockSpec((tq, hw), lambda b, h, i: (b * tpb + i, h)),
            pl.BlockSpec((rpb, hw), lambda b, h, i: (b, h)),
            pl.BlockSpec((rpb, hw), lambda b, h, i: (b, h)),
            pl.BlockSpec((1, hw), lambda b, h, i: (0, 0)),
        ],
        out_specs=pl.BlockSpec((tq, hw), lambda b, h, i: (b * tpb + i, h)),
        scratch_shapes=[pltpu.VMEM((rpb, 2 * hw), BF16), pltpu.VMEM((2, tq, 2 * hw), F32)],
    )
    return pl.pallas_call(
        functools.partial(_da_attn_kernel, ctx_tiles=ct, ctx_chunks=geo.n_ctx // DA_KEY_CHUNK,
                          all_chunks=rpb // DA_KEY_CHUNK, out_scale=1.0 - lam_init),
        grid_spec=grid_spec,
        out_shape=jax.ShapeDtypeStruct((n, width), BF16),
        compiler_params=_cp(("parallel", "parallel", "arbitrary")),
        name="da_attn",
    )(lam.reshape(1).astype(F32), q, k, v, sub_g.reshape(1, hw).astype(F32))


def _matmul_kernel(x_ref, w_ref, o_ref):
    o_ref[...] = jnp.dot(x_ref[...], w_ref[...], preferred_element_type=F32)


def _matmul(x, w, tm=ROW_TILE):
    n, kd = x.shape
    d = w.shape[1]
    return pl.pallas_call(
        _matmul_kernel,
        grid=(n // tm,),
        in_specs=[pl.BlockSpec((tm, kd), lambda i: (i, 0)), pl.BlockSpec((kd, d), lambda i: (0, 0))],
        out_specs=pl.BlockSpec((tm, d), lambda i: (i, 0)),
        out_shape=jax.ShapeDtypeStruct((n, d), F32),
        compiler_params=_cp(("parallel",)),
        name="da_out",
    )(x, w)


def _da_mixer(x, mods, g1, prm, layer_idx, geo):
    qkv_w, q_g, k_g, lam_vec, sub_g, out_w = prm
    lam_init = 0.8 - 0.6 * math.exp(-0.3 * layer_idx)
    lv = lam_vec.astype(F32)
    lam = jnp.exp(jnp.sum(lv[0] * lv[1])) - jnp.exp(jnp.sum(lv[2] * lv[3])) + lam_init
    q, k, v = _da_qkv(x, mods, g1, qkv_w, q_g, k_g, geo)
    o = _da_attn(q, k, v, lam, sub_g, lam_init, geo)
    return _matmul(o, out_w.astype(BF16))


def kernel(x, c, ctx, c_ctx, mod_w, mod_b, norm1_g, norm2_g, s5_a_re, s5_a_im, s5_log_dt, s5_b_re, s5_b_im, s5_c_re, s5_c_im, s5_d, s5_glu_w, s5_glu_b, ssd_in_w, ssd_conv_w, ssd_conv_b, ssd_dt_bias, ssd_a_log, ssd_d, ssd_norm_g, ssd_out_w, da_qkv_w, da_q_g, da_k_g, da_lam, da_sub_g, da_out_w, moe_router_w, moe_router_b, moe_gu_w, moe_gu_b, moe_dn_w, moe_dn_b):
    bsz, n_lat, d = x.shape
    n_ctx = ctx.shape[1]
    depth = mod_w.shape[0]
    geo = Geo(bsz, n_ctx, n_lat)
    xs = jnp.concatenate([ctx, x], axis=1).reshape(geo.rows, d)
    mods_all = _mods(c, c_ctx, mod_w, mod_b)
    n_mixers = 3
    for i in range(depth):
        last = i == depth - 1
        kind, j = i % n_mixers, i // n_mixers
        mods = mods_all[i]
        if kind == 0:
            prm = (s5_a_re[j], s5_a_im[j], s5_log_dt[j], s5_b_re[j], s5_b_im[j], s5_c_re[j], s5_c_im[j],
                   s5_d[j], s5_glu_w[j], s5_glu_b[j])
            y = _s5_mixer(xs, mods, norm1_g[i], prm, geo)
        elif kind == 1:
            prm = (ssd_in_w[j], ssd_conv_w[j], ssd_conv_b[j], ssd_dt_bias[j], ssd_a_log[j], ssd_d[j],
                   ssd_norm_g[j], ssd_out_w[j])
            y = _ssd_mixer(xs, mods, norm1_g[i], prm, geo)
        else:
            prm = (da_qkv_w[j], da_q_g[j], da_k_g[j], da_lam[j], da_sub_g[j], da_out_w[j])
            y = _da_mixer(xs, mods, norm1_g[i], prm, i, geo)
        xs = _moe_layer(xs, y, mods, norm2_g[i], moe_router_w[i], moe_router_b[i], moe_gu_w[i], moe_gu_b[i],
                        moe_dn_w[i], moe_dn_b[i], geo, last)
    return xs.reshape(bsz, n_lat, d)
```

```python
import functools
import math
from typing import NamedTuple

import jax
import jax.numpy as jnp
from jax import lax
from jax.experimental import pallas as pl
from jax.experimental.pallas import tpu as pltpu

F32 = jnp.float32
BF16 = jnp.bfloat16
EPS = 1e-6
HP = lax.Precision.HIGHEST

LANES = 128
VMEM_LIMIT = 56 * 1024 * 1024

N_EXPERTS = 32
TOP_K = 4
SWIGLU_ALPHA = 1.702
SWIGLU_LIMIT = 7.0
MOE_BLK = 512
MOD_TILE = 256
ROW_TILE = 512


class Geo(NamedTuple):
    n_batch: int
    n_ctx: int
    n_lat: int

    @property
    def rows_per_batch(self):
        return self.n_ctx + self.n_lat

    @property
    def rows(self):
        return self.n_batch * self.rows_per_batch

    def tiles(self, tm):
        return self.rows_per_batch // tm, self.n_ctx // tm, self.n_lat // tm


def _cp(sem, vmem=VMEM_LIMIT):
    return pltpu.CompilerParams(dimension_semantics=sem, vmem_limit_bytes=vmem)


def _tile_maps(geo, tm, lat_only):
    tpb, ct, lt = geo.tiles(tm)
    if lat_only:
        src = lambda i: (i // lt) * tpb + ct + i % lt
        mod = lambda i: i // lt
        return geo.n_batch * lt, src, mod
    mod = lambda i: jnp.where(i % tpb < ct, geo.n_batch, i // tpb)
    return geo.n_batch * tpb, (lambda i: i), mod


def _rms(x):
    return x * lax.rsqrt(jnp.mean(x * x, axis=-1, keepdims=True) + EPS)


def _mods_kernel(s_ref, w_ref, b_ref, o_ref):
    s = s_ref[...]
    s = s * jax.nn.sigmoid(s)
    o_ref[0] = jnp.dot(s, w_ref[0], preferred_element_type=F32, precision=HP) + b_ref[0]


def _mods(c, c_ctx, mod_w, mod_b):
    depth, d, d6 = mod_w.shape
    bsz = c.shape[0]
    rows = -(-(bsz + 1) // 8) * 8
    s = jnp.zeros((rows, d), F32).at[:bsz].set(c).at[bsz].set(c_ctx)
    nj = d6 // d
    out = pl.pallas_call(
        _mods_kernel,
        grid=(depth, nj),
        in_specs=[
            pl.BlockSpec((rows, d), lambda l, j: (0, 0)),
            pl.BlockSpec((1, d, d), lambda l, j: (l, 0, j)),
            pl.BlockSpec((1, 1, d), lambda l, j: (l, 0, j)),
        ],
        out_specs=pl.BlockSpec((1, rows, d), lambda l, j: (l, 0, j)),
        out_shape=jax.ShapeDtypeStruct((depth, rows, d6), F32),
        compiler_params=_cp(("arbitrary", "arbitrary")),
        name="mods",
    )(s, mod_w, mod_b.reshape(depth, 1, d6))
    out = out[:, :bsz + 1].reshape(depth, bsz + 1, nj, d)
    return jnp.pad(out, ((0, 0), (0, 0), (0, 8 - nj), (0, 0)))


def _to_row_tiles(ref, val):
    rows, d = val.shape
    sub = d // LANES
    for s in range(sub):
        ref[pl.ds(s, rows, stride=sub), :] = val[:, s * LANES:(s + 1) * LANES]


def _from_row_tiles(ref, rows, sub, dtype=None):
    parts = [ref[pl.ds(s, rows, stride=sub), :] for s in range(sub)]
    if dtype is not None:
        parts = [p.astype(dtype) for p in parts]
    return jnp.concatenate(parts, axis=1)


def _route_kernel(x_ref, y_ref, mod_ref, g_ref, rwh_ref, rwl_ref, rb_ref, tri_ref,
                  xn_ref, h_ref, route_ref, cnt_ref, base_ref):
    i = pl.program_id(0)

    @pl.when(i == 0)
    def _():
        base_ref[...] = jnp.zeros_like(base_ref)

    m = mod_ref[0]
    x = x_ref[...] + m[2:3, :] * y_ref[...]
    xn_ref[...] = x
    h = _rms(x) * g_ref[...] * (1.0 + m[4:5, :]) + m[3:4, :]
    _to_row_tiles(h_ref, h)
    dot = functools.partial(jnp.dot, preferred_element_type=F32)
    h_hi = h.astype(BF16)
    h_lo = (h - h_hi.astype(F32)).astype(BF16)
    logits = dot(h_hi, rwh_ref[...]) + dot(h_lo, rwh_ref[...]) + dot(h_hi, rwl_ref[...]) + rb_ref[...]
    tm = logits.shape[0]
    lane = lax.broadcasted_iota(jnp.int32, (tm, LANES), 1).astype(F32)
    neg = jnp.float32(-jnp.inf)
    work = jnp.where(lane < N_EXPERTS, logits, neg)
    vals, ohs, idxs = [], [], []
    for _ in range(TOP_K):
        mk = jnp.max(work, axis=-1, keepdims=True)
        ik = jnp.min(jnp.where(work == mk, lane, float(LANES)), axis=-1, keepdims=True)
        oh = lane == ik
        work = jnp.where(oh, neg, work)
        vals.append(mk)
        ohs.append(oh)
        idxs.append(ik)
    es = [jnp.exp(v - vals[0]) for v in vals]
    den = es[0] + es[1] + es[2] + es[3]
    onehot = jnp.zeros((tm, LANES), F32)
    for oh in ohs:
        onehot = onehot + jnp.where(oh, 1.0, 0.0)
    prefix = jnp.dot(tri_ref[...], onehot.astype(BF16), preferred_element_type=F32)
    pos = prefix + base_ref[0:1, :]
    out = jnp.zeros((tm, LANES), F32)
    for k in range(TOP_K):
        rank = jnp.sum(jnp.where(ohs[k], pos, 0.0), axis=-1, keepdims=True)
        out = out + jnp.where(lane == k, idxs[k], 0.0)
        out = out + jnp.where(lane == TOP_K + k, rank, 0.0)
        out = out + jnp.where(lane == 2 * TOP_K + k, es[k] / den, 0.0)
    route_ref[...] = out
    newbase = base_ref[0:1, :] + jnp.sum(onehot, axis=0, keepdims=True)
    base_ref[...] = jnp.broadcast_to(newbase, base_ref.shape)
    cnt_ref[...] = jnp.broadcast_to(newbase, cnt_ref.shape)


def _route(x, y, mods, g2, rw, rb, geo, lat_only, tm=MOD_TILE):
    d = x.shape[1]
    sub = d // LANES
    nt, src, mod = _tile_maps(geo, tm, lat_only)
    n = nt * tm
    rw_p = jnp.zeros((d, LANES), F32).at[:, :N_EXPERTS].set(rw)
    rw_hi = rw_p.astype(BF16)
    rw_lo = (rw_p - rw_hi.astype(F32)).astype(BF16)
    rb_p = jnp.zeros((1, LANES), F32).at[0, :N_EXPERTS].set(rb)
    tri = (jnp.arange(tm)[:, None] > jnp.arange(tm)[None, :]).astype(BF16)
    const = lambda i: (0, 0)
    return pl.pallas_call(
        _route_kernel,
        grid=(nt,),
        in_specs=[
            pl.BlockSpec((tm, d), lambda i: (src(i), 0)),
            pl.BlockSpec((tm, d), lambda i: (src(i), 0)),
            pl.BlockSpec((1, 8, d), lambda i: (mod(i), 0, 0)),
            pl.BlockSpec((1, d), const),
            pl.BlockSpec((d, LANES), const),
            pl.BlockSpec((d, LANES), const),
            pl.BlockSpec((1, LANES), const),
            pl.BlockSpec((tm, tm), const),
        ],
        out_specs=[
            pl.BlockSpec((tm, d), lambda i: (i, 0)),
            pl.BlockSpec((tm * sub, LANES), lambda i: (i, 0)),
            pl.BlockSpec((tm, LANES), lambda i: (i, 0)),
            pl.BlockSpec((8, LANES), const),
        ],
        out_shape=[
            jax.ShapeDtypeStruct((n, d), F32),
            jax.ShapeDtypeStruct((n * sub, LANES), F32),
            jax.ShapeDtypeStruct((n, LANES), F32),
            jax.ShapeDtypeStruct((8, LANES), F32),
        ],
        scratch_shapes=[pltpu.VMEM((8, LANES), F32)],
        compiler_params=_cp(("arbitrary",)),
        name="route",
    )(x, y, mods, g2.reshape(1, d), rw_hi, rw_lo, rb_p, tri)


def _dispatch_kernel(last_ref, dest_ref, h_ref, xs_ref, zero_ref, sem, zsem, *, sub, blk):
    i = pl.program_id(0)
    tm = h_ref.shape[0] // sub

    @pl.when(i == 0)
    def _():
        zero_ref[...] = jnp.zeros_like(zero_ref)

        def zcopy(e):
            return pltpu.make_async_copy(zero_ref, xs_ref.at[pl.ds(last_ref[e] * (blk * sub), blk * sub), :], zsem)

        def zstart(e, c):
            @pl.when(last_ref[e] >= 0)
            def _():
                zcopy(e).start()
            return c

        def zwait(e, c):
            @pl.when(last_ref[e] >= 0)
            def _():
                zcopy(e).wait()
            return c

        lax.fori_loop(0, N_EXPERTS, zstart, 0)
        lax.fori_loop(0, N_EXPERTS, zwait, 0)

    tok_per_row = LANES // TOP_K

    def start(row, c):
        for ln in range(LANES):
            src0 = pl.multiple_of(row * (tok_per_row * sub) + (ln // TOP_K) * sub, sub)
            dst0 = pl.multiple_of(dest_ref[row, ln] * sub, sub)
            pltpu.make_async_copy(h_ref.at[pl.ds(src0, sub), :], xs_ref.at[pl.ds(dst0, sub), :], sem).start(
                priority=ln % 2)
        return c

    lax.fori_loop(0, tm // tok_per_row, start, 0)
    for k in range(TOP_K):
        pltpu.make_async_copy(h_ref, xs_ref.at[pl.ds(0, tm * sub), :], sem).wait()


def _dispatch(h, dest2d, last_blk, n_slots, sub, tm=ROW_TILE):
    n = h.shape[0] // sub
    nt = n // tm
    db = tm * TOP_K // LANES
    grid_spec = pltpu.PrefetchScalarGridSpec(
        num_scalar_prefetch=1,
        grid=(nt,),
        in_specs=[
            pl.BlockSpec((db, LANES), lambda i, lb: (i, 0), memory_space=pltpu.SMEM),
            pl.BlockSpec((tm * sub, LANES), lambda i, lb: (i, 0)),
        ],
        out_specs=pl.BlockSpec(memory_space=pl.ANY),
        scratch_shapes=[pltpu.VMEM((MOE_BLK * sub, LANES), F32), pltpu.SemaphoreType.DMA(()),
                        pltpu.SemaphoreType.DMA(())],
    )
    return pl.pallas_call(
        functools.partial(_dispatch_kernel, sub=sub, blk=MOE_BLK),
        grid_spec=grid_spec,
        out_shape=jax.ShapeDtypeStruct((n_slots * sub, LANES), F32),
        compiler_params=_cp(("arbitrary",)),
        name="dispatch",
    )(last_blk, dest2d, h)


def _expert_kernel(be_ref, na_ref, xs_ref, guw_ref, gub_ref, dnw_ref, dnb_ref, ys_ref, guw_bf, dnw_bf):
    j = pl.program_id(0)
    active = j < na_ref[0]

    @pl.when(jnp.logical_and(active, jnp.logical_or(j == 0, be_ref[j] != be_ref[jnp.maximum(j - 1, 0)])))
    def _():
        guw_bf[...] = guw_ref[0].astype(BF16)
        dnw_bf[...] = dnw_ref[0].astype(BF16)

    @pl.when(active)
    def _():
        f, d = dnw_ref.shape[1], dnw_ref.shape[2]
        sub = d // LANES
        x = _from_row_tiles(xs_ref, xs_ref.shape[0] // sub, sub, BF16)
        gu = jnp.dot(x, guw_bf[...], preferred_element_type=F32) + gub_ref[0]
        gate = jnp.minimum(gu[:, :f], SWIGLU_LIMIT)
        up = jnp.clip(gu[:, f:], -SWIGLU_LIMIT, SWIGLU_LIMIT)
        act = (up + 1.0) * gate * jax.nn.sigmoid(SWIGLU_ALPHA * gate)
        y = jnp.dot(act.astype(BF16), dnw_bf[...], preferred_element_type=F32) + dnb_ref[0]
        _to_row_tiles(ys_ref, y)


def _experts(xs, block_e, n_active, guw, gub, dnw, dnb, blk=MOE_BLK):
    e, d, f2 = guw.shape
    f = f2 // 2
    sub = d // LANES
    nb = xs.shape[0] // (blk * sub)
    act_blk = lambda j, be, na: (jnp.minimum(j, na[0] - 1), 0)
    wsel = lambda j, be, na: (be[j], 0, 0)
    grid_spec = pltpu.PrefetchScalarGridSpec(
        num_scalar_prefetch=2,
        grid=(nb,),
        in_specs=[
            pl.BlockSpec((blk * sub, LANES), act_blk),
            pl.BlockSpec((1, d, f2), wsel),
            pl.BlockSpec((1, 1, f2), wsel),
            pl.BlockSpec((1, f, d), wsel),
            pl.BlockSpec((1, 1, d), wsel),
        ],
        out_specs=pl.BlockSpec((blk * sub, LANES), act_blk),
        scratch_shapes=[pltpu.VMEM((d, f2), BF16), pltpu.VMEM((f, d), BF16)],
    )
    return pl.pallas_call(
        _expert_kernel,
        grid_spec=grid_spec,
        out_shape=jax.ShapeDtypeStruct(xs.shape, F32),
        compiler_params=_cp(("arbitrary",)),
        name="experts",
    )(block_e, n_active, xs, guw, gub.reshape(e, 1, f2), dnw, dnb.reshape(e, 1, d))


def _combine_kernel(dest_ref, x_ref, route_ref, mod_ref, ys_ref, o_ref, buf, sem):
    tm, d = x_ref.shape
    sub = d // LANES

    tok_per_row = LANES // TOP_K

    def start(row, c):
        for ln in range(LANES):
            src0 = pl.multiple_of(dest_ref[row, ln] * sub, sub)
            dst0 = pl.multiple_of(row * (tok_per_row * sub) + (ln // TOP_K) * sub, sub)
            pltpu.make_async_copy(ys_ref.at[pl.ds(src0, sub), :], buf.at[ln % TOP_K, pl.ds(dst0, sub), :], sem).start(
                priority=ln % 2)
        return c

    lax.fori_loop(0, tm // tok_per_row, start, 0)
    for k in range(TOP_K):
        pltpu.make_async_copy(ys_ref.at[pl.ds(0, tm * sub), :], buf.at[k], sem).wait()
    route = route_ref[...]
    gates = [route[:, 2 * TOP_K + k:2 * TOP_K + k + 1] for k in range(TOP_K)]
    scale = mod_ref[0][5:6, :]
    for s in range(sub):
        cols = slice(s * LANES, (s + 1) * LANES)
        f = gates[0] * buf[0, pl.ds(s, tm, stride=sub), :]
        for k in range(1, TOP_K):
            f = f + gates[k] * buf[k, pl.ds(s, tm, stride=sub), :]
        o_ref[:, cols] = x_ref[:, cols] + scale[:, cols] * f


def _combine(x, route, mods, ys, dest2d, geo, lat_only, tm=MOD_TILE):
    n, d = x.shape
    sub = d // LANES
    nt, _, mod = _tile_maps(geo, tm, lat_only)
    db = tm * TOP_K // LANES
    return pl.pallas_call(
        _combine_kernel,
        grid=(nt,),
        in_specs=[
            pl.BlockSpec((db, LANES), lambda i: (i, 0), memory_space=pltpu.SMEM),
            pl.BlockSpec((tm, d), lambda i: (i, 0)),
            pl.BlockSpec((tm, LANES), lambda i: (i, 0)),
            pl.BlockSpec((1, 8, d), lambda i: (mod(i), 0, 0)),
            pl.BlockSpec(memory_space=pl.ANY),
        ],
        out_specs=pl.BlockSpec((tm, d), lambda i: (i, 0)),
        out_shape=jax.ShapeDtypeStruct((n, d), F32),
        scratch_shapes=[pltpu.VMEM((TOP_K, tm * sub, LANES), F32), pltpu.SemaphoreType.DMA(())],
        compiler_params=_cp(("arbitrary",)),
        name="combine",
    )(dest2d, x, route, mods, ys)


def _moe_layer(x, y, mods, g2, rw, rb, guw, gub, dnw, dnb, geo, lat_only):
    xn, h, route, cnt = _route(x, y, mods, g2, rw, rb, geo, lat_only)
    n, d = xn.shape
    sub = d // LANES
    counts = cnt[0, :N_EXPERTS].astype(jnp.int32)
    nblk = (counts + MOE_BLK - 1) // MOE_BLK
    blk_end = jnp.cumsum(nblk)
    pad_start = (blk_end - nblk) * MOE_BLK
    n_blocks = -(-(n * TOP_K) // MOE_BLK) + N_EXPERTS
    block_e = jnp.minimum(jnp.sum(blk_end[None, :] <= jnp.arange(n_blocks)[:, None], axis=1), N_EXPERTS - 1)
    n_active = blk_end[-1:].astype(jnp.int32)
    last_blk = jnp.where(nblk > 0, blk_end - 1, -1).astype(jnp.int32)
    eid = route[:, :TOP_K].astype(jnp.int32)
    rank = route[:, TOP_K:2 * TOP_K].astype(jnp.int32)
    dest = (pad_start[eid] + rank).astype(jnp.int32).reshape(n * TOP_K // LANES, LANES)
    xs = _dispatch(h, dest, last_blk, n_blocks * MOE_BLK, sub)
    ys = _experts(xs, block_e.astype(jnp.int32), n_active, guw, gub, dnw, dnb)
    return _combine(xn, route, mods, ys, dest, geo, lat_only)


def _prenorm_kernel(x_ref, mod_ref, g_ref, h_ref):
    m = mod_ref[0]
    h = _rms(x_ref[...]) * g_ref[...] * (1.0 + m[1:2, :]) + m[0:1, :]
    h_ref[...] = h.astype(h_ref.dtype)


def _prenorm(x, mods, g1, geo, dtype, tm=MOD_TILE):
    n, d = x.shape
    nt, _, mod = _tile_maps(geo, tm, False)
    return pl.pallas_call(
        _prenorm_kernel,
        grid=(nt,),
        in_specs=[
            pl.BlockSpec((tm, d), lambda i: (i, 0)),
            pl.BlockSpec((1, 8, d), lambda i: (mod(i), 0, 0)),
            pl.BlockSpec((1, d), lambda i: (0, 0)),
        ],
        out_specs=pl.BlockSpec((tm, d), lambda i: (i, 0)),
        out_shape=jax.ShapeDtypeStruct((n, d), dtype),
        compiler_params=_cp(("parallel",)),
        name="prenorm",
    )(x, mods, g1.reshape(1, d))


S5_CHUNK = 16
S5_GB = 8


def _s5_matrices(a_re, a_im, log_dt, b_re, b_im, c_re, c_im):
    L = S5_CHUNK
    a_re = a_re.astype(F32)
    a_im = a_im.astype(F32)
    dt = jnp.exp(log_dt.astype(F32))[..., None]
    mag = jnp.exp(dt * a_re)
    ang = dt * a_im
    ab_re, ab_im = mag * jnp.cos(ang), mag * jnp.sin(ang)
    den = a_re * a_re + a_im * a_im
    num_re = ab_re - 1.0
    co_re = (num_re * a_re + ab_im * a_im) / den
    co_im = (ab_im * a_re - num_re * a_im) / den
    bb_re = co_re[..., None] * b_re - co_im[..., None] * b_im
    bb_im = co_re[..., None] * b_im + co_im[..., None] * b_re
    taus = jnp.arange(L + 1, dtype=F32)[:, None, None, None]
    pmag = jnp.exp(taus * dt * a_re)
    pang = taus * dt * a_im
    pw_re, pw_im = pmag * jnp.cos(pang), pmag * jnp.sin(pang)
    cp_re = c_re[None] * pw_re[:, :, :, None, :] - c_im[None] * pw_im[:, :, :, None, :]
    cp_im = c_re[None] * pw_im[:, :, :, None, :] + c_im[None] * pw_re[:, :, :, None, :]
    kk = (jnp.einsum('tdgjp,dgpi->tdgji', cp_re, bb_re, precision=HP)
          - jnp.einsum('tdgjp,dgpi->tdgji', cp_im, bb_im, precision=HP))
    t_idx = jnp.arange(L)
    lag = t_idx[None, :] - t_idx[:, None]

    def toeplitz(kd, lagm):
        kt = kd[jnp.clip(lagm, 0, L)]
        kt = jnp.where((lagm >= 0)[:, :, None, None, None], kt, 0.0)
        return kt.transpose(2, 0, 4, 1, 3)

    g = a_re.shape[1]
    p = a_re.shape[2]
    j = b_re.shape[3]
    m = (toeplitz(kk[:, 0], lag) + toeplitz(kk[:, 1], -lag)).reshape(g, L * j, L * j)

    def inject(d, steps):
        pr, pi = pw_re[steps, d], pw_im[steps, d]
        re = pr[:, :, :, None] * bb_re[d][None] - pi[:, :, :, None] * bb_im[d][None]
        im = pr[:, :, :, None] * bb_im[d][None] + pi[:, :, :, None] * bb_re[d][None]
        f = lambda z: z.transpose(1, 0, 3, 2).reshape(g, L * j, p)
        return f(re), f(im)

    inj = inject(0, L - 1 - t_idx) + inject(1, t_idx)

    def readout(d, steps):
        re = cp_re[steps, d]
        im = cp_im[steps, d]
        f = lambda z: z.transpose(1, 3, 0, 2).reshape(g, p, L * j)
        return f(re), f(-im)

    rd = readout(0, t_idx + 1) + readout(1, L - t_idx)

    gb = S5_GB
    nblk = g // gb

    def spread(z, width):
        z = z.astype(BF16)
        mask = (jnp.arange(gb)[:, None] == jnp.arange(gb * width)[None, :] // width).astype(BF16)
        mask = mask.reshape((1, gb) + (1,) * (z.ndim - 3) + (gb * width,))
        return jnp.tile(z, (1,) * (z.ndim - 1) + (gb,)) * mask

    m_blk = spread(m.reshape(nblk, gb, L, j, L, j), j).transpose(0, 2, 1, 3, 4, 5).reshape(nblk, L * gb * j, L * gb * j)

    def inj_blk(re, im):
        f = lambda z: spread(z.reshape(nblk, gb, L, j, p), p).transpose(0, 2, 1, 3, 4).reshape(nblk, L * gb * j, gb * p)
        return jnp.concatenate([f(re), f(im)], axis=2)

    def rd_blk(re, im):
        f = lambda z: spread(z.reshape(nblk, gb, p, L, j), j).reshape(nblk, gb * p, L * gb * j)
        return jnp.concatenate([f(re), f(im)], axis=1)

    def adv_blk(d):
        return jnp.concatenate([pw_re[L, d].reshape(nblk, 1, gb * p), pw_im[L, d].reshape(nblk, 1, gb * p)], axis=2)

    adv = jnp.pad(jnp.concatenate([adv_blk(0), adv_blk(1)], axis=1), ((0, 0), (0, 6), (0, 0)))
    return (m_blk, inj_blk(inj[0], inj[1]), inj_blk(inj[2], inj[3]), rd_blk(rd[0], rd[1]), rd_blk(rd[2], rd[3]), adv)


def _s5_chunk_vectors(x_ref):
    nb, rows, _ = x_ref.shape
    c = rows // S5_CHUNK
    parts = [x_ref[:, pl.ds(l, c, stride=S5_CHUNK), :].reshape(nb * c, LANES).astype(BF16) for l in range(S5_CHUNK)]
    return jnp.concatenate(parts, axis=1)


def _s5_put_states(sloc_ref, val):
    for k in range(sloc_ref.shape[0]):
        sloc_ref[k] = val[:, k * LANES:(k + 1) * LANES]


def _s5_get_states(sloc_ref):
    return jnp.concatenate([sloc_ref[k] for k in range(sloc_ref.shape[0])], axis=1)


def _s5_chunk_scan(sloc_ref, carry_ref, adv, nb, c, reverse):
    nk = sloc_ref.shape[0]
    half = nk // 2
    a = [jnp.broadcast_to(adv[:, k * LANES:(k + 1) * LANES], (nb, LANES)) for k in range(nk)]

    def step(i, st):
        ci = (c - 1 - i) if reverse else i
        rows = pl.ds(ci, nb, stride=c)
        new_re, new_im = [], []
        for k in range(half):
            l_re, l_im = sloc_ref[k, rows, :], sloc_ref[half + k, rows, :]
            s_re, s_im = st[k], st[half + k]
            sloc_ref[k, rows, :] = s_re
            sloc_ref[half + k, rows, :] = s_im
            new_re.append(a[k] * s_re - a[half + k] * s_im + l_re)
            new_im.append(a[k] * s_im + a[half + k] * s_re + l_im)
        return tuple(new_re + new_im)

    st = lax.fori_loop(0, c, step, tuple(carry_ref[k] for k in range(nk)))
    for k in range(nk):
        carry_ref[k] = st[k]


def _s5_bwd_kernel(x_ref, inj_ref, adv_ref, sn_ref, sloc_ref, carry_ref):
    @pl.when(pl.program_id(1) == 0)
    def _():
        carry_ref[...] = jnp.zeros_like(carry_ref)

    nb = x_ref.shape[0]
    c = x_ref.shape[1] // S5_CHUNK
    z = _s5_chunk_vectors(x_ref)
    _s5_put_states(sloc_ref, jnp.dot(z, inj_ref[0], preferred_element_type=F32))
    _s5_chunk_scan(sloc_ref, carry_ref, adv_ref[0][1:2, :], nb, c, True)
    sn_ref[0, 0] = _s5_get_states(sloc_ref).astype(sn_ref.dtype)


def _s5_fwd_kernel(x_ref, sn_ref, m_ref, inj_ref, rdf_ref, rdb_ref, adv_ref, y_ref, sloc_ref, carry_ref):
    @pl.when(pl.program_id(1) == 0)
    def _():
        carry_ref[...] = jnp.zeros_like(carry_ref)

    nb = x_ref.shape[0]
    c = x_ref.shape[1] // S5_CHUNK
    dot = functools.partial(jnp.dot, preferred_element_type=F32)
    z = _s5_chunk_vectors(x_ref)
    _s5_put_states(sloc_ref, dot(z, inj_ref[0]))
    _s5_chunk_scan(sloc_ref, carry_ref, adv_ref[0][0:1, :], nb, c, False)
    y = dot(z, m_ref[0]) + dot(_s5_get_states(sloc_ref).astype(BF16), rdf_ref[0]) + dot(sn_ref[0, 0], rdb_ref[0])
    for t in range(S5_CHUNK):
        y_ref[:, pl.ds(t, c, stride=S5_CHUNK), :] = y[:, t * LANES:(t + 1) * LANES].reshape(nb, c, LANES)


def _s5_core(h, mats, geo, tile=MOD_TILE):
    m, inj_f, inj_b, rd_f, rd_b, adv = mats
    nblk, kdim, sdim = inj_f.shape
    nb, rpb = geo.n_batch, geo.rows_per_batch
    d = h.shape[1]
    tpb, ct, _ = geo.tiles(tile)
    c = tile // S5_CHUNK
    h3 = h.reshape(nb, rpb, d)
    bwd_tile = lambda s: jnp.where(s < ct, ct - 1 - s, tpb - 1 - (s - ct))
    once = pl.Buffered(1)
    wspec = lambda shape: pl.BlockSpec((1,) + shape, lambda g, s: (g, 0, 0), pipeline_mode=once)
    scratch = [pltpu.VMEM((sdim // LANES, nb * c, LANES), F32), pltpu.VMEM((sdim // LANES, nb, LANES), F32)]
    sn = pl.pallas_call(
        _s5_bwd_kernel,
        grid=(nblk, tpb),
        in_specs=[
            pl.BlockSpec((nb, tile, LANES), lambda g, s: (0, bwd_tile(s), g)),
            wspec((kdim, sdim)),
            wspec((8, sdim)),
        ],
        out_specs=pl.BlockSpec((1, 1, nb * c, sdim), lambda g, s: (g, bwd_tile(s), 0, 0)),
        out_shape=jax.ShapeDtypeStruct((nblk, tpb, nb * c, sdim), BF16),
        scratch_shapes=scratch,
        compiler_params=_cp(("parallel", "arbitrary")),
        name="s5_bwd",
    )(h3, inj_b, adv)
    y = pl.pallas_call(
        _s5_fwd_kernel,
        grid=(nblk, tpb),
        in_specs=[
            pl.BlockSpec((nb, tile, LANES), lambda g, s: (0, s, g)),
            pl.BlockSpec((1, 1, nb * c, sdim), lambda g, s: (g, s, 0, 0)),
            wspec((kdim, kdim)),
            wspec((kdim, sdim)),
            wspec((sdim, kdim)),
            wspec((sdim, kdim)),
            wspec((8, sdim)),
        ],
        out_specs=pl.BlockSpec((nb, tile, LANES), lambda g, s: (0, s, g)),
        out_shape=jax.ShapeDtypeStruct((nb, rpb, d), F32),
        scratch_shapes=scratch,
        compiler_params=_cp(("parallel", "arbitrary")),
        name="s5_fwd",
    )(h3, sn, m, inj_f, rd_f, rd_b, adv)
    return y.reshape(nb * rpb, d)


def _gelu_tanh(x):
    return 0.5 * x * (1.0 + jnp.tanh(math.sqrt(2.0 / math.pi) * (x + 0.044715 * (x * x * x))))


def _s5_glu_kernel(h_ref, y_ref, d_ref, w_ref, b_ref, o_ref):
    y = d_ref[...] * h_ref[...].astype(F32) + y_ref[...].astype(F32)
    gl = _gelu_tanh(y).astype(BF16)
    z = jnp.dot(gl, w_ref[...], preferred_element_type=F32) + b_ref[...]
    dd = o_ref.shape[1]
    o_ref[...] = z[:, :dd] * jax.nn.sigmoid(z[:, dd:])


def _s5_glu(h, y, d_skip, glu_w, glu_b, tm=ROW_TILE):
    n, d = h.shape
    return pl.pallas_call(
        _s5_glu_kernel,
        grid=(n // tm,),
        in_specs=[
            pl.BlockSpec((tm, d), lambda i: (i, 0)),
            pl.BlockSpec((tm, d), lambda i: (i, 0)),
            pl.BlockSpec((1, d), lambda i: (0, 0)),
            pl.BlockSpec((d, 2 * d), lambda i: (0, 0)),
            pl.BlockSpec((1, 2 * d), lambda i: (0, 0)),
        ],
        out_specs=pl.BlockSpec((tm, d), lambda i: (i, 0)),
        out_shape=jax.ShapeDtypeStruct((n, d), F32),
        compiler_params=_cp(("parallel",)),
        name="s5_glu",
    )(h, y, d_skip.reshape(1, d).astype(F32), glu_w.astype(BF16), glu_b.reshape(1, 2 * d).astype(F32))


def _s5_mixer(x, mods, g1, prm, geo):
    a_re, a_im, log_dt, b_re, b_im, c_re, c_im, d_skip, glu_w, glu_b = prm
    h = _prenorm(x, mods, g1, geo, F32)
    mats = _s5_matrices(a_re, a_im, log_dt, b_re, b_im, c_re, c_im)
    y = _s5_core(h, mats, geo)
    return _s5_glu(h, y, d_skip, glu_w, glu_b)


SSD_Q = 256
SSD_HEAD_DIM = 64
SSD_STATE = 128
SSD_CONV = 5
CONV_HALO = 16


def _ssd_inproj_kernel(x_ref, mod_ref, g_ref, wz_ref, wx_ref, wd_ref, z_ref, xbc_ref, dt_ref):
    m = mod_ref[0]
    h = (_rms(x_ref[...]) * g_ref[...] * (1.0 + m[1:2, :]) + m[0:1, :]).astype(BF16)
    z_ref[...] = jnp.dot(h, wz_ref[...], preferred_element_type=F32).astype(z_ref.dtype)
    xbc_ref[...] = jnp.dot(h, wx_ref[...], preferred_element_type=F32).astype(xbc_ref.dtype)
    dt_ref[...] = jnp.dot(h, wd_ref[...], preferred_element_type=F32)


def _ssd_inproj(x, mods, g1, in_w, d_inner, n_bc, geo, tm=MOD_TILE):
    n, d = x.shape
    nt, _, mod = _tile_maps(geo, tm, False)
    wz = in_w[:, :d_inner].astype(BF16)
    wx = in_w[:, d_inner:2 * d_inner + n_bc].astype(BF16)
    wd = in_w[:, 2 * d_inner + n_bc:]
    wd = jnp.pad(wd, ((0, 0), (0, LANES - wd.shape[1]))).astype(BF16)
    cx = d_inner + n_bc
    const = lambda i: (0, 0)
    row = lambda i: (i, 0)
    return pl.pallas_call(
        _ssd_inproj_kernel,
        grid=(nt,),
        in_specs=[
            pl.BlockSpec((tm, d), row),
            pl.BlockSpec((1, 8, d), lambda i: (mod(i), 0, 0)),
            pl.BlockSpec((1, d), const),
            pl.BlockSpec((d, d_inner), const),
            pl.BlockSpec((d, cx), const),
            pl.BlockSpec((d, LANES), const),
        ],
        out_specs=[pl.BlockSpec((tm, d_inner), row), pl.BlockSpec((tm, cx), row), pl.BlockSpec((tm, LANES), row)],
        out_shape=[
            jax.ShapeDtypeStruct((n, d_inner), BF16),
            jax.ShapeDtypeStruct((n, cx), BF16),
            jax.ShapeDtypeStruct((n, LANES), F32),
        ],
        compiler_params=_cp(("parallel",)),
        name="ssd_inproj",
    )(x, mods, g1.reshape(1, d), wz, wx, wd)


def _ssd_conv_kernel(prev_ref, cur_ref, next_ref, w_ref, b_ref, o_ref, *, tiles_per_batch, ctx_tiles):
    j = pl.program_id(0) % tiles_per_batch
    first = jnp.logical_or(j == 0, j == ctx_tiles)
    last = jnp.logical_or(j == ctx_tiles - 1, j == tiles_per_batch - 1)
    half = CONV_HALO // 2
    cur = cur_ref[...].astype(F32)
    tt = cur.shape[0]
    prev = jnp.where(first, 0.0, prev_ref[...].astype(F32)[half:, :])
    nxt = jnp.where(last, 0.0, next_ref[...].astype(F32)[:half, :])
    ext = jnp.concatenate([prev, cur, nxt], axis=0)
    w = w_ref[...]
    pad = (SSD_CONV - 1) // 2
    acc = jnp.broadcast_to(b_ref[...], cur.shape)
    for k in range(SSD_CONV):
        off = half + k - pad
        acc = acc + w[k:k + 1, :] * ext[off:off + tt, :]
    o_ref[...] = (acc * jax.nn.sigmoid(acc)).astype(o_ref.dtype)


def _ssd_conv(xbc, conv_w, conv_b, geo, tt=MOD_TILE, tc=2048):
    n, cx = xbc.shape
    tpb, ct, _ = geo.tiles(tt)
    nt = n // tt
    hb = tt // CONV_HALO
    nhb = n // CONV_HALO
    wpad = jnp.pad(conv_w, ((0, 8 - conv_w.shape[0]), (0, 0))).astype(F32)
    return pl.pallas_call(
        functools.partial(_ssd_conv_kernel, tiles_per_batch=tpb, ctx_tiles=ct),
        grid=(nt, cx // tc),
        in_specs=[
            pl.BlockSpec((CONV_HALO, tc), lambda i, c: (jnp.maximum(i * hb - 1, 0), c)),
            pl.BlockSpec((tt, tc), lambda i, c: (i, c)),
            pl.BlockSpec((CONV_HALO, tc), lambda i, c: (jnp.minimum((i + 1) * hb, nhb - 1), c)),
            pl.BlockSpec((8, tc), lambda i, c: (0, c)),
            pl.BlockSpec((1, tc), lambda i, c: (0, c)),
        ],
        out_specs=pl.BlockSpec((tt, tc), lambda i, c: (i, c)),
        out_shape=jax.ShapeDtypeStruct((n, cx), BF16),
        compiler_params=_cp(("parallel", "parallel")),
        name="ssd_conv",
    )(xbc, xbc, xbc, wpad, conv_b.reshape(1, cx).astype(F32))


def _ssd_dt_kernel(raw_ref, bias_ref, a_ref, dt_ref, cs_ref, cst_ref, *, n_heads):
    q = raw_ref.shape[0]
    dt = jax.nn.softplus(raw_ref[...] + bias_ref[...])
    dta = dt * a_ref[...]
    r = lax.broadcasted_iota(jnp.int32, (q, q), 0)
    c = lax.broadcasted_iota(jnp.int32, (q, q), 1)
    lower = jnp.where(c <= r, 1.0, 0.0)
    upper = jnp.where(c >= r, 1.0, 0.0)
    cs_f = jnp.dot(lower, dta, preferred_element_type=F32, precision=HP)
    cs_b = jnp.dot(upper, dta, preferred_element_type=F32, precision=HP)
    lane = lax.broadcasted_iota(jnp.int32, dt.shape, 1)
    cs = jnp.where(lane < n_heads, cs_f, cs_b)
    dt_ref[...] = dt
    cs_ref[...] = cs
    cst_ref[0] = cs.T


def _ssd_dt(dt_raw, dt_bias, a_log, n_heads):
    n = dt_raw.shape[0]
    q = SSD_Q
    pad = lambda v: jnp.pad(v.reshape(1, -1).astype(F32), ((0, 0), (0, LANES - v.size)))
    return pl.pallas_call(
        functools.partial(_ssd_dt_kernel, n_heads=n_heads),
        grid=(n // q,),
        in_specs=[
            pl.BlockSpec((q, LANES), lambda i: (i, 0)),
            pl.BlockSpec((1, LANES), lambda i: (0, 0)),
            pl.BlockSpec((1, LANES), lambda i: (0, 0)),
        ],
        out_specs=[
            pl.BlockSpec((q, LANES), lambda i: (i, 0)),
            pl.BlockSpec((q, LANES), lambda i: (i, 0)),
            pl.BlockSpec((1, LANES, q), lambda i: (i, 0, 0)),
        ],
        out_shape=[
            jax.ShapeDtypeStruct((n, LANES), F32),
            jax.ShapeDtypeStruct((n, LANES), F32),
            jax.ShapeDtypeStruct((n // q, LANES, q), F32),
        ],
        compiler_params=_cp(("parallel",)),
        name="ssd_dt",
    )(dt_raw, pad(dt_bias), pad(-jnp.exp(a_log.astype(F32))))


def _ssd_scan_kernel(x_ref, b_ref, c_ref, dt_ref, cs_ref, cst_ref, dsk_ref, y_ref, st_ref, *,
                     n_heads, heads_per_group, ctx_chunks, lat_chunks):
    g = pl.program_id(1)
    q = SSD_Q
    hd = SSD_HEAD_DIM
    width = heads_per_group * hd
    dot = functools.partial(jnp.dot, preferred_element_type=F32)
    lane = lax.broadcasted_iota(jnp.int32, (q, LANES), 1)
    blk = lax.broadcasted_iota(jnp.int32, (q, width), 1) // hd
    r_i = lax.broadcasted_iota(jnp.int32, (q, q), 0)
    c_i = lax.broadcasted_iota(jnp.int32, (q, q), 1)

    def spread(cols):
        out = jnp.broadcast_to(cols[0], (q, width))
        for hh in range(1, heads_per_group):
            out = jnp.where(blk == hh, cols[hh], out)
        return out

    def chunk(ci, direction, first_pass):
        r0 = ci * q if isinstance(ci, int) else pl.multiple_of(ci * q, q)
        rows = pl.ds(r0, q)
        xb = x_ref[rows, :].astype(F32)
        bm = b_ref[rows, :]
        cm = c_ref[rows, :]
        dt = dt_ref[rows, :]
        cs = cs_ref[rows, :]
        col0 = direction * n_heads + g * heads_per_group
        dt_cols, cs_cols, cs_rows = [], [], []
        for hh in range(heads_per_group):
            sel = lane == col0 + hh
            dt_cols.append(jnp.sum(jnp.where(sel, dt, 0.0), axis=1, keepdims=True))
            cs_cols.append(jnp.sum(jnp.where(sel, cs, 0.0), axis=1, keepdims=True))
            cs_rows.append(cst_ref[ci, pl.ds(col0 + hh, 1), :])
        xdt = xb * spread(dt_cols)
        csf = spread(cs_cols)
        edge = csf[q - 1:q, :] if direction == 0 else csf[0:1, :]
        scores = lax.dot_general(cm, bm, (((1,), (1,)), ((), ())), preferred_element_type=F32)
        keep = (c_i <= r_i) if direction == 0 else (c_i >= r_i)
        xdt_b = xdt.astype(BF16)
        state = st_ref[...]
        y = dot(cm, state.astype(BF16)) * jnp.exp(csf)
        for hh in range(heads_per_group):
            seg = jnp.where(keep, cs_cols[hh] - cs_rows[hh], -jnp.inf)
            pm = (scores * jnp.exp(seg)).astype(BF16)
            y = y + dot(pm, jnp.where(blk == hh, xdt_b, jnp.zeros_like(xdt_b)))
        w = (xdt * jnp.exp(edge - csf)).astype(BF16)
        bt = bm.astype(F32).T.astype(BF16)
        st_ref[...] = jnp.exp(edge) * state + dot(bt, w)
        if first_pass:
            y_ref[rows, :] = y + dsk_ref[0] * xb
        else:
            y_ref[rows, :] = y_ref[rows, :] + y

    for direction in range(2):
        st_ref[...] = jnp.zeros_like(st_ref)
        for ci in range(ctx_chunks):
            cc = ci if direction == 0 else ctx_chunks - 1 - ci
            chunk(cc, direction, direction == 0)

        def body(i, carry, direction=direction):
            cc = ctx_chunks + (i if direction == 0 else lat_chunks - 1 - i)
            chunk(cc, direction, direction == 0)
            return carry

        lax.fori_loop(0, lat_chunks, body, 0)


def _ssd_scan(xbc, dt, cs, cst, d_skip, geo, *, n_heads, n_groups):
    n = xbc.shape[0]
    rpb = geo.rows_per_batch
    hpg = n_heads // n_groups
    width = hpg * SSD_HEAD_DIM
    d_inner = n_heads * SSD_HEAD_DIM
    xblocks = d_inner // width
    bblocks = d_inner // SSD_STATE
    cpb = rpb // SSD_Q
    dsk = jnp.repeat(d_skip.astype(F32), SSD_HEAD_DIM).reshape(n_groups, 1, width)
    kern = functools.partial(_ssd_scan_kernel, n_heads=n_heads, heads_per_group=hpg,
                             ctx_chunks=geo.n_ctx // SSD_Q, lat_chunks=geo.n_lat // SSD_Q)
    return pl.pallas_call(
        kern,
        grid=(geo.n_batch, n_groups),
        in_specs=[
            pl.BlockSpec((rpb, width), lambda b, g: (b, g)),
            pl.BlockSpec((rpb, SSD_STATE), lambda b, g: (b, bblocks + g)),
            pl.BlockSpec((rpb, SSD_STATE), lambda b, g: (b, bblocks + n_groups + g)),
            pl.BlockSpec((rpb, LANES), lambda b, g: (b, 0)),
            pl.BlockSpec((rpb, LANES), lambda b, g: (b, 0)),
            pl.BlockSpec((cpb, LANES, SSD_Q), lambda b, g: (b, 0, 0)),
            pl.BlockSpec((1, 1, width), lambda b, g: (g, 0, 0)),
        ],
        out_specs=pl.BlockSpec((rpb, width), lambda b, g: (b, g)),
        out_shape=jax.ShapeDtypeStruct((n, d_inner), F32),
        scratch_shapes=[pltpu.VMEM((SSD_STATE, width), F32)],
        compiler_params=_cp(("parallel", "arbitrary")),
        name="ssd_scan",
    )(xbc, xbc, xbc, dt, cs, cst, dsk)


def _ssd_out_kernel(y_ref, z_ref, g_ref, w_ref, o_ref, *, n_groups):
    z = z_ref[...].astype(F32)
    gy = y_ref[...] * (z * jax.nn.sigmoid(z))
    gw = gy.shape[1] // n_groups
    acc = jnp.zeros(o_ref.shape, F32)
    for gi in range(n_groups):
        sl = slice(gi * gw, (gi + 1) * gw)
        ng = (_rms(gy[:, sl]) * g_ref[:, sl]).astype(BF16)
        acc = acc + jnp.dot(ng, w_ref[sl, :], preferred_element_type=F32)
    o_ref[...] = acc


def _ssd_out(y, z, norm_g, out_w, n_groups, tm=ROW_TILE):
    n, di = y.shape
    d = out_w.shape[1]
    return pl.pallas_call(
        functools.partial(_ssd_out_kernel, n_groups=n_groups),
        grid=(n // tm,),
        in_specs=[
            pl.BlockSpec((tm, di), lambda i: (i, 0)),
            pl.BlockSpec((tm, di), lambda i: (i, 0)),
            pl.BlockSpec((1, di), lambda i: (0, 0)),
            pl.BlockSpec((di, d), lambda i: (0, 0)),
        ],
        out_specs=pl.BlockSpec((tm, d), lambda i: (i, 0)),
        out_shape=jax.ShapeDtypeStruct((n, d), F32),
        compiler_params=_cp(("parallel",)),
        name="ssd_out",
    )(y, z, norm_g.reshape(1, di).astype(F32), out_w.astype(BF16))


def _ssd_mixer(x, mods, g1, prm, geo):
    in_w, conv_w, conv_b, dt_bias, a_log, d_skip, norm_g, out_w = prm
    n_heads = a_log.shape[1]
    d_inner = out_w.shape[0]
    n_bc = conv_w.shape[1] - d_inner
    n_groups = n_bc // (2 * SSD_STATE)
    z, xbc_raw, dt_raw = _ssd_inproj(x, mods, g1, in_w, d_inner, n_bc, geo)
    xbc = _ssd_conv(xbc_raw, conv_w, conv_b, geo)
    dt, cs, cst = _ssd_dt(dt_raw, dt_bias, a_log, n_heads)
    y = _ssd_scan(xbc, dt, cs, cst, d_skip, geo, n_heads=n_heads, n_groups=n_groups)
    return _ssd_out(y, z, norm_g, out_w, n_groups)


DA_HEAD_DIM = 64
ROPE_BASE = 10000.0
GRID_W = 64


def _rope_tables(geo, tm):
    hd = DA_HEAD_DIM
    n_freq = hd // 4
    t = jnp.arange(geo.n_lat, dtype=F32)
    row, col = jnp.floor(t / GRID_W), jnp.mod(t, GRID_W)
    inv = ROPE_BASE ** (-jnp.arange(n_freq, dtype=F32) / n_freq)
    lane = jnp.arange(LANES)
    within = lane % hd
    freq = inv[within % n_freq]
    pos = jnp.where((within < hd // 2)[None, :], row[:, None], col[:, None])
    ang = pos * freq[None, :]
    sign = jnp.where((within % (2 * n_freq)) < n_freq, -1.0, 1.0)
    cos = jnp.concatenate([jnp.cos(ang), jnp.ones((tm, LANES), F32)], axis=0)
    sin = jnp.concatenate([jnp.sin(ang) * sign[None, :], jnp.zeros((tm, LANES), F32)], axis=0)
    return cos, sin


def _da_qkv_kernel(x_ref, mod_ref, g_ref, w_ref, cos_ref, sin_ref, qg_ref, kg_ref, seg_ref,
                   q_ref, k_ref, v_ref):
    m = mod_ref[0]
    h = (_rms(x_ref[...]) * g_ref[...] * (1.0 + m[1:2, :]) + m[0:1, :]).astype(BF16)
    width = q_ref.shape[1]
    tm = h.shape[0]
    cos = cos_ref[...]
    sin = sin_ref[...]
    lane = lax.broadcasted_iota(jnp.int32, (tm, LANES), 1)
    n_freq = DA_HEAD_DIM // 4
    first_half = (lane % (2 * n_freq)) < n_freq
    scale = DA_HEAD_DIM ** -0.5 * math.log2(math.e)
    v_ref[...] = jnp.dot(h, w_ref[:, 2 * width:], preferred_element_type=F32).astype(v_ref.dtype)
    for o_ref, off, gain, mult in ((q_ref, 0, qg_ref, scale), (k_ref, width, kg_ref, 1.0)):
        for blk in range(width // LANES):
            cols = slice(off + blk * LANES, off + (blk + 1) * LANES)
            a = jnp.dot(h, w_ref[:, cols], preferred_element_type=F32)
            ms = jnp.dot((a * a).astype(BF16), seg_ref[...], preferred_element_type=F32)
            a = a * lax.rsqrt(ms + EPS) * gain[...]
            partner = jnp.where(first_half, pltpu.roll(a, LANES - n_freq, 1), pltpu.roll(a, n_freq, 1))
            a = a * cos + partner * sin
            o_ref[:, blk * LANES:(blk + 1) * LANES] = (a * mult).astype(o_ref.dtype)


def _da_qkv(x, mods, g1, qkv_w, q_g, k_g, geo, tm=MOD_TILE):
    n, d = x.shape
    width = qkv_w.shape[1] // 3
    nt, _, mod = _tile_maps(geo, tm, False)
    tpb, ct, lt = geo.tiles(tm)
    cos, sin = _rope_tables(geo, tm)
    tab = lambda i: (jnp.where(i % tpb < ct, lt, i % tpb - ct), 0)
    lane = jnp.arange(LANES)
    seg = jnp.where((lane[:, None] // DA_HEAD_DIM) == (lane[None, :] // DA_HEAD_DIM), 1.0 / DA_HEAD_DIM, 0.0)
    tile2 = lambda v: jnp.tile(v.astype(F32), LANES // DA_HEAD_DIM).reshape(1, LANES)
    const = lambda i: (0, 0)
    row = lambda i: (i, 0)
    return pl.pallas_call(
        _da_qkv_kernel,
        grid=(nt,),
        in_specs=[
            pl.BlockSpec((tm, d), row),
            pl.BlockSpec((1, 8, d), lambda i: (mod(i), 0, 0)),
            pl.BlockSpec((1, d), const),
            pl.BlockSpec((d, 3 * width), const),
            pl.BlockSpec((tm, LANES), tab),
            pl.BlockSpec((tm, LANES), tab),
            pl.BlockSpec((1, LANES), const),
            pl.BlockSpec((1, LANES), const),
            pl.BlockSpec((LANES, LANES), const),
        ],
        out_specs=[pl.BlockSpec((tm, width), row)] * 3,
        out_shape=[jax.ShapeDtypeStruct((n, width), BF16)] * 3,
        compiler_params=_cp(("parallel",)),
        name="da_qkv",
    )(x, mods, g1.reshape(1, d), qkv_w.astype(BF16), cos, sin, tile2(q_g), tile2(k_g), seg.astype(BF16))


DA_KEY_CHUNK = 256


def _da_attn_kernel(lam_ref, q_ref, k_ref, v_ref, sg_ref, o_ref, vx_ref, acc_ref, *, ctx_tiles, ctx_chunks,
                    all_chunks, out_scale):
    qi = pl.program_id(2)
    hw = v_ref.shape[1]
    kc = DA_KEY_CHUNK

    @pl.when(qi == 0)
    def _():
        vx_ref[:, :hw] = v_ref[...]
        vx_ref[:, hw:] = jnp.ones((vx_ref.shape[0], hw), vx_ref.dtype)

    q = q_ref[...]
    tq = q.shape[0]
    lane = lax.broadcasted_iota(jnp.int32, q.shape, 1)
    zero = jnp.zeros_like(q)
    qs = (jnp.where(lane < DA_HEAD_DIM, q, zero), jnp.where(lane < DA_HEAD_DIM, zero, q))
    nt = (((1,), (1,)), ((), ()))
    acc_ref[...] = jnp.zeros_like(acc_ref)

    def step(j, ms):
        r0 = j * kc if isinstance(j, int) else pl.multiple_of(j * kc, kc)
        kb = k_ref[pl.ds(r0, kc), :]
        vb = vx_ref[pl.ds(r0, kc), :]
        new = []
        for c in range(2):
            s = lax.dot_general(qs[c], kb, nt, preferred_element_type=F32)
            mn = jnp.maximum(ms[c], jnp.max(s, axis=-1, keepdims=True))
            e = jnp.exp2(s - mn).astype(BF16)
            acc_ref[c] = jnp.exp2(ms[c] - mn) * acc_ref[c] + jnp.dot(e, vb, preferred_element_type=F32)
            new.append(mn)
        return tuple(new)

    neg = jnp.full((tq, 1), -jnp.inf, F32)
    ms = (neg, neg)
    for j in range(ctx_chunks):
        ms = step(j, ms)

    @pl.when(qi >= ctx_tiles)
    def _():
        mm = ms
        for j in range(ctx_chunks, all_chunks):
            mm = step(j, mm)

    o = acc_ref[0][:, :hw] / acc_ref[0][:, hw:] - lam_ref[0] * (acc_ref[1][:, :hw] / acc_ref[1][:, hw:])
    o_ref[...] = (_rms(o) * sg_ref[...] * out_scale).astype(o_ref.dtype)


def _da_attn(q, k, v, lam, sub_g, lam_init, geo, tq=MOD_TILE):
    n, width = q.shape
    hw = 2 * DA_HEAD_DIM
    n_heads = width // hw
    rpb = geo.rows_per_batch
    tpb, ct, _ = geo.tiles(tq)
    grid_spec = pltpu.PrefetchScalarGridSpec(
        num_scalar_prefetch=0,
        grid=(geo.n_batch, n_heads, tpb),
        in_specs=[
            pl.BlockSpec(memory_space=pltpu.SMEM),
            pl.BlockSpec((tq, hw), lambda b, h, i: (b * tpb + i, h)),
            pl.BlockSpec((rpb, hw), lambda b, h, i: (b, h)),
            pl.BlockSpec((rpb, hw), lambda b, h, i: (b, h)),
            pl.BlockSpec((1, hw), lambda b, h, i: (0, 0)),
        ],
        out_specs=pl.BlockSpec((tq, hw), lambda b, h, i: (b * tpb + i, h)),
        scratch_shapes=[pltpu.VMEM((rpb, 2 * hw), BF16), pltpu.VMEM((2, tq, 2 * hw), F32)],
    )
    return pl.pallas_call(
        functools.partial(_da_attn_kernel, ctx_tiles=ct, ctx_chunks=geo.n_ctx // DA_KEY_CHUNK,
                          all_chunks=rpb // DA_KEY_CHUNK, out_scale=1.0 - lam_init),
        grid_spec=grid_spec,
        out_shape=jax.ShapeDtypeStruct((n, width), BF16),
        compiler_params=_cp(("parallel", "parallel", "arbitrary")),
        name="da_attn",
    )(lam.reshape(1).astype(F32), q, k, v, sub_g.reshape(1, hw).astype(F32))


def _matmul_kernel(x_ref, w_ref, o_ref):
    o_ref[...] = jnp.dot(x_ref[...], w_ref[...], preferred_element_type=F32)


def _matmul(x, w, tm=ROW_TILE):
    n, kd = x.shape
    d = w.shape[1]
    return pl.pallas_call(
        _matmul_kernel,
        grid=(n // tm,),
        in_specs=[pl.BlockSpec((tm, kd), lambda i: (i, 0)), pl.BlockSpec((kd, d), lambda i: (0, 0))],
        out_specs=pl.BlockSpec((tm, d), lambda i: (i, 0)),
        out_shape=jax.ShapeDtypeStruct((n, d), F32),
        compiler_params=_cp(("parallel",)),
        name="da_out",
    )(x, w)


def _da_mixer(x, mods, g1, prm, layer_idx, geo):
    qkv_w, q_g, k_g, lam_vec, sub_g, out_w = prm
    lam_init = 0.8 - 0.6 * math.exp(-0.3 * layer_idx)
    lv = lam_vec.astype(F32)
    lam = jnp.exp(jnp.sum(lv[0] * lv[1])) - jnp.exp(jnp.sum(lv[2] * lv[3])) + lam_init
    q, k, v = _da_qkv(x, mods, g1, qkv_w, q_g, k_g, geo)
    o = _da_attn(q, k, v, lam, sub_g, lam_init, geo)
    return _matmul(o, out_w.astype(BF16))


def kernel(x, c, ctx, c_ctx, mod_w, mod_b, norm1_g, norm2_g, s5_a_re, s5_a_im, s5_log_dt, s5_b_re, s5_b_im, s5_c_re, s5_c_im, s5_d, s5_glu_w, s5_glu_b, ssd_in_w, ssd_conv_w, ssd_conv_b, ssd_dt_bias, ssd_a_log, ssd_d, ssd_norm_g, ssd_out_w, da_qkv_w, da_q_g, da_k_g, da_lam, da_sub_g, da_out_w, moe_router_w, moe_router_b, moe_gu_w, moe_gu_b, moe_dn_w, moe_dn_b):
    bsz, n_lat, d = x.shape
    n_ctx = ctx.shape[1]
    depth = mod_w.shape[0]
    geo = Geo(bsz, n_ctx, n_lat)
    xs = jnp.concatenate([ctx, x], axis=1).reshape(geo.rows, d)
    mods_all = _mods(c, c_ctx, mod_w, mod_b)
    n_mixers = 3
    for i in range(depth):
        last = i == depth - 1
        kind, j = i % n_mixers, i // n_mixers
        mods = mods_all[i]
        if kind == 0:
            prm = (s5_a_re[j], s5_a_im[j], s5_log_dt[j], s5_b_re[j], s5_b_im[j], s5_c_re[j], s5_c_im[j],
                   s5_d[j], s5_glu_w[j], s5_glu_b[j])
            y = _s5_mixer(xs, mods, norm1_g[i], prm, geo)
        elif kind == 1:
            prm = (ssd_in_w[j], ssd_conv_w[j], ssd_conv_b[j], ssd_dt_bias[j], ssd_a_log[j], ssd_d[j],
                   ssd_norm_g[j], ssd_out_w[j])
            y = _ssd_mixer(xs, mods, norm1_g[i], prm, geo)
        else:
            prm = (da_qkv_w[j], da_q_g[j], da_k_g[j], da_lam[j], da_sub_g[j], da_out_w[j])
            y = _da_mixer(xs, mods, norm1_g[i], prm, i, geo)
        xs = _moe_layer(xs, y, mods, norm2_g[i], moe_router_w[i], moe_router_b[i], moe_gu_w[i], moe_gu_b[i],
                        moe_dn_w[i], moe_dn_b[i], geo, last)
    return xs.reshape(bsz, n_lat, d)
```

```python
import functools
import math
from typing import NamedTuple

import jax
import jax.numpy as jnp
from jax import lax
from jax.experimental import pallas as pl
from jax.experimental.pallas import tpu as pltpu

F32 = jnp.float32
BF16 = jnp.bfloat16
EPS = 1e-6
HP = lax.Precision.HIGHEST

LANES = 128
VMEM_LIMIT = 56 * 1024 * 1024

N_EXPERTS = 32
TOP_K = 4
SWIGLU_ALPHA = 1.702
SWIGLU_LIMIT = 7.0
MOE_BLK = 512
MOD_TILE = 256
ROW_TILE = 512


class Geo(NamedTuple):
    n_batch: int
    n_ctx: int
    n_lat: int

    @property
    def rows_per_batch(self):
        return self.n_ctx + self.n_lat

    @property
    def rows(self):
        return self.n_batch * self.rows_per_batch

    def tiles(self, tm):
        return self.rows_per_batch // tm, self.n_ctx // tm, self.n_lat // tm


def _cp(sem, vmem=VMEM_LIMIT):
    return pltpu.CompilerParams(dimension_semantics=sem, vmem_limit_bytes=vmem)


def _tile_maps(geo, tm, lat_only):
    tpb, ct, lt = geo.tiles(tm)
    if lat_only:
        src = lambda i: (i // lt) * tpb + ct + i % lt
        mod = lambda i: i // lt
        return geo.n_batch * lt, src, mod
    mod = lambda i: jnp.where(i % tpb < ct, geo.n_batch, i // tpb)
    return geo.n_batch * tpb, (lambda i: i), mod


def _rms(x):
    return x * lax.rsqrt(jnp.mean(x * x, axis=-1, keepdims=True) + EPS)


def _mods_kernel(s_ref, w_ref, b_ref, o_ref):
    s = s_ref[...]
    s = s * jax.nn.sigmoid(s)
    o_ref[0] = jnp.dot(s, w_ref[0], preferred_element_type=F32, precision=HP) + b_ref[0]


def _mods(c, c_ctx, mod_w, mod_b):
    depth, d, d6 = mod_w.shape
    bsz = c.shape[0]
    rows = -(-(bsz + 1) // 8) * 8
    s = jnp.zeros((rows, d), F32).at[:bsz].set(c).at[bsz].set(c_ctx)
    nj = d6 // d
    out = pl.pallas_call(
        _mods_kernel,
        grid=(depth, nj),
        in_specs=[
            pl.BlockSpec((rows, d), lambda l, j: (0, 0)),
            pl.BlockSpec((1, d, d), lambda l, j: (l, 0, j)),
            pl.BlockSpec((1, 1, d), lambda l, j: (l, 0, j)),
        ],
        out_specs=pl.BlockSpec((1, rows, d), lambda l, j: (l, 0, j)),
        out_shape=jax.ShapeDtypeStruct((depth, rows, d6), F32),
        compiler_params=_cp(("arbitrary", "arbitrary")),
        name="mods",
    )(s, mod_w, mod_b.reshape(depth, 1, d6))
    out = out[:, :bsz + 1].reshape(depth, bsz + 1, nj, d)
    return jnp.pad(out, ((0, 0), (0, 0), (0, 8 - nj), (0, 0)))


def _to_row_tiles(ref, val):
    rows, d = val.shape
    sub = d // LANES
    for s in range(sub):
        ref[pl.ds(s, rows, stride=sub), :] = val[:, s * LANES:(s + 1) * LANES]


def _from_row_tiles(ref, rows, sub, dtype=None):
    parts = [ref[pl.ds(s, rows, stride=sub), :] for s in range(sub)]
    if dtype is not None:
        parts = [p.astype(dtype) for p in parts]
    return jnp.concatenate(parts, axis=1)


def _route_kernel(x_ref, y_ref, mod_ref, g_ref, rwh_ref, rwl_ref, rb_ref, tri_ref,
                  xn_ref, h_ref, route_ref, cnt_ref, base_ref):
    i = pl.program_id(0)

    @pl.when(i == 0)
    def _():
        base_ref[...] = jnp.zeros_like(base_ref)

    m = mod_ref[0]
    x = x_ref[...] + m[2:3, :] * y_ref[...]
    xn_ref[...] = x
    h = _rms(x) * g_ref[...] * (1.0 + m[4:5, :]) + m[3:4, :]
    _to_row_tiles(h_ref, h)
    dot = functools.partial(jnp.dot, preferred_element_type=F32)
    h_hi = h.astype(BF16)
    h_lo = (h - h_hi.astype(F32)).astype(BF16)
    logits = dot(h_hi, rwh_ref[...]) + dot(h_lo, rwh_ref[...]) + dot(h_hi, rwl_ref[...]) + rb_ref[...]
    tm = logits.shape[0]
    lane = lax.broadcasted_iota(jnp.int32, (tm, LANES), 1).astype(F32)
    neg = jnp.float32(-jnp.inf)
    work = jnp.where(lane < N_EXPERTS, logits, neg)
    vals, ohs, idxs = [], [], []
    for _ in range(TOP_K):
        mk = jnp.max(work, axis=-1, keepdims=True)
        ik = jnp.min(jnp.where(work == mk, lane, float(LANES)), axis=-1, keepdims=True)
        oh = lane == ik
        work = jnp.where(oh, neg, work)
        vals.append(mk)
        ohs.append(oh)
        idxs.append(ik)
    es = [jnp.exp(v - vals[0]) for v in vals]
    den = es[0] + es[1] + es[2] + es[3]
    onehot = jnp.zeros((tm, LANES), F32)
    for oh in ohs:
        onehot = onehot + jnp.where(oh, 1.0, 0.0)
    prefix = jnp.dot(tri_ref[...], onehot.astype(BF16), preferred_element_type=F32)
    pos = prefix + base_ref[0:1, :]
    out = jnp.zeros((tm, LANES), F32)
    for k in range(TOP_K):
        rank = jnp.sum(jnp.where(ohs[k], pos, 0.0), axis=-1, keepdims=True)
        out = out + jnp.where(lane == k, idxs[k], 0.0)
        out = out + jnp.where(lane == TOP_K + k, rank, 0.0)
        out = out + jnp.where(lane == 2 * TOP_K + k, es[k] / den, 0.0)
    route_ref[...] = out
    newbase = base_ref[0:1, :] + jnp.sum(onehot, axis=0, keepdims=True)
    base_ref[...] = jnp.broadcast_to(newbase, base_ref.shape)
    cnt_ref[...] = jnp.broadcast_to(newbase, cnt_ref.shape)


def _route(x, y, mods, g2, rw, rb, geo, lat_only, tm=MOD_TILE):
    d = x.shape[1]
    sub = d // LANES
    nt, src, mod = _tile_maps(geo, tm, lat_only)
    n = nt * tm
    rw_p = jnp.zeros((d, LANES), F32).at[:, :N_EXPERTS].set(rw)
    rw_hi = rw_p.astype(BF16)
    rw_lo = (rw_p - rw_hi.astype(F32)).astype(BF16)
    rb_p = jnp.zeros((1, LANES), F32).at[0, :N_EXPERTS].set(rb)
    tri = (jnp.arange(tm)[:, None] > jnp.arange(tm)[None, :]).astype(BF16)
    const = lambda i: (0, 0)
    return pl.pallas_call(
        _route_kernel,
        grid=(nt,),
        in_specs=[
            pl.BlockSpec((tm, d), lambda i: (src(i), 0)),
            pl.BlockSpec((tm, d), lambda i: (src(i), 0)),
            pl.BlockSpec((1, 8, d), lambda i: (mod(i), 0, 0)),
            pl.BlockSpec((1, d), const),
            pl.BlockSpec((d, LANES), const),
            pl.BlockSpec((d, LANES), const),
            pl.BlockSpec((1, LANES), const),
            pl.BlockSpec((tm, tm), const),
        ],
        out_specs=[
            pl.BlockSpec((tm, d), lambda i: (i, 0)),
            pl.BlockSpec((tm * sub, LANES), lambda i: (i, 0)),
            pl.BlockSpec((tm, LANES), lambda i: (i, 0)),
            pl.BlockSpec((8, LANES), const),
        ],
        out_shape=[
            jax.ShapeDtypeStruct((n, d), F32),
            jax.ShapeDtypeStruct((n * sub, LANES), F32),
            jax.ShapeDtypeStruct((n, LANES), F32),
            jax.ShapeDtypeStruct((8, LANES), F32),
        ],
        scratch_shapes=[pltpu.VMEM((8, LANES), F32)],
        compiler_params=_cp(("arbitrary",)),
        name="route",
    )(x, y, mods, g2.reshape(1, d), rw_hi, rw_lo, rb_p, tri)


def _dispatch_kernel(last_ref, dest_ref, h_ref, xs_ref, zero_ref, sem, zsem, *, sub, blk):
    i = pl.program_id(0)
    tm = h_ref.shape[0] // sub

    @pl.when(i == 0)
    def _():
        zero_ref[...] = jnp.zeros_like(zero_ref)

        def zcopy(e):
            return pltpu.make_async_copy(zero_ref, xs_ref.at[pl.ds(last_ref[e] * (blk * sub), blk * sub), :], zsem)

        def zstart(e, c):
            @pl.when(last_ref[e] >= 0)
            def _():
                zcopy(e).start()
            return c

        def zwait(e, c):
            @pl.when(last_ref[e] >= 0)
            def _():
                zcopy(e).wait()
            return c

        lax.fori_loop(0, N_EXPERTS, zstart, 0)
        lax.fori_loop(0, N_EXPERTS, zwait, 0)

    tok_per_row = LANES // TOP_K

    def start(row, c):
        for ln in range(LANES):
            src0 = pl.multiple_of(row * (tok_per_row * sub) + (ln // TOP_K) * sub, sub)
            dst0 = pl.multiple_of(dest_ref[row, ln] * sub, sub)
            pltpu.make_async_copy(h_ref.at[pl.ds(src0, sub), :], xs_ref.at[pl.ds(dst0, sub), :], sem).start(
                priority=ln % 2)
        return c

    lax.fori_loop(0, tm // tok_per_row, start, 0)
    for k in range(TOP_K):
        pltpu.make_async_copy(h_ref, xs_ref.at[pl.ds(0, tm * sub), :], sem).wait()


def _dispatch(h, dest2d, last_blk, n_slots, sub, tm=ROW_TILE):
    n = h.shape[0] // sub
    nt = n // tm
    db = tm * TOP_K // LANES
    grid_spec = pltpu.PrefetchScalarGridSpec(
        num_scalar_prefetch=1,
        grid=(nt,),
        in_specs=[
            pl.BlockSpec((db, LANES), lambda i, lb: (i, 0), memory_space=pltpu.SMEM),
            pl.BlockSpec((tm * sub, LANES), lambda i, lb: (i, 0)),
        ],
        out_specs=pl.BlockSpec(memory_space=pl.ANY),
        scratch_shapes=[pltpu.VMEM((MOE_BLK * sub, LANES), F32), pltpu.SemaphoreType.DMA(()),
                        pltpu.SemaphoreType.DMA(())],
    )
    return pl.pallas_call(
        functools.partial(_dispatch_kernel, sub=sub, blk=MOE_BLK),
        grid_spec=grid_spec,
        out_shape=jax.ShapeDtypeStruct((n_slots * sub, LANES), F32),
        compiler_params=_cp(("arbitrary",)),
        name="dispatch",
    )(last_blk, dest2d, h)


def _expert_kernel(be_ref, na_ref, xs_ref, guw_ref, gub_ref, dnw_ref, dnb_ref, ys_ref, guw_bf, dnw_bf):
    j = pl.program_id(0)
    active = j < na_ref[0]

    @pl.when(jnp.logical_and(active, jnp.logical_or(j == 0, be_ref[j] != be_ref[jnp.maximum(j - 1, 0)])))
    def _():
        guw_bf[...] = guw_ref[0].astype(BF16)
        dnw_bf[...] = dnw_ref[0].astype(BF16)

    @pl.when(active)
    def _():
        f, d = dnw_ref.shape[1], dnw_ref.shape[2]
        sub = d // LANES
        x = _from_row_tiles(xs_ref, xs_ref.shape[0] // sub, sub, BF16)
        gu = jnp.dot(x, guw_bf[...], preferred_element_type=F32) + gub_ref[0]
        gate = jnp.minimum(gu[:, :f], SWIGLU_LIMIT)
        up = jnp.clip(gu[:, f:], -SWIGLU_LIMIT, SWIGLU_LIMIT)
        act = (up + 1.0) * gate * jax.nn.sigmoid(SWIGLU_ALPHA * gate)
        y = jnp.dot(act.astype(BF16), dnw_bf[...], preferred_element_type=F32) + dnb_ref[0]
        _to_row_tiles(ys_ref, y)


def _experts(xs, block_e, n_active, guw, gub, dnw, dnb, blk=MOE_BLK):
    e, d, f2 = guw.shape
    f = f2 // 2
    sub = d // LANES
    nb = xs.shape[0] // (blk * sub)
    act_blk = lambda j, be, na: (jnp.minimum(j, na[0] - 1), 0)
    wsel = lambda j, be, na: (be[j], 0, 0)
    grid_spec = pltpu.PrefetchScalarGridSpec(
        num_scalar_prefetch=2,
        grid=(nb,),
        in_specs=[
            pl.BlockSpec((blk * sub, LANES), act_blk),
            pl.BlockSpec((1, d, f2), wsel),
            pl.BlockSpec((1, 1, f2), wsel),
            pl.BlockSpec((1, f, d), wsel),
            pl.BlockSpec((1, 1, d), wsel),
        ],
        out_specs=pl.BlockSpec((blk * sub, LANES), act_blk),
        scratch_shapes=[pltpu.VMEM((d, f2), BF16), pltpu.VMEM((f, d), BF16)],
    )
    return pl.pallas_call(
        _expert_kernel,
        grid_spec=grid_spec,
        out_shape=jax.ShapeDtypeStruct(xs.shape, F32),
        compiler_params=_cp(("arbitrary",)),
        name="experts",
    )(block_e, n_active, xs, guw, gub.reshape(e, 1, f2), dnw, dnb.reshape(e, 1, d))


def _combine_kernel(dest_ref, next_ref, x_ref, route_ref, mod_ref, ys_ref, o_ref, buf, sems):
    i = pl.program_id(0)
    nt = pl.num_programs(0)
    tm, d = x_ref.shape
    sub = d // LANES
    tok_per_row = LANES // TOP_K

    def gather(idx_ref, slot):
        def start(row, c):
            for ln in range(LANES):
                src0 = pl.multiple_of(idx_ref[row, ln] * sub, sub)
                dst0 = pl.multiple_of(row * (tok_per_row * sub) + (ln // TOP_K) * sub, sub)
                pltpu.make_async_copy(ys_ref.at[pl.ds(src0, sub), :], buf.at[slot, ln % TOP_K, pl.ds(dst0, sub), :],
                                      sems.at[slot]).start(priority=ln % 2)
            return c

        lax.fori_loop(0, tm // tok_per_row, start, 0)

    @pl.when(i == 0)
    def _():
        gather(dest_ref, 0)

    @pl.when(i + 1 < nt)
    def _():
        gather(next_ref, (i + 1) % 2)

    slot = i % 2
    for k in range(TOP_K):
        pltpu.make_async_copy(ys_ref.at[pl.ds(0, tm * sub), :], buf.at[slot, k], sems.at[slot]).wait()
    route = route_ref[...]
    gates = [route[:, 2 * TOP_K + k:2 * TOP_K + k + 1] for k in range(TOP_K)]
    scale = mod_ref[0][5:6, :]
    for s in range(sub):
        cols = slice(s * LANES, (s + 1) * LANES)
        f = gates[0] * buf[slot, 0, pl.ds(s, tm, stride=sub), :]
        for k in range(1, TOP_K):
            f = f + gates[k] * buf[slot, k, pl.ds(s, tm, stride=sub), :]
        o_ref[:, cols] = x_ref[:, cols] + scale[:, cols] * f


def _combine(x, route, mods, ys, dest2d, geo, lat_only, tm=MOD_TILE):
    n, d = x.shape
    sub = d // LANES
    nt, _, mod = _tile_maps(geo, tm, lat_only)
    db = tm * TOP_K // LANES
    return pl.pallas_call(
        _combine_kernel,
        grid=(nt,),
        in_specs=[
            pl.BlockSpec((db, LANES), lambda i: (i, 0), memory_space=pltpu.SMEM),
            pl.BlockSpec((db, LANES), lambda i: (jnp.minimum(i + 1, nt - 1), 0), memory_space=pltpu.SMEM),
            pl.BlockSpec((tm, d), lambda i: (i, 0)),
            pl.BlockSpec((tm, LANES), lambda i: (i, 0)),
            pl.BlockSpec((1, 8, d), lambda i: (mod(i), 0, 0)),
            pl.BlockSpec(memory_space=pl.ANY),
        ],
        out_specs=pl.BlockSpec((tm, d), lambda i: (i, 0)),
        out_shape=jax.ShapeDtypeStruct((n, d), F32),
        scratch_shapes=[pltpu.VMEM((2, TOP_K, tm * sub, LANES), F32), pltpu.SemaphoreType.DMA((2,))],
        compiler_params=_cp(("arbitrary",)),
        name="combine",
    )(dest2d, dest2d, x, route, mods, ys)


def _moe_layer(x, y, mods, g2, rw, rb, guw, gub, dnw, dnb, geo, lat_only):
    xn, h, route, cnt = _route(x, y, mods, g2, rw, rb, geo, lat_only)
    n, d = xn.shape
    sub = d // LANES
    counts = cnt[0, :N_EXPERTS].astype(jnp.int32)
    nblk = (counts + MOE_BLK - 1) // MOE_BLK
    blk_end = jnp.cumsum(nblk)
    pad_start = (blk_end - nblk) * MOE_BLK
    n_blocks = -(-(n * TOP_K) // MOE_BLK) + N_EXPERTS
    block_e = jnp.minimum(jnp.sum(blk_end[None, :] <= jnp.arange(n_blocks)[:, None], axis=1), N_EXPERTS - 1)
    n_active = blk_end[-1:].astype(jnp.int32)
    last_blk = jnp.where(nblk > 0, blk_end - 1, -1).astype(jnp.int32)
    eid = route[:, :TOP_K].astype(jnp.int32)
    rank = route[:, TOP_K:2 * TOP_K].astype(jnp.int32)
    dest = (pad_start[eid] + rank).astype(jnp.int32).reshape(n * TOP_K // LANES, LANES)
    xs = _dispatch(h, dest, last_blk, n_blocks * MOE_BLK, sub)
    ys = _experts(xs, block_e.astype(jnp.int32), n_active, guw, gub, dnw, dnb)
    return _combine(xn, route, mods, ys, dest, geo, lat_only)


def _prenorm_kernel(x_ref, mod_ref, g_ref, h_ref):
    m = mod_ref[0]
    h = _rms(x_ref[...]) * g_ref[...] * (1.0 + m[1:2, :]) + m[0:1, :]
    h_ref[...] = h.astype(h_ref.dtype)


def _prenorm(x, mods, g1, geo, dtype, tm=MOD_TILE):
    n, d = x.shape
    nt, _, mod = _tile_maps(geo, tm, False)
    return pl.pallas_call(
        _prenorm_kernel,
        grid=(nt,),
        in_specs=[
            pl.BlockSpec((tm, d), lambda i: (i, 0)),
            pl.BlockSpec((1, 8, d), lambda i: (mod(i), 0, 0)),
            pl.BlockSpec((1, d), lambda i: (0, 0)),
        ],
        out_specs=pl.BlockSpec((tm, d), lambda i: (i, 0)),
        out_shape=jax.ShapeDtypeStruct((n, d), dtype),
        compiler_params=_cp(("parallel",)),
        name="prenorm",
    )(x, mods, g1.reshape(1, d))


def _prenorm_join_kernel(c_ref, x_ref, mod_ref, g_ref, h_ref, xs_ref, *, tiles_per_batch, ctx_tiles):
    is_ctx = pl.program_id(0) % tiles_per_batch < ctx_tiles
    x = jnp.where(is_ctx, c_ref[0], x_ref[0])
    xs_ref[...] = x
    m = mod_ref[0]
    h_ref[...] = (_rms(x) * g_ref[...] * (1.0 + m[1:2, :]) + m[0:1, :]).astype(h_ref.dtype)


def _prenorm_join(ctx, x, mods, g1, geo, dtype, tm=MOD_TILE):
    d = x.shape[-1]
    nt, _, mod = _tile_maps(geo, tm, False)
    tpb, ct, _ = geo.tiles(tm)
    row = lambda i: (i, 0)
    return pl.pallas_call(
        functools.partial(_prenorm_join_kernel, tiles_per_batch=tpb, ctx_tiles=ct),
        grid=(nt,),
        in_specs=[
            pl.BlockSpec((1, tm, d), lambda i: (i // tpb, jnp.minimum(i % tpb, ct - 1), 0)),
            pl.BlockSpec((1, tm, d), lambda i: (i // tpb, jnp.maximum(i % tpb - ct, 0), 0)),
            pl.BlockSpec((1, 8, d), lambda i: (mod(i), 0, 0)),
            pl.BlockSpec((1, d), lambda i: (0, 0)),
        ],
        out_specs=[pl.BlockSpec((tm, d), row), pl.BlockSpec((tm, d), row)],
        out_shape=[jax.ShapeDtypeStruct((geo.rows, d), dtype), jax.ShapeDtypeStruct((geo.rows, d), F32)],
        compiler_params=_cp(("parallel",)),
        name="prenorm_join",
    )(ctx, x, mods, g1.reshape(1, d))


S5_CHUNK = 16
S5_GB = 8


def _s5_matrices(a_re, a_im, log_dt, b_re, b_im, c_re, c_im):
    L = S5_CHUNK
    a_re = a_re.astype(F32)
    a_im = a_im.astype(F32)
    dt = jnp.exp(log_dt.astype(F32))[..., None]
    mag = jnp.exp(dt * a_re)
    ang = dt * a_im
    ab_re, ab_im = mag * jnp.cos(ang), mag * jnp.sin(ang)
    den = a_re * a_re + a_im * a_im
    num_re = ab_re - 1.0
    co_re = (num_re * a_re + ab_im * a_im) / den
    co_im = (ab_im * a_re - num_re * a_im) / den
    bb_re = co_re[..., None] * b_re - co_im[..., None] * b_im
    bb_im = co_re[..., None] * b_im + co_im[..., None] * b_re
    taus = jnp.arange(L + 1, dtype=F32)[:, None, None, None]
    pmag = jnp.exp(taus * dt * a_re)
    pang = taus * dt * a_im
    pw_re, pw_im = pmag * jnp.cos(pang), pmag * jnp.sin(pang)
    cp_re = c_re[None] * pw_re[:, :, :, None, :] - c_im[None] * pw_im[:, :, :, None, :]
    cp_im = c_re[None] * pw_im[:, :, :, None, :] + c_im[None] * pw_re[:, :, :, None, :]
    kk = (jnp.einsum('tdgjp,dgpi->tdgji', cp_re, bb_re, precision=HP)
          - jnp.einsum('tdgjp,dgpi->tdgji', cp_im, bb_im, precision=HP))
    t_idx = jnp.arange(L)
    lag = t_idx[None, :] - t_idx[:, None]

    def toeplitz(kd, lagm):
        kt = kd[jnp.clip(lagm, 0, L)]
        kt = jnp.where((lagm >= 0)[:, :, None, None, None], kt, 0.0)
        return kt.transpose(2, 0, 4, 1, 3)

    g = a_re.shape[1]
    p = a_re.shape[2]
    j = b_re.shape[3]
    m = (toeplitz(kk[:, 0], lag) + toeplitz(kk[:, 1], -lag)).reshape(g, L * j, L * j)

    def inject(d, steps):
        pr, pi = pw_re[steps, d], pw_im[steps, d]
        re = pr[:, :, :, None] * bb_re[d][None] - pi[:, :, :, None] * bb_im[d][None]
        im = pr[:, :, :, None] * bb_im[d][None] + pi[:, :, :, None] * bb_re[d][None]
        f = lambda z: z.transpose(1, 0, 3, 2).reshape(g, L * j, p)
        return f(re), f(im)

    inj = inject(0, L - 1 - t_idx) + inject(1, t_idx)

    def readout(d, steps):
        re = cp_re[steps, d]
        im = cp_im[steps, d]
        f = lambda z: z.transpose(1, 3, 0, 2).reshape(g, p, L * j)
        return f(re), f(-im)

    rd = readout(0, t_idx + 1) + readout(1, L - t_idx)

    gb = S5_GB
    nblk = g // gb

    def spread(z, width):
        z = z.astype(BF16)
        lanes = jnp.arange(gb * width)
        mask = (jnp.arange(gb)[:, None] == lanes[None, :] // width).astype(BF16)
        mask = mask.reshape((1, gb) + (1,) * (z.ndim - 3) + (gb * width,))
        rep = (jnp.arange(width)[:, None] == lanes[None, :] % width).astype(BF16)
        tiled = lax.dot_general(z, rep, (((z.ndim - 1,), (0,)), ((), ())), preferred_element_type=BF16)
        return tiled * mask

    m_blk = spread(m.reshape(nblk, gb, L, j, L, j), j).transpose(0, 2, 1, 3, 4, 5).reshape(nblk, L * gb * j, L * gb * j)

    def inj_blk(re, im):
        f = lambda z: spread(z.reshape(nblk, gb, L, j, p), p).transpose(0, 2, 1, 3, 4).reshape(nblk, L * gb * j, gb * p)
        return jnp.concatenate([f(re), f(im)], axis=2)

    def rd_blk(re, im):
        f = lambda z: spread(z.reshape(nblk, gb, p, L, j), j).reshape(nblk, gb * p, L * gb * j)
        return jnp.concatenate([f(re), f(im)], axis=1)

    def adv_blk(d):
        return jnp.concatenate([pw_re[L, d].reshape(nblk, 1, gb * p), pw_im[L, d].reshape(nblk, 1, gb * p)], axis=2)

    adv = jnp.pad(jnp.concatenate([adv_blk(0), adv_blk(1)], axis=1), ((0, 0), (0, 6), (0, 0)))
    return (m_blk, inj_blk(inj[0], inj[1]), inj_blk(inj[2], inj[3]), rd_blk(rd[0], rd[1]), rd_blk(rd[2], rd[3]), adv)


def _s5_chunk_vectors(x_ref):
    nb, rows, _ = x_ref.shape
    c = rows // S5_CHUNK
    parts = [x_ref[:, pl.ds(l, c, stride=S5_CHUNK), :].reshape(nb * c, LANES).astype(BF16) for l in range(S5_CHUNK)]
    return jnp.concatenate(parts, axis=1)


def _s5_put_states(sloc_ref, val):
    for k in range(sloc_ref.shape[0]):
        sloc_ref[k] = val[:, k * LANES:(k + 1) * LANES]


def _s5_get_states(sloc_ref):
    return jnp.concatenate([sloc_ref[k] for k in range(sloc_ref.shape[0])], axis=1)


def _s5_chunk_scan(sloc_ref, carry_ref, adv, nb, c, reverse):
    nk = sloc_ref.shape[0]
    half = nk // 2
    a = [jnp.broadcast_to(adv[:, k * LANES:(k + 1) * LANES], (nb, LANES)) for k in range(nk)]

    def step(i, st):
        ci = (c - 1 - i) if reverse else i
        rows = pl.ds(ci, nb, stride=c)
        new_re, new_im = [], []
        for k in range(half):
            l_re, l_im = sloc_ref[k, rows, :], sloc_ref[half + k, rows, :]
            s_re, s_im = st[k], st[half + k]
            sloc_ref[k, rows, :] = s_re
            sloc_ref[half + k, rows, :] = s_im
            new_re.append(a[k] * s_re - a[half + k] * s_im + l_re)
            new_im.append(a[k] * s_im + a[half + k] * s_re + l_im)
        return tuple(new_re + new_im)

    st = lax.fori_loop(0, c, step, tuple(carry_ref[k] for k in range(nk)))
    for k in range(nk):
        carry_ref[k] = st[k]


def _s5_bwd_kernel(x_ref, inj_ref, adv_ref, sn_ref, sloc_ref, carry_ref):
    @pl.when(pl.program_id(1) == 0)
    def _():
        carry_ref[...] = jnp.zeros_like(carry_ref)

    nb = x_ref.shape[0]
    c = x_ref.shape[1] // S5_CHUNK
    z = _s5_chunk_vectors(x_ref)
    _s5_put_states(sloc_ref, jnp.dot(z, inj_ref[0], preferred_element_type=F32))
    _s5_chunk_scan(sloc_ref, carry_ref, adv_ref[0][1:2, :], nb, c, True)
    sn_ref[0, 0] = _s5_get_states(sloc_ref).astype(sn_ref.dtype)


def _s5_fwd_kernel(x_ref, sn_ref, m_ref, inj_ref, rdf_ref, rdb_ref, adv_ref, y_ref, sloc_ref, carry_ref):
    @pl.when(pl.program_id(1) == 0)
    def _():
        carry_ref[...] = jnp.zeros_like(carry_ref)

    nb = x_ref.shape[0]
    c = x_ref.shape[1] // S5_CHUNK
    dot = functools.partial(jnp.dot, preferred_element_type=F32)
    z = _s5_chunk_vectors(x_ref)
    _s5_put_states(sloc_ref, dot(z, inj_ref[0]))
    _s5_chunk_scan(sloc_ref, carry_ref, adv_ref[0][0:1, :], nb, c, False)
    y = dot(z, m_ref[0]) + dot(_s5_get_states(sloc_ref).astype(BF16), rdf_ref[0]) + dot(sn_ref[0, 0], rdb_ref[0])
    for t in range(S5_CHUNK):
        y_ref[:, pl.ds(t, c, stride=S5_CHUNK), :] = y[:, t * LANES:(t + 1) * LANES].reshape(nb, c, LANES)


def _s5_core(h, mats, geo, tile=MOD_TILE):
    m, inj_f, inj_b, rd_f, rd_b, adv = mats
    nblk, kdim, sdim = inj_f.shape
    nb, rpb = geo.n_batch, geo.rows_per_batch
    d = h.shape[1]
    tpb, ct, _ = geo.tiles(tile)
    c = tile // S5_CHUNK
    h3 = h.reshape(nb, rpb, d)
    bwd_tile = lambda s: jnp.where(s < ct, ct - 1 - s, tpb - 1 - (s - ct))
    once = pl.Buffered(1)
    wspec = lambda shape: pl.BlockSpec((1,) + shape, lambda g, s: (g, 0, 0), pipeline_mode=once)
    scratch = [pltpu.VMEM((sdim // LANES, nb * c, LANES), F32), pltpu.VMEM((sdim // LANES, nb, LANES), F32)]
    sn = pl.pallas_call(
        _s5_bwd_kernel,
        grid=(nblk, tpb),
        in_specs=[
            pl.BlockSpec((nb, tile, LANES), lambda g, s: (0, bwd_tile(s), g)),
            wspec((kdim, sdim)),
            wspec((8, sdim)),
        ],
        out_specs=pl.BlockSpec((1, 1, nb * c, sdim), lambda g, s: (g, bwd_tile(s), 0, 0)),
        out_shape=jax.ShapeDtypeStruct((nblk, tpb, nb * c, sdim), BF16),
        scratch_shapes=scratch,
        compiler_params=_cp(("parallel", "arbitrary")),
        name="s5_bwd",
    )(h3, inj_b, adv)
    y = pl.pallas_call(
        _s5_fwd_kernel,
        grid=(nblk, tpb),
        in_specs=[
            pl.BlockSpec((nb, tile, LANES), lambda g, s: (0, s, g)),
            pl.BlockSpec((1, 1, nb * c, sdim), lambda g, s: (g, s, 0, 0)),
            wspec((kdim, kdim)),
            wspec((kdim, sdim)),
            wspec((sdim, kdim)),
            wspec((sdim, kdim)),
            wspec((8, sdim)),
        ],
        out_specs=pl.BlockSpec((nb, tile, LANES), lambda g, s: (0, s, g)),
        out_shape=jax.ShapeDtypeStruct((nb, rpb, d), F32),
        scratch_shapes=scratch,
        compiler_params=_cp(("parallel", "arbitrary")),
        name="s5_fwd",
    )(h3, sn, m, inj_f, rd_f, rd_b, adv)
    return y.reshape(nb * rpb, d)


def _gelu_tanh(x):
    return 0.5 * x * (1.0 + jnp.tanh(math.sqrt(2.0 / math.pi) * (x + 0.044715 * (x * x * x))))


def _s5_glu_kernel(h_ref, y_ref, d_ref, w_ref, b_ref, o_ref):
    y = d_ref[...] * h_ref[...].astype(F32) + y_ref[...].astype(F32)
    gl = _gelu_tanh(y).astype(BF16)
    z = jnp.dot(gl, w_ref[...], preferred_element_type=F32) + b_ref[...]
    dd = o_ref.shape[1]
    o_ref[...] = z[:, :dd] * jax.nn.sigmoid(z[:, dd:])


def _s5_glu(h, y, d_skip, glu_w, glu_b, tm=ROW_TILE):
    n, d = h.shape
    return pl.pallas_call(
        _s5_glu_kernel,
        grid=(n // tm,),
        in_specs=[
            pl.BlockSpec((tm, d), lambda i: (i, 0)),
            pl.BlockSpec((tm, d), lambda i: (i, 0)),
            pl.BlockSpec((1, d), lambda i: (0, 0)),
            pl.BlockSpec((d, 2 * d), lambda i: (0, 0)),
            pl.BlockSpec((1, 2 * d), lambda i: (0, 0)),
        ],
        out_specs=pl.BlockSpec((tm, d), lambda i: (i, 0)),
        out_shape=jax.ShapeDtypeStruct((n, d), F32),
        compiler_params=_cp(("parallel",)),
        name="s5_glu",
    )(h, y, d_skip.reshape(1, d).astype(F32), glu_w.astype(BF16), glu_b.reshape(1, 2 * d).astype(F32))


def _s5_mixer(h, prm, geo):
    a_re, a_im, log_dt, b_re, b_im, c_re, c_im, d_skip, glu_w, glu_b = prm
    mats = _s5_matrices(a_re, a_im, log_dt, b_re, b_im, c_re, c_im)
    y = _s5_core(h, mats, geo)
    return _s5_glu(h, y, d_skip, glu_w, glu_b)


SSD_Q = 256
SSD_HEAD_DIM = 64
SSD_STATE = 128
SSD_CONV = 5
CONV_HALO = 16


def _ssd_inproj_kernel(x_ref, mod_ref, g_ref, wz_ref, wx_ref, wd_ref, z_ref, xbc_ref, dt_ref):
    m = mod_ref[0]
    h = (_rms(x_ref[...]) * g_ref[...] * (1.0 + m[1:2, :]) + m[0:1, :]).astype(BF16)
    z_ref[...] = jnp.dot(h, wz_ref[...], preferred_element_type=F32).astype(z_ref.dtype)
    xbc_ref[...] = jnp.dot(h, wx_ref[...], preferred_element_type=F32).astype(xbc_ref.dtype)
    dt_ref[...] = jnp.dot(h, wd_ref[...], preferred_element_type=F32)


def _ssd_inproj(x, mods, g1, in_w, d_inner, n_bc, geo, tm=MOD_TILE):
    n, d = x.shape
    nt, _, mod = _tile_maps(geo, tm, False)
    wz = in_w[:, :d_inner].astype(BF16)
    wx = in_w[:, d_inner:2 * d_inner + n_bc].astype(BF16)
    wd = in_w[:, 2 * d_inner + n_bc:]
    wd = jnp.pad(wd, ((0, 0), (0, LANES - wd.shape[1]))).astype(BF16)
    cx = d_inner + n_bc
    const = lambda i: (0, 0)
    row = lambda i: (i, 0)
    return pl.pallas_call(
        _ssd_inproj_kernel,
        grid=(nt,),
        in_specs=[
            pl.BlockSpec((tm, d), row),
            pl.BlockSpec((1, 8, d), lambda i: (mod(i), 0, 0)),
            pl.BlockSpec((1, d), const),
            pl.BlockSpec((d, d_inner), const),
            pl.BlockSpec((d, cx), const),
            pl.BlockSpec((d, LANES), const),
        ],
        out_specs=[pl.BlockSpec((tm, d_inner), row), pl.BlockSpec((tm, cx), row), pl.BlockSpec((tm, LANES), row)],
        out_shape=[
            jax.ShapeDtypeStruct((n, d_inner), BF16),
            jax.ShapeDtypeStruct((n, cx), BF16),
            jax.ShapeDtypeStruct((n, LANES), F32),
        ],
        compiler_params=_cp(("parallel",)),
        name="ssd_inproj",
    )(x, mods, g1.reshape(1, d), wz, wx, wd)


def _ssd_conv_kernel(prev_ref, cur_ref, next_ref, w_ref, b_ref, o_ref, *, tiles_per_batch, ctx_tiles):
    j = pl.program_id(0) % tiles_per_batch
    first = jnp.logical_or(j == 0, j == ctx_tiles)
    last = jnp.logical_or(j == ctx_tiles - 1, j == tiles_per_batch - 1)
    half = CONV_HALO // 2
    cur = cur_ref[...].astype(F32)
    tt = cur.shape[0]
    prev = jnp.where(first, 0.0, prev_ref[...].astype(F32)[half:, :])
    nxt = jnp.where(last, 0.0, next_ref[...].astype(F32)[:half, :])
    ext = jnp.concatenate([prev, cur, nxt], axis=0)
    w = w_ref[...]
    pad = (SSD_CONV - 1) // 2
    acc = jnp.broadcast_to(b_ref[...], cur.shape)
    for k in range(SSD_CONV):
        off = half + k - pad
        acc = acc + w[k:k + 1, :] * ext[off:off + tt, :]
    o_ref[...] = (acc * jax.nn.sigmoid(acc)).astype(o_ref.dtype)


def _ssd_conv(xbc, conv_w, conv_b, geo, tt=MOD_TILE, tc=2048):
    n, cx = xbc.shape
    tpb, ct, _ = geo.tiles(tt)
    nt = n // tt
    hb = tt // CONV_HALO
    nhb = n // CONV_HALO
    wpad = jnp.pad(conv_w, ((0, 8 - conv_w.shape[0]), (0, 0))).astype(F32)
    return pl.pallas_call(
        functools.partial(_ssd_conv_kernel, tiles_per_batch=tpb, ctx_tiles=ct),
        grid=(nt, cx // tc),
        in_specs=[
            pl.BlockSpec((CONV_HALO, tc), lambda i, c: (jnp.maximum(i * hb - 1, 0), c)),
            pl.BlockSpec((tt, tc), lambda i, c: (i, c)),
            pl.BlockSpec((CONV_HALO, tc), lambda i, c: (jnp.minimum((i + 1) * hb, nhb - 1), c)),
            pl.BlockSpec((8, tc), lambda i, c: (0, c)),
            pl.BlockSpec((1, tc), lambda i, c: (0, c)),
        ],
        out_specs=pl.BlockSpec((tt, tc), lambda i, c: (i, c)),
        out_shape=jax.ShapeDtypeStruct((n, cx), BF16),
        compiler_params=_cp(("parallel", "parallel")),
        name="ssd_conv",
    )(xbc, xbc, xbc, wpad, conv_b.reshape(1, cx).astype(F32))


def _ssd_dt_kernel(raw_ref, bias_ref, a_ref, dt_ref, cs_ref, cst_ref, *, n_heads):
    q = raw_ref.shape[0]
    dt = jax.nn.softplus(raw_ref[...] + bias_ref[...])
    dta = dt * a_ref[...]
    r = lax.broadcasted_iota(jnp.int32, (q, q), 0)
    c = lax.broadcasted_iota(jnp.int32, (q, q), 1)
    lower = jnp.where(c <= r, 1.0, 0.0)
    upper = jnp.where(c >= r, 1.0, 0.0)
    cs_f = jnp.dot(lower, dta, preferred_element_type=F32, precision=HP)
    cs_b = jnp.dot(upper, dta, preferred_element_type=F32, precision=HP)
    lane = lax.broadcasted_iota(jnp.int32, dt.shape, 1)
    cs = jnp.where(lane < n_heads, cs_f, cs_b)
    dt_ref[...] = dt
    cs_ref[...] = cs
    cst_ref[0] = cs.T


def _ssd_dt(dt_raw, dt_bias, a_log, n_heads):
    n = dt_raw.shape[0]
    q = SSD_Q
    pad = lambda v: jnp.pad(v.reshape(1, -1).astype(F32), ((0, 0), (0, LANES - v.size)))
    return pl.pallas_call(
        functools.partial(_ssd_dt_kernel, n_heads=n_heads),
        grid=(n // q,),
        in_specs=[
            pl.BlockSpec((q, LANES), lambda i: (i, 0)),
            pl.BlockSpec((1, LANES), lambda i: (0, 0)),
            pl.BlockSpec((1, LANES), lambda i: (0, 0)),
        ],
        out_specs=[
            pl.BlockSpec((q, LANES), lambda i: (i, 0)),
            pl.BlockSpec((q, LANES), lambda i: (i, 0)),
            pl.BlockSpec((1, LANES, q), lambda i: (i, 0, 0)),
        ],
        out_shape=[
            jax.ShapeDtypeStruct((n, LANES), F32),
            jax.ShapeDtypeStruct((n, LANES), F32),
            jax.ShapeDtypeStruct((n // q, LANES, q), F32),
        ],
        compiler_params=_cp(("parallel",)),
        name="ssd_dt",
    )(dt_raw, pad(dt_bias), pad(-jnp.exp(a_log.astype(F32))))


def _ssd_scan_kernel(x_ref, b_ref, c_ref, dt_ref, cs_ref, cst_ref, dsk_ref, y_ref, st_ref, *,
                     n_heads, heads_per_group, ctx_chunks, lat_chunks):
    g = pl.program_id(1)
    q = SSD_Q
    hd = SSD_HEAD_DIM
    width = heads_per_group * hd
    dot = functools.partial(jnp.dot, preferred_element_type=F32)
    lane = lax.broadcasted_iota(jnp.int32, (q, LANES), 1)
    blk = lax.broadcasted_iota(jnp.int32, (q, width), 1) // hd
    r_i = lax.broadcasted_iota(jnp.int32, (q, q), 0)
    c_i = lax.broadcasted_iota(jnp.int32, (q, q), 1)

    def spread(cols):
        out = jnp.broadcast_to(cols[0], (q, width))
        for hh in range(1, heads_per_group):
            out = jnp.where(blk == hh, cols[hh], out)
        return out

    def chunk(ci, direction, first_pass):
        r0 = ci * q if isinstance(ci, int) else pl.multiple_of(ci * q, q)
        rows = pl.ds(r0, q)
        xb = x_ref[rows, :].astype(F32)
        bm = b_ref[rows, :]
        cm = c_ref[rows, :]
        dt = dt_ref[rows, :]
        cs = cs_ref[rows, :]
        col0 = direction * n_heads + g * heads_per_group
        dt_cols, cs_cols, cs_rows = [], [], []
        for hh in range(heads_per_group):
            sel = lane == col0 + hh
            dt_cols.append(jnp.sum(jnp.where(sel, dt, 0.0), axis=1, keepdims=True))
            cs_cols.append(jnp.sum(jnp.where(sel, cs, 0.0), axis=1, keepdims=True))
            cs_rows.append(cst_ref[ci, pl.ds(col0 + hh, 1), :])
        xdt = xb * spread(dt_cols)
        csf = spread(cs_cols)
        edge = csf[q - 1:q, :] if direction == 0 else csf[0:1, :]
        scores = lax.dot_general(cm, bm, (((1,), (1,)), ((), ())), preferred_element_type=F32)
        keep = (c_i <= r_i) if direction == 0 else (c_i >= r_i)
        xdt_b = xdt.astype(BF16)
        state = st_ref[...]
        y = dot(cm, state.astype(BF16)) * jnp.exp(csf)
        for hh in range(heads_per_group):
            seg = jnp.where(keep, cs_cols[hh] - cs_rows[hh], -jnp.inf)
            pm = (scores * jnp.exp(seg)).astype(BF16)
            y = y + dot(pm, jnp.where(blk == hh, xdt_b, jnp.zeros_like(xdt_b)))
        w = (xdt * jnp.exp(edge - csf)).astype(BF16)
        bt = bm.astype(F32).T.astype(BF16)
        st_ref[...] = jnp.exp(edge) * state + dot(bt, w)
        if first_pass:
            y_ref[rows, :] = y + dsk_ref[0] * xb
        else:
            y_ref[rows, :] = y_ref[rows, :] + y

    for direction in range(2):
        st_ref[...] = jnp.zeros_like(st_ref)
        for ci in range(ctx_chunks):
            cc = ci if direction == 0 else ctx_chunks - 1 - ci
            chunk(cc, direction, direction == 0)

        def body(i, carry, direction=direction):
            cc = ctx_chunks + (i if direction == 0 else lat_chunks - 1 - i)
            chunk(cc, direction, direction == 0)
            return carry

        lax.fori_loop(0, lat_chunks, body, 0)


def _ssd_scan(xbc, dt, cs, cst, d_skip, geo, *, n_heads, n_groups):
    n = xbc.shape[0]
    rpb = geo.rows_per_batch
    hpg = n_heads // n_groups
    width = hpg * SSD_HEAD_DIM
    d_inner = n_heads * SSD_HEAD_DIM
    xblocks = d_inner // width
    bblocks = d_inner // SSD_STATE
    cpb = rpb // SSD_Q
    dsk = jnp.repeat(d_skip.astype(F32), SSD_HEAD_DIM).reshape(n_groups, 1, width)
    kern = functools.partial(_ssd_scan_kernel, n_heads=n_heads, heads_per_group=hpg,
                             ctx_chunks=geo.n_ctx // SSD_Q, lat_chunks=geo.n_lat // SSD_Q)
    return pl.pallas_call(
        kern,
        grid=(geo.n_batch, n_groups),
        in_specs=[
            pl.BlockSpec((rpb, width), lambda b, g: (b, g)),
            pl.BlockSpec((rpb, SSD_STATE), lambda b, g: (b, bblocks + g)),
            pl.BlockSpec((rpb, SSD_STATE), lambda b, g: (b, bblocks + n_groups + g)),
            pl.BlockSpec((rpb, LANES), lambda b, g: (b, 0)),
            pl.BlockSpec((rpb, LANES), lambda b, g: (b, 0)),
            pl.BlockSpec((cpb, LANES, SSD_Q), lambda b, g: (b, 0, 0)),
            pl.BlockSpec((1, 1, width), lambda b, g: (g, 0, 0)),
        ],
        out_specs=pl.BlockSpec((rpb, width), lambda b, g: (b, g)),
        out_shape=jax.ShapeDtypeStruct((n, d_inner), F32),
        scratch_shapes=[pltpu.VMEM((SSD_STATE, width), F32)],
        compiler_params=_cp(("parallel", "arbitrary")),
        name="ssd_scan",
    )(xbc, xbc, xbc, dt, cs, cst, dsk)


def _ssd_out_kernel(y_ref, z_ref, g_ref, w_ref, o_ref, *, n_groups):
    z = z_ref[...].astype(F32)
    gy = y_ref[...] * (z * jax.nn.sigmoid(z))
    gw = gy.shape[1] // n_groups
    acc = jnp.zeros(o_ref.shape, F32)
    for gi in range(n_groups):
        sl = slice(gi * gw, (gi + 1) * gw)
        ng = (_rms(gy[:, sl]) * g_ref[:, sl]).astype(BF16)
        acc = acc + jnp.dot(ng, w_ref[sl, :], preferred_element_type=F32)
    o_ref[...] = acc


def _ssd_out(y, z, norm_g, out_w, n_groups, tm=ROW_TILE):
    n, di = y.shape
    d = out_w.shape[1]
    return pl.pallas_call(
        functools.partial(_ssd_out_kernel, n_groups=n_groups),
        grid=(n // tm,),
        in_specs=[
            pl.BlockSpec((tm, di), lambda i: (i, 0)),
            pl.BlockSpec((tm, di), lambda i: (i, 0)),
            pl.BlockSpec((1, di), lambda i: (0, 0)),
            pl.BlockSpec((di, d), lambda i: (0, 0)),
        ],
        out_specs=pl.BlockSpec((tm, d), lambda i: (i, 0)),
        out_shape=jax.ShapeDtypeStruct((n, d), F32),
        compiler_params=_cp(("parallel",)),
        name="ssd_out",
    )(y, z, norm_g.reshape(1, di).astype(F32), out_w.astype(BF16))


def _ssd_mixer(x, mods, g1, prm, geo):
    in_w, conv_w, conv_b, dt_bias, a_log, d_skip, norm_g, out_w = prm
    n_heads = a_log.shape[1]
    d_inner = out_w.shape[0]
    n_bc = conv_w.shape[1] - d_inner
    n_groups = n_bc // (2 * SSD_STATE)
    z, xbc_raw, dt_raw = _ssd_inproj(x, mods, g1, in_w, d_inner, n_bc, geo)
    xbc = _ssd_conv(xbc_raw, conv_w, conv_b, geo)
    dt, cs, cst = _ssd_dt(dt_raw, dt_bias, a_log, n_heads)
    y = _ssd_scan(xbc, dt, cs, cst, d_skip, geo, n_heads=n_heads, n_groups=n_groups)
    return _ssd_out(y, z, norm_g, out_w, n_groups)


DA_HEAD_DIM = 64
ROPE_BASE = 10000.0
GRID_W = 64


def _rope_tables(geo, tm):
    hd = DA_HEAD_DIM
    n_freq = hd // 4
    t = jnp.arange(geo.n_lat, dtype=F32)
    row, col = jnp.floor(t / GRID_W), jnp.mod(t, GRID_W)
    inv = ROPE_BASE ** (-jnp.arange(n_freq, dtype=F32) / n_freq)
    lane = jnp.arange(LANES)
    within = lane % hd
    freq = inv[within % n_freq]
    pos = jnp.where((within < hd // 2)[None, :], row[:, None], col[:, None])
    ang = pos * freq[None, :]
    sign = jnp.where((within % (2 * n_freq)) < n_freq, -1.0, 1.0)
    cos = jnp.concatenate([jnp.cos(ang), jnp.ones((tm, LANES), F32)], axis=0)
    sin = jnp.concatenate([jnp.sin(ang) * sign[None, :], jnp.zeros((tm, LANES), F32)], axis=0)
    return cos, sin


def _da_qkv_kernel(x_ref, mod_ref, g_ref, w_ref, cos_ref, sin_ref, qg_ref, kg_ref, seg_ref,
                   q_ref, k_ref, v_ref):
    m = mod_ref[0]
    h = (_rms(x_ref[...]) * g_ref[...] * (1.0 + m[1:2, :]) + m[0:1, :]).astype(BF16)
    width = q_ref.shape[1]
    tm = h.shape[0]
    cos = cos_ref[...]
    sin = sin_ref[...]
    lane = lax.broadcasted_iota(jnp.int32, (tm, LANES), 1)
    n_freq = DA_HEAD_DIM // 4
    first_half = (lane % (2 * n_freq)) < n_freq
    scale = DA_HEAD_DIM ** -0.5 * math.log2(math.e)
    v_ref[...] = jnp.dot(h, w_ref[:, 2 * width:], preferred_element_type=F32).astype(v_ref.dtype)
    for o_ref, off, gain, mult in ((q_ref, 0, qg_ref, scale), (k_ref, width, kg_ref, 1.0)):
        for blk in range(width // LANES):
            cols = slice(off + blk * LANES, off + (blk + 1) * LANES)
            a = jnp.dot(h, w_ref[:, cols], preferred_element_type=F32)
            ms = jnp.dot((a * a).astype(BF16), seg_ref[...], preferred_element_type=F32)
            a = a * lax.rsqrt(ms + EPS) * gain[...]
            partner = jnp.where(first_half, pltpu.roll(a, LANES - n_freq, 1), pltpu.roll(a, n_freq, 1))
            a = a * cos + partner * sin
            o_ref[:, blk * LANES:(blk + 1) * LANES] = (a * mult).astype(o_ref.dtype)


def _da_qkv(x, mods, g1, qkv_w, q_g, k_g, geo, tm=MOD_TILE):
    n, d = x.shape
    width = qkv_w.shape[1] // 3
    nt, _, mod = _tile_maps(geo, tm, False)
    tpb, ct, lt = geo.tiles(tm)
    cos, sin = _rope_tables(geo, tm)
    tab = lambda i: (jnp.where(i % tpb < ct, lt, i % tpb - ct), 0)
    lane = jnp.arange(LANES)
    seg = jnp.where((lane[:, None] // DA_HEAD_DIM) == (lane[None, :] // DA_HEAD_DIM), 1.0 / DA_HEAD_DIM, 0.0)
    tile2 = lambda v: jnp.tile(v.astype(F32), LANES // DA_HEAD_DIM).reshape(1, LANES)
    const = lambda i: (0, 0)
    row = lambda i: (i, 0)
    return pl.pallas_call(
        _da_qkv_kernel,
        grid=(nt,),
        in_specs=[
            pl.BlockSpec((tm, d), row),
            pl.BlockSpec((1, 8, d), lambda i: (mod(i), 0, 0)),
            pl.BlockSpec((1, d), const),
            pl.BlockSpec((d, 3 * width), const),
            pl.BlockSpec((tm, LANES), tab),
            pl.BlockSpec((tm, LANES), tab),
            pl.BlockSpec((1, LANES), const),
            pl.BlockSpec((1, LANES), const),
            pl.BlockSpec((LANES, LANES), const),
        ],
        out_specs=[pl.BlockSpec((tm, width), row)] * 3,
        out_shape=[jax.ShapeDtypeStruct((n, width), BF16)] * 3,
        compiler_params=_cp(("parallel",)),
        name="da_qkv",
    )(x, mods, g1.reshape(1, d), qkv_w.astype(BF16), cos, sin, tile2(q_g), tile2(k_g), seg.astype(BF16))


DA_KEY_CHUNK = 256


def _da_attn_kernel(lam_ref, q_ref, k_ref, v_ref, sg_ref, o_ref, vx_ref, acc_ref, *, ctx_tiles, ctx_chunks,
                    all_chunks, out_scale):
    qi = pl.program_id(2)
    hw = v_ref.shape[1]
    kc = DA_KEY_CHUNK

    @pl.when(qi == 0)
    def _():
        vx_ref[:, :hw] = v_ref[...]
        vx_ref[:, hw:] = jnp.ones((vx_ref.shape[0], hw), vx_ref.dtype)

    q = q_ref[...]
    tq = q.shape[0]
    lane = lax.broadcasted_iota(jnp.int32, q.shape, 1)
    zero = jnp.zeros_like(q)
    qs = (jnp.where(lane < DA_HEAD_DIM, q, zero), jnp.where(lane < DA_HEAD_DIM, zero, q))
    nt = (((1,), (1,)), ((), ()))
    acc_ref[...] = jnp.zeros_like(acc_ref)

    def step(j, ms):
        r0 = j * kc if isinstance(j, int) else pl.multiple_of(j * kc, kc)
        kb = k_ref[pl.ds(r0, kc), :]
        vb = vx_ref[pl.ds(r0, kc), :]
        new = []
        for c in range(2):
            s = lax.dot_general(qs[c], kb, nt, preferred_element_type=F32)
            mn = jnp.maximum(ms[c], jnp.max(s, axis=-1, keepdims=True))
            e = jnp.exp2(s - mn).astype(BF16)
            acc_ref[c] = jnp.exp2(ms[c] - mn) * acc_ref[c] + jnp.dot(e, vb, preferred_element_type=F32)
            new.append(mn)
        return tuple(new)

    neg = jnp.full((tq, 1), -jnp.inf, F32)
    ms = (neg, neg)
    for j in range(ctx_chunks):
        ms = step(j, ms)

    @pl.when(qi >= ctx_tiles)
    def _():
        mm = ms
        for j in range(ctx_chunks, all_chunks):
            mm = step(j, mm)

    o = acc_ref[0][:, :hw] / acc_ref[0][:, hw:] - lam_ref[0] * (acc_ref[1][:, :hw] / acc_ref[1][:, hw:])
    o_ref[...] = (_rms(o) * sg_ref[...] * out_scale).astype(o_ref.dtype)


def _da_attn(q, k, v, lam, sub_g, lam_init, geo, tq=MOD_TILE):
    n, width = q.shape
    hw = 2 * DA_HEAD_DIM
    n_heads = width // hw
    rpb = geo.rows_per_batch
    tpb, ct, _ = geo.tiles(tq)
    grid_spec = pltpu.PrefetchScalarGridSpec(
        num_scalar_prefetch=0,
        grid=(geo.n_batch, n_heads, tpb),
        in_specs=[
            pl.BlockSpec(memory_space=pltpu.SMEM),
            pl.BlockSpec((tq, hw), lambda b, h, i: (b * tpb + i, h)),
            pl.BlockSpec((rpb, hw), lambda b, h, i: (b, h)),
            pl.BlockSpec((rpb, hw), lambda b, h, i: (b, h)),
            pl.BlockSpec((1, hw), lambda b, h, i: (0, 0)),
        ],
        out_specs=pl.BlockSpec((tq, hw), lambda b, h, i: (b * tpb + i, h)),
        scratch_shapes=[pltpu.VMEM((rpb, 2 * hw), BF16), pltpu.VMEM((2, tq, 2 * hw), F32)],
    )
    return pl.pallas_call(
        functools.partial(_da_attn_kernel, ctx_tiles=ct, ctx_chunks=geo.n_ctx // DA_KEY_CHUNK,
                          all_chunks=rpb // DA_KEY_CHUNK, out_scale=1.0 - lam_init),
        grid_spec=grid_spec,
        out_shape=jax.ShapeDtypeStruct((n, width), BF16),
        compiler_params=_cp(("parallel", "parallel", "arbitrary")),
        name="da_attn",
    )(lam.reshape(1).astype(F32), q, k, v, sub_g.reshape(1, hw).astype(F32))


def _matmul_kernel(x_ref, w_ref, o_ref):
    o_ref[...] = jnp.dot(x_ref[...], w_ref[...], preferred_element_type=F32)


def _matmul(x, w, tm=ROW_TILE):
    n, kd = x.shape
    d = w.shape[1]
    return pl.pallas_call(
        _matmul_kernel,
        grid=(n // tm,),
        in_specs=[pl.BlockSpec((tm, kd), lambda i: (i, 0)), pl.BlockSpec((kd, d), lambda i: (0, 0))],
        out_specs=pl.BlockSpec((tm, d), lambda i: (i, 0)),
        out_shape=jax.ShapeDtypeStruct((n, d), F32),
        compiler_params=_cp(("parallel",)),
        name="da_out",
    )(x, w)


def _da_mixer(x, mods, g1, prm, layer_idx, geo):
    qkv_w, q_g, k_g, lam_vec, sub_g, out_w = prm
    lam_init = 0.8 - 0.6 * math.exp(-0.3 * layer_idx)
    lv = lam_vec.astype(F32)
    lam = jnp.exp(jnp.sum(lv[0] * lv[1])) - jnp.exp(jnp.sum(lv[2] * lv[3])) + lam_init
    q, k, v = _da_qkv(x, mods, g1, qkv_w, q_g, k_g, geo)
    o = _da_attn(q, k, v, lam, sub_g, lam_init, geo)
    return _matmul(o, out_w.astype(BF16))


def kernel(x, c, ctx, c_ctx, mod_w, mod_b, norm1_g, norm2_g, s5_a_re, s5_a_im, s5_log_dt, s5_b_re, s5_b_im, s5_c_re, s5_c_im, s5_d, s5_glu_w, s5_glu_b, ssd_in_w, ssd_conv_w, ssd_conv_b, ssd_dt_bias, ssd_a_log, ssd_d, ssd_norm_g, ssd_out_w, da_qkv_w, da_q_g, da_k_g, da_lam, da_sub_g, da_out_w, moe_router_w, moe_router_b, moe_gu_w, moe_gu_b, moe_dn_w, moe_dn_b):
    bsz, n_lat, d = x.shape
    n_ctx = ctx.shape[1]
    depth = mod_w.shape[0]
    geo = Geo(bsz, n_ctx, n_lat)
    mods_all = _mods(c, c_ctx, mod_w, mod_b)
    n_mixers = 3
    xs = None
    for i in range(depth):
        last = i == depth - 1
        kind, j = i % n_mixers, i // n_mixers
        mods = mods_all[i]
        if kind == 0:
            prm = (s5_a_re[j], s5_a_im[j], s5_log_dt[j], s5_b_re[j], s5_b_im[j], s5_c_re[j], s5_c_im[j],
                   s5_d[j], s5_glu_w[j], s5_glu_b[j])
            if i == 0:
                h, xs = _prenorm_join(ctx, x, mods, norm1_g[i], geo, F32)
            else:
                h = _prenorm(xs, mods, norm1_g[i], geo, F32)
            y = _s5_mixer(h, prm, geo)
        elif kind == 1:
            prm = (ssd_in_w[j], ssd_conv_w[j], ssd_conv_b[j], ssd_dt_bias[j], ssd_a_log[j], ssd_d[j],
                   ssd_norm_g[j], ssd_out_w[j])
            y = _ssd_mixer(xs, mods, norm1_g[i], prm, geo)
        else:
            prm = (da_qkv_w[j], da_q_g[j], da_k_g[j], da_lam[j], da_sub_g[j], da_out_w[j])
            y = _da_mixer(xs, mods, norm1_g[i], prm, i, geo)
        xs = _moe_layer(xs, y, mods, norm2_g[i], moe_router_w[i], moe_router_b[i], moe_gu_w[i], moe_gu_b[i],
                        moe_dn_w[i], moe_dn_b[i], geo, last)
    return xs.reshape(bsz, n_lat, d)
```

```python
import functools
import math
from typing import NamedTuple

import jax
import jax.numpy as jnp
from jax import lax
from jax.experimental import pallas as pl
from jax.experimental.pallas import tpu as pltpu

F32 = jnp.float32
BF16 = jnp.bfloat16
EPS = 1e-6
HP = lax.Precision.HIGHEST

LANES = 128
VMEM_LIMIT = 56 * 1024 * 1024

N_EXPERTS = 32
TOP_K = 4
SWIGLU_ALPHA = 1.702
SWIGLU_LIMIT = 7.0
MOE_BLK = 512
MOD_TILE = 256
ROW_TILE = 512


class Geo(NamedTuple):
    n_batch: int
    n_ctx: int
    n_lat: int

    @property
    def rows_per_batch(self):
        return self.n_ctx + self.n_lat

    @property
    def rows(self):
        return self.n_batch * self.rows_per_batch

    def tiles(self, tm):
        return self.rows_per_batch // tm, self.n_ctx // tm, self.n_lat // tm


def _cp(sem, vmem=VMEM_LIMIT):
    return pltpu.CompilerParams(dimension_semantics=sem, vmem_limit_bytes=vmem)


def _tile_maps(geo, tm, lat_only):
    tpb, ct, lt = geo.tiles(tm)
    if lat_only:
        src = lambda i: (i // lt) * tpb + ct + i % lt
        mod = lambda i: i // lt
        return geo.n_batch * lt, src, mod
    mod = lambda i: jnp.where(i % tpb < ct, geo.n_batch, i // tpb)
    return geo.n_batch * tpb, (lambda i: i), mod


def _rms(x):
    return x * lax.rsqrt(jnp.mean(x * x, axis=-1, keepdims=True) + EPS)


def _mods_kernel(s_ref, w_ref, b_ref, o_ref):
    s = s_ref[...]
    s = s * jax.nn.sigmoid(s)
    o_ref[0] = jnp.dot(s, w_ref[0], preferred_element_type=F32, precision=HP) + b_ref[0]


def _mods(c, c_ctx, mod_w, mod_b):
    depth, d, d6 = mod_w.shape
    bsz = c.shape[0]
    rows = -(-(bsz + 1) // 8) * 8
    s = jnp.zeros((rows, d), F32).at[:bsz].set(c).at[bsz].set(c_ctx)
    nj = d6 // d
    out = pl.pallas_call(
        _mods_kernel,
        grid=(depth, nj),
        in_specs=[
            pl.BlockSpec((rows, d), lambda l, j: (0, 0)),
            pl.BlockSpec((1, d, d), lambda l, j: (l, 0, j)),
            pl.BlockSpec((1, 1, d), lambda l, j: (l, 0, j)),
        ],
        out_specs=pl.BlockSpec((1, rows, d), lambda l, j: (l, 0, j)),
        out_shape=jax.ShapeDtypeStruct((depth, rows, d6), F32),
        compiler_params=_cp(("arbitrary", "arbitrary")),
        name="mods",
    )(s, mod_w, mod_b.reshape(depth, 1, d6))
    out = out[:, :bsz + 1].reshape(depth, bsz + 1, nj, d)
    return jnp.pad(out, ((0, 0), (0, 0), (0, 8 - nj), (0, 0)))


def _to_row_tiles(ref, val):
    rows, d = val.shape
    sub = d // LANES
    for s in range(sub):
        ref[pl.ds(s, rows, stride=sub), :] = val[:, s * LANES:(s + 1) * LANES]


def _from_row_tiles(ref, rows, sub, dtype=None):
    parts = [ref[pl.ds(s, rows, stride=sub), :] for s in range(sub)]
    if dtype is not None:
        parts = [p.astype(dtype) for p in parts]
    return jnp.concatenate(parts, axis=1)


def _route_kernel(x_ref, y_ref, mod_ref, g_ref, rwh_ref, rwl_ref, rb_ref, tri_ref,
                  xn_ref, h_ref, route_ref, cnt_ref, base_ref):
    i = pl.program_id(0)

    @pl.when(i == 0)
    def _():
        base_ref[...] = jnp.zeros_like(base_ref)

    m = mod_ref[0]
    x = x_ref[...] + m[2:3, :] * y_ref[...]
    xn_ref[...] = x
    h = _rms(x) * g_ref[...] * (1.0 + m[4:5, :]) + m[3:4, :]
    _to_row_tiles(h_ref, h)
    dot = functools.partial(jnp.dot, preferred_element_type=F32)
    h_hi = h.astype(BF16)
    h_lo = (h - h_hi.astype(F32)).astype(BF16)
    logits = dot(h_hi, rwh_ref[...]) + dot(h_lo, rwh_ref[...]) + dot(h_hi, rwl_ref[...]) + rb_ref[...]
    tm = logits.shape[0]
    lane = lax.broadcasted_iota(jnp.int32, (tm, LANES), 1).astype(F32)
    neg = jnp.float32(-jnp.inf)
    work = jnp.where(lane < N_EXPERTS, logits, neg)
    vals, ohs, idxs = [], [], []
    for _ in range(TOP_K):
        mk = jnp.max(work, axis=-1, keepdims=True)
        ik = jnp.min(jnp.where(work == mk, lane, float(LANES)), axis=-1, keepdims=True)
        oh = lane == ik
        work = jnp.where(oh, neg, work)
        vals.append(mk)
        ohs.append(oh)
        idxs.append(ik)
    es = [jnp.exp(v - vals[0]) for v in vals]
    den = es[0] + es[1] + es[2] + es[3]
    onehot = jnp.zeros((tm, LANES), F32)
    for oh in ohs:
        onehot = onehot + jnp.where(oh, 1.0, 0.0)
    prefix = jnp.dot(tri_ref[...], onehot.astype(BF16), preferred_element_type=F32)
    pos = prefix + base_ref[0:1, :]
    out = jnp.zeros((tm, LANES), F32)
    for k in range(TOP_K):
        rank = jnp.sum(jnp.where(ohs[k], pos, 0.0), axis=-1, keepdims=True)
        out = out + jnp.where(lane == k, idxs[k], 0.0)
        out = out + jnp.where(lane == TOP_K + k, rank, 0.0)
        out = out + jnp.where(lane == 2 * TOP_K + k, es[k] / den, 0.0)
    route_ref[...] = out
    newbase = base_ref[0:1, :] + jnp.sum(onehot, axis=0, keepdims=True)
    base_ref[...] = jnp.broadcast_to(newbase, base_ref.shape)
    cnt_ref[...] = jnp.broadcast_to(newbase, cnt_ref.shape)


def _route(x, y, mods, g2, rw, rb, geo, lat_only, tm=MOD_TILE):
    d = x.shape[1]
    sub = d // LANES
    nt, src, mod = _tile_maps(geo, tm, lat_only)
    n = nt * tm
    rw_p = jnp.zeros((d, LANES), F32).at[:, :N_EXPERTS].set(rw)
    rw_hi = rw_p.astype(BF16)
    rw_lo = (rw_p - rw_hi.astype(F32)).astype(BF16)
    rb_p = jnp.zeros((1, LANES), F32).at[0, :N_EXPERTS].set(rb)
    tri = (jnp.arange(tm)[:, None] > jnp.arange(tm)[None, :]).astype(BF16)
    const = lambda i: (0, 0)
    return pl.pallas_call(
        _route_kernel,
        grid=(nt,),
        in_specs=[
            pl.BlockSpec((tm, d), lambda i: (src(i), 0)),
            pl.BlockSpec((tm, d), lambda i: (src(i), 0)),
            pl.BlockSpec((1, 8, d), lambda i: (mod(i), 0, 0)),
            pl.BlockSpec((1, d), const),
            pl.BlockSpec((d, LANES), const),
            pl.BlockSpec((d, LANES), const),
            pl.BlockSpec((1, LANES), const),
            pl.BlockSpec((tm, tm), const),
        ],
        out_specs=[
            pl.BlockSpec((tm, d), lambda i: (i, 0)),
            pl.BlockSpec((tm * sub, LANES), lambda i: (i, 0)),
            pl.BlockSpec((tm, LANES), lambda i: (i, 0)),
            pl.BlockSpec((8, LANES), const),
        ],
        out_shape=[
            jax.ShapeDtypeStruct((n, d), F32),
            jax.ShapeDtypeStruct((n * sub, LANES), F32),
            jax.ShapeDtypeStruct((n, LANES), F32),
            jax.ShapeDtypeStruct((8, LANES), F32),
        ],
        scratch_shapes=[pltpu.VMEM((8, LANES), F32)],
        compiler_params=_cp(("arbitrary",)),
        name="route",
    )(x, y, mods, g2.reshape(1, d), rw_hi, rw_lo, rb_p, tri)


def _dispatch_kernel(last_ref, dest_ref, h_ref, xs_ref, zero_ref, sem, zsem, *, sub, blk):
    i = pl.program_id(0)
    tm = h_ref.shape[0] // sub

    @pl.when(i == 0)
    def _():
        zero_ref[...] = jnp.zeros_like(zero_ref)

        def zcopy(e):
            return pltpu.make_async_copy(zero_ref, xs_ref.at[pl.ds(last_ref[e] * (blk * sub), blk * sub), :], zsem)

        def zstart(e, c):
            @pl.when(last_ref[e] >= 0)
            def _():
                zcopy(e).start()
            return c

        def zwait(e, c):
            @pl.when(last_ref[e] >= 0)
            def _():
                zcopy(e).wait()
            return c

        lax.fori_loop(0, N_EXPERTS, zstart, 0)
        lax.fori_loop(0, N_EXPERTS, zwait, 0)

    tok_per_row = LANES // TOP_K

    def start(row, c):
        for ln in range(LANES):
            src0 = pl.multiple_of(row * (tok_per_row * sub) + (ln // TOP_K) * sub, sub)
            dst0 = pl.multiple_of(dest_ref[row, ln] * sub, sub)
            pltpu.make_async_copy(h_ref.at[pl.ds(src0, sub), :], xs_ref.at[pl.ds(dst0, sub), :], sem).start(
                priority=ln % 2)
        return c

    lax.fori_loop(0, tm // tok_per_row, start, 0)
    for k in range(TOP_K):
        pltpu.make_async_copy(h_ref, xs_ref.at[pl.ds(0, tm * sub), :], sem).wait()


def _dispatch(h, dest2d, last_blk, n_slots, sub, tm=ROW_TILE):
    n = h.shape[0] // sub
    nt = n // tm
    db = tm * TOP_K // LANES
    grid_spec = pltpu.PrefetchScalarGridSpec(
        num_scalar_prefetch=1,
        grid=(nt,),
        in_specs=[
            pl.BlockSpec((db, LANES), lambda i, lb: (i, 0), memory_space=pltpu.SMEM),
            pl.BlockSpec((tm * sub, LANES), lambda i, lb: (i, 0)),
        ],
        out_specs=pl.BlockSpec(memory_space=pl.ANY),
        scratch_shapes=[pltpu.VMEM((MOE_BLK * sub, LANES), F32), pltpu.SemaphoreType.DMA(()),
                        pltpu.SemaphoreType.DMA(())],
    )
    return pl.pallas_call(
        functools.partial(_dispatch_kernel, sub=sub, blk=MOE_BLK),
        grid_spec=grid_spec,
        out_shape=jax.ShapeDtypeStruct((n_slots * sub, LANES), F32),
        compiler_params=_cp(("arbitrary",)),
        name="dispatch",
    )(last_blk, dest2d, h)


def _expert_kernel(be_ref, na_ref, xs_ref, guw_ref, gub_ref, dnw_ref, dnb_ref, ys_ref, guw_bf, dnw_bf):
    j = pl.program_id(0)
    active = j < na_ref[0]

    @pl.when(jnp.logical_and(active, jnp.logical_or(j == 0, be_ref[j] != be_ref[jnp.maximum(j - 1, 0)])))
    def _():
        guw_bf[...] = guw_ref[0, 0].astype(BF16)
        dnw_bf[...] = dnw_ref[0, 0].astype(BF16)

    @pl.when(active)
    def _():
        f, d = dnw_bf.shape
        sub = d // LANES
        x = _from_row_tiles(xs_ref, xs_ref.shape[0] // sub, sub, BF16)
        gu = jnp.dot(x, guw_bf[...], preferred_element_type=F32) + gub_ref[0]
        gate = jnp.minimum(gu[:, :f], SWIGLU_LIMIT)
        up = jnp.clip(gu[:, f:], -SWIGLU_LIMIT, SWIGLU_LIMIT)
        act = (up + 1.0) * gate * jax.nn.sigmoid(SWIGLU_ALPHA * gate)
        y = jnp.dot(act.astype(BF16), dnw_bf[...], preferred_element_type=F32) + dnb_ref[0]
        _to_row_tiles(ys_ref, y)


def _experts(xs, block_e, n_active, layer, guw, gub, dnw, dnb, blk=MOE_BLK):
    _, e, d, f2 = guw.shape
    f = f2 // 2
    sub = d // LANES
    nb = xs.shape[0] // (blk * sub)
    act_blk = lambda j, be, na: (jnp.minimum(j, na[0] - 1), 0)
    wsel = lambda j, be, na: (layer, be[j], 0, 0)
    bsel = lambda j, be, na: (be[j], 0, 0)
    grid_spec = pltpu.PrefetchScalarGridSpec(
        num_scalar_prefetch=2,
        grid=(nb,),
        in_specs=[
            pl.BlockSpec((blk * sub, LANES), act_blk),
            pl.BlockSpec((1, 1, d, f2), wsel),
            pl.BlockSpec((1, 1, f2), bsel),
            pl.BlockSpec((1, 1, f, d), wsel),
            pl.BlockSpec((1, 1, d), bsel),
        ],
        out_specs=pl.BlockSpec((blk * sub, LANES), act_blk),
        scratch_shapes=[pltpu.VMEM((d, f2), BF16), pltpu.VMEM((f, d), BF16)],
    )
    return pl.pallas_call(
        _expert_kernel,
        grid_spec=grid_spec,
        out_shape=jax.ShapeDtypeStruct(xs.shape, F32),
        compiler_params=_cp(("arbitrary",)),
        name="experts",
    )(block_e, n_active, xs, guw, gub.reshape(e, 1, f2), dnw, dnb.reshape(e, 1, d))


def _combine_kernel(dest_ref, next_ref, x_ref, route_ref, mod_ref, ys_ref, o_ref, buf, sems):
    i = pl.program_id(0)
    nt = pl.num_programs(0)
    tm, d = x_ref.shape
    sub = d // LANES
    tok_per_row = LANES // TOP_K

    def gather(idx_ref, slot):
        def start(row, c):
            for ln in range(LANES):
                src0 = pl.multiple_of(idx_ref[row, ln] * sub, sub)
                dst0 = pl.multiple_of(row * (tok_per_row * sub) + (ln // TOP_K) * sub, sub)
                pltpu.make_async_copy(ys_ref.at[pl.ds(src0, sub), :], buf.at[slot, ln % TOP_K, pl.ds(dst0, sub), :],
                                      sems.at[slot]).start(priority=ln % 2)
            return c

        lax.fori_loop(0, tm // tok_per_row, start, 0)

    @pl.when(i == 0)
    def _():
        gather(dest_ref, 0)

    @pl.when(i + 1 < nt)
    def _():
        gather(next_ref, (i + 1) % 2)

    slot = i % 2
    for k in range(TOP_K):
        pltpu.make_async_copy(ys_ref.at[pl.ds(0, tm * sub), :], buf.at[slot, k], sems.at[slot]).wait()
    route = route_ref[...]
    gates = [route[:, 2 * TOP_K + k:2 * TOP_K + k + 1] for k in range(TOP_K)]
    scale = mod_ref[0][5:6, :]
    for s in range(sub):
        cols = slice(s * LANES, (s + 1) * LANES)
        f = gates[0] * buf[slot, 0, pl.ds(s, tm, stride=sub), :]
        for k in range(1, TOP_K):
            f = f + gates[k] * buf[slot, k, pl.ds(s, tm, stride=sub), :]
        o_ref[:, cols] = x_ref[:, cols] + scale[:, cols] * f


def _combine(x, route, mods, ys, dest2d, geo, lat_only, tm=MOD_TILE):
    n, d = x.shape
    sub = d // LANES
    nt, _, mod = _tile_maps(geo, tm, lat_only)
    db = tm * TOP_K // LANES
    return pl.pallas_call(
        _combine_kernel,
        grid=(nt,),
        in_specs=[
            pl.BlockSpec((db, LANES), lambda i: (i, 0), memory_space=pltpu.SMEM),
            pl.BlockSpec((db, LANES), lambda i: (jnp.minimum(i + 1, nt - 1), 0), memory_space=pltpu.SMEM),
            pl.BlockSpec((tm, d), lambda i: (i, 0)),
            pl.BlockSpec((tm, LANES), lambda i: (i, 0)),
            pl.BlockSpec((1, 8, d), lambda i: (mod(i), 0, 0)),
            pl.BlockSpec(memory_space=pl.ANY),
        ],
        out_specs=pl.BlockSpec((tm, d), lambda i: (i, 0)),
        out_shape=jax.ShapeDtypeStruct((n, d), F32),
        scratch_shapes=[pltpu.VMEM((2, TOP_K, tm * sub, LANES), F32), pltpu.SemaphoreType.DMA((2,))],
        compiler_params=_cp(("arbitrary",)),
        name="combine",
    )(dest2d, dest2d, x, route, mods, ys)


def _moe_layer(x, y, mods, g2, rw, rb, layer, guw, gub, dnw, dnb, geo, lat_only):
    xn, h, route, cnt = _route(x, y, mods, g2, rw, rb, geo, lat_only)
    n, d = xn.shape
    sub = d // LANES
    counts = cnt[0, :N_EXPERTS].astype(jnp.int32)
    nblk = (counts + MOE_BLK - 1) // MOE_BLK
    blk_end = jnp.cumsum(nblk)
    pad_start = (blk_end - nblk) * MOE_BLK
    n_blocks = -(-(n * TOP_K) // MOE_BLK) + N_EXPERTS
    block_e = jnp.minimum(jnp.sum(blk_end[None, :] <= jnp.arange(n_blocks)[:, None], axis=1), N_EXPERTS - 1)
    n_active = blk_end[-1:].astype(jnp.int32)
    last_blk = jnp.where(nblk > 0, blk_end - 1, -1).astype(jnp.int32)
    eid = route[:, :TOP_K].astype(jnp.int32)
    rank = route[:, TOP_K:2 * TOP_K].astype(jnp.int32)
    dest = (pad_start[eid] + rank).astype(jnp.int32).reshape(n * TOP_K // LANES, LANES)
    xs = _dispatch(h, dest, last_blk, n_blocks * MOE_BLK, sub)
    ys = _experts(xs, block_e.astype(jnp.int32), n_active, layer, guw, gub, dnw, dnb)
    return _combine(xn, route, mods, ys, dest, geo, lat_only)


def _prenorm_kernel(x_ref, mod_ref, g_ref, h_ref):
    m = mod_ref[0]
    h = _rms(x_ref[...]) * g_ref[...] * (1.0 + m[1:2, :]) + m[0:1, :]
    h_ref[...] = h.astype(h_ref.dtype)


def _prenorm(x, mods, g1, geo, dtype, tm=MOD_TILE):
    n, d = x.shape
    nt, _, mod = _tile_maps(geo, tm, False)
    return pl.pallas_call(
        _prenorm_kernel,
        grid=(nt,),
        in_specs=[
            pl.BlockSpec((tm, d), lambda i: (i, 0)),
            pl.BlockSpec((1, 8, d), lambda i: (mod(i), 0, 0)),
            pl.BlockSpec((1, d), lambda i: (0, 0)),
        ],
        out_specs=pl.BlockSpec((tm, d), lambda i: (i, 0)),
        out_shape=jax.ShapeDtypeStruct((n, d), dtype),
        compiler_params=_cp(("parallel",)),
        name="prenorm",
    )(x, mods, g1.reshape(1, d))


def _prenorm_join_kernel(c_ref, x_ref, mod_ref, g_ref, h_ref, xs_ref, *, tiles_per_batch, ctx_tiles):
    is_ctx = pl.program_id(0) % tiles_per_batch < ctx_tiles
    x = jnp.where(is_ctx, c_ref[0], x_ref[0])
    xs_ref[...] = x
    m = mod_ref[0]
    h_ref[...] = (_rms(x) * g_ref[...] * (1.0 + m[1:2, :]) + m[0:1, :]).astype(h_ref.dtype)


def _prenorm_join(ctx, x, mods, g1, geo, dtype, tm=MOD_TILE):
    d = x.shape[-1]
    nt, _, mod = _tile_maps(geo, tm, False)
    tpb, ct, _ = geo.tiles(tm)
    row = lambda i: (i, 0)
    return pl.pallas_call(
        functools.partial(_prenorm_join_kernel, tiles_per_batch=tpb, ctx_tiles=ct),
        grid=(nt,),
        in_specs=[
            pl.BlockSpec((1, tm, d), lambda i: (i // tpb, jnp.minimum(i % tpb, ct - 1), 0)),
            pl.BlockSpec((1, tm, d), lambda i: (i // tpb, jnp.maximum(i % tpb - ct, 0), 0)),
            pl.BlockSpec((1, 8, d), lambda i: (mod(i), 0, 0)),
            pl.BlockSpec((1, d), lambda i: (0, 0)),
        ],
        out_specs=[pl.BlockSpec((tm, d), row), pl.BlockSpec((tm, d), row)],
        out_shape=[jax.ShapeDtypeStruct((geo.rows, d), dtype), jax.ShapeDtypeStruct((geo.rows, d), F32)],
        compiler_params=_cp(("parallel",)),
        name="prenorm_join",
    )(ctx, x, mods, g1.reshape(1, d))


S5_CHUNK = 16
S5_GB = 8


def _s5_matrices(a_re, a_im, log_dt, b_re, b_im, c_re, c_im):
    L = S5_CHUNK
    a_re = a_re.astype(F32)
    a_im = a_im.astype(F32)
    dt = jnp.exp(log_dt.astype(F32))[..., None]
    mag = jnp.exp(dt * a_re)
    ang = dt * a_im
    ab_re, ab_im = mag * jnp.cos(ang), mag * jnp.sin(ang)
    den = a_re * a_re + a_im * a_im
    num_re = ab_re - 1.0
    co_re = (num_re * a_re + ab_im * a_im) / den
    co_im = (ab_im * a_re - num_re * a_im) / den
    bb_re = co_re[..., None] * b_re - co_im[..., None] * b_im
    bb_im = co_re[..., None] * b_im + co_im[..., None] * b_re
    taus = jnp.arange(L + 1, dtype=F32)[:, None, None, None]
    pmag = jnp.exp(taus * dt * a_re)
    pang = taus * dt * a_im
    pw_re, pw_im = pmag * jnp.cos(pang), pmag * jnp.sin(pang)
    cp_re = c_re[None] * pw_re[:, :, :, None, :] - c_im[None] * pw_im[:, :, :, None, :]
    cp_im = c_re[None] * pw_im[:, :, :, None, :] + c_im[None] * pw_re[:, :, :, None, :]
    kk = (jnp.einsum('tdgjp,dgpi->tdgji', cp_re, bb_re, precision=HP)
          - jnp.einsum('tdgjp,dgpi->tdgji', cp_im, bb_im, precision=HP))
    t_idx = jnp.arange(L)
    lag = t_idx[None, :] - t_idx[:, None]

    def toeplitz(kd, lagm):
        kt = kd[jnp.clip(lagm, 0, L)]
        kt = jnp.where((lagm >= 0)[:, :, None, None, None], kt, 0.0)
        return kt.transpose(2, 0, 4, 1, 3)

    g = a_re.shape[1]
    p = a_re.shape[2]
    j = b_re.shape[3]
    m = (toeplitz(kk[:, 0], lag) + toeplitz(kk[:, 1], -lag)).reshape(g, L * j, L * j)

    def inject(d, steps):
        pr, pi = pw_re[steps, d], pw_im[steps, d]
        re = pr[:, :, :, None] * bb_re[d][None] - pi[:, :, :, None] * bb_im[d][None]
        im = pr[:, :, :, None] * bb_im[d][None] + pi[:, :, :, None] * bb_re[d][None]
        f = lambda z: z.transpose(1, 0, 3, 2).reshape(g, L * j, p)
        return f(re), f(im)

    inj = inject(0, L - 1 - t_idx) + inject(1, t_idx)

    def readout(d, steps):
        re = cp_re[steps, d]
        im = cp_im[steps, d]
        f = lambda z: z.transpose(1, 3, 0, 2).reshape(g, p, L * j)
        return f(re), f(-im)

    rd = readout(0, t_idx + 1) + readout(1, L - t_idx)

    gb = S5_GB
    nblk = g // gb

    def spread(z, width):
        z = z.astype(BF16)
        lanes = jnp.arange(gb * width)
        mask = (jnp.arange(gb)[:, None] == lanes[None, :] // width).astype(BF16)
        mask = mask.reshape((1, gb) + (1,) * (z.ndim - 3) + (gb * width,))
        rep = (jnp.arange(width)[:, None] == lanes[None, :] % width).astype(BF16)
        tiled = lax.dot_general(z, rep, (((z.ndim - 1,), (0,)), ((), ())), preferred_element_type=BF16)
        return tiled * mask

    m_blk = spread(m.reshape(nblk, gb, L, j, L, j), j).transpose(0, 2, 1, 3, 4, 5).reshape(nblk, L * gb * j, L * gb * j)

    def inj_blk(re, im):
        f = lambda z: spread(z.reshape(nblk, gb, L, j, p), p).transpose(0, 2, 1, 3, 4).reshape(nblk, L * gb * j, gb * p)
        return jnp.concatenate([f(re), f(im)], axis=2)

    def rd_blk(re, im):
        f = lambda z: spread(z.reshape(nblk, gb, p, L, j), j).reshape(nblk, gb * p, L * gb * j)
        return jnp.concatenate([f(re), f(im)], axis=1)

    def adv_blk(d):
        return jnp.concatenate([pw_re[L, d].reshape(nblk, 1, gb * p), pw_im[L, d].reshape(nblk, 1, gb * p)], axis=2)

    adv = jnp.pad(jnp.concatenate([adv_blk(0), adv_blk(1)], axis=1), ((0, 0), (0, 6), (0, 0)))
    return (m_blk, inj_blk(inj[0], inj[1]), inj_blk(inj[2], inj[3]), rd_blk(rd[0], rd[1]), rd_blk(rd[2], rd[3]), adv)


def _s5_chunk_vectors(x_ref):
    nb, rows, _ = x_ref.shape
    c = rows // S5_CHUNK
    parts = [x_ref[:, pl.ds(l, c, stride=S5_CHUNK), :].reshape(nb * c, LANES).astype(BF16) for l in range(S5_CHUNK)]
    return jnp.concatenate(parts, axis=1)


def _s5_put_states(sloc_ref, val):
    for k in range(sloc_ref.shape[0]):
        sloc_ref[k] = val[:, k * LANES:(k + 1) * LANES]


def _s5_get_states(sloc_ref):
    return jnp.concatenate([sloc_ref[k] for k in range(sloc_ref.shape[0])], axis=1)


def _s5_chunk_scan(sloc_ref, carry_ref, adv, nb, c, reverse):
    nk = sloc_ref.shape[0]
    half = nk // 2
    a = [jnp.broadcast_to(adv[:, k * LANES:(k + 1) * LANES], (nb, LANES)) for k in range(nk)]

    def step(i, st):
        ci = (c - 1 - i) if reverse else i
        rows = pl.ds(ci, nb, stride=c)
        new_re, new_im = [], []
        for k in range(half):
            l_re, l_im = sloc_ref[k, rows, :], sloc_ref[half + k, rows, :]
            s_re, s_im = st[k], st[half + k]
            sloc_ref[k, rows, :] = s_re
            sloc_ref[half + k, rows, :] = s_im
            new_re.append(a[k] * s_re - a[half + k] * s_im + l_re)
            new_im.append(a[k] * s_im + a[half + k] * s_re + l_im)
        return tuple(new_re + new_im)

    st = lax.fori_loop(0, c, step, tuple(carry_ref[k] for k in range(nk)))
    for k in range(nk):
        carry_ref[k] = st[k]


def _s5_bwd_kernel(x_ref, inj_ref, adv_ref, sn_ref, sloc_ref, carry_ref):
    @pl.when(pl.program_id(1) == 0)
    def _():
        carry_ref[...] = jnp.zeros_like(carry_ref)

    nb = x_ref.shape[0]
    c = x_ref.shape[1] // S5_CHUNK
    z = _s5_chunk_vectors(x_ref)
    _s5_put_states(sloc_ref, jnp.dot(z, inj_ref[0], preferred_element_type=F32))
    _s5_chunk_scan(sloc_ref, carry_ref, adv_ref[0][1:2, :], nb, c, True)
    sn_ref[0, 0] = _s5_get_states(sloc_ref).astype(sn_ref.dtype)


def _s5_fwd_kernel(x_ref, sn_ref, m_ref, inj_ref, rdf_ref, rdb_ref, adv_ref, y_ref, sloc_ref, carry_ref):
    @pl.when(pl.program_id(1) == 0)
    def _():
        carry_ref[...] = jnp.zeros_like(carry_ref)

    nb = x_ref.shape[0]
    c = x_ref.shape[1] // S5_CHUNK
    dot = functools.partial(jnp.dot, preferred_element_type=F32)
    z = _s5_chunk_vectors(x_ref)
    _s5_put_states(sloc_ref, dot(z, inj_ref[0]))
    _s5_chunk_scan(sloc_ref, carry_ref, adv_ref[0][0:1, :], nb, c, False)
    y = dot(z, m_ref[0]) + dot(_s5_get_states(sloc_ref).astype(BF16), rdf_ref[0]) + dot(sn_ref[0, 0], rdb_ref[0])
    for t in range(S5_CHUNK):
        y_ref[:, pl.ds(t, c, stride=S5_CHUNK), :] = y[:, t * LANES:(t + 1) * LANES].reshape(nb, c, LANES)


def _s5_core(h, mats, geo, tile=MOD_TILE):
    m, inj_f, inj_b, rd_f, rd_b, adv = mats
    nblk, kdim, sdim = inj_f.shape
    nb, rpb = geo.n_batch, geo.rows_per_batch
    d = h.shape[1]
    tpb, ct, _ = geo.tiles(tile)
    c = tile // S5_CHUNK
    h3 = h.reshape(nb, rpb, d)
    bwd_tile = lambda s: jnp.where(s < ct, ct - 1 - s, tpb - 1 - (s - ct))
    once = pl.Buffered(1)
    wspec = lambda shape: pl.BlockSpec((1,) + shape, lambda g, s: (g, 0, 0), pipeline_mode=once)
    scratch = [pltpu.VMEM((sdim // LANES, nb * c, LANES), F32), pltpu.VMEM((sdim // LANES, nb, LANES), F32)]
    sn = pl.pallas_call(
        _s5_bwd_kernel,
        grid=(nblk, tpb),
        in_specs=[
            pl.BlockSpec((nb, tile, LANES), lambda g, s: (0, bwd_tile(s), g)),
            wspec((kdim, sdim)),
            wspec((8, sdim)),
        ],
        out_specs=pl.BlockSpec((1, 1, nb * c, sdim), lambda g, s: (g, bwd_tile(s), 0, 0)),
        out_shape=jax.ShapeDtypeStruct((nblk, tpb, nb * c, sdim), BF16),
        scratch_shapes=scratch,
        compiler_params=_cp(("parallel", "arbitrary")),
        name="s5_bwd",
    )(h3, inj_b, adv)
    y = pl.pallas_call(
        _s5_fwd_kernel,
        grid=(nblk, tpb),
        in_specs=[
            pl.BlockSpec((nb, tile, LANES), lambda g, s: (0, s, g)),
            pl.BlockSpec((1, 1, nb * c, sdim), lambda g, s: (g, s, 0, 0)),
            wspec((kdim, kdim)),
            wspec((kdim, sdim)),
            wspec((sdim, kdim)),
            wspec((sdim, kdim)),
            wspec((8, sdim)),
        ],
        out_specs=pl.BlockSpec((nb, tile, LANES), lambda g, s: (0, s, g)),
        out_shape=jax.ShapeDtypeStruct((nb, rpb, d), F32),
        scratch_shapes=scratch,
        compiler_params=_cp(("parallel", "arbitrary")),
        name="s5_fwd",
    )(h3, sn, m, inj_f, rd_f, rd_b, adv)
    return y.reshape(nb * rpb, d)


def _gelu_tanh(x):
    return 0.5 * x * (1.0 + jnp.tanh(math.sqrt(2.0 / math.pi) * (x + 0.044715 * (x * x * x))))


def _s5_glu_kernel(h_ref, y_ref, d_ref, w_ref, b_ref, o_ref):
    y = d_ref[...] * h_ref[...].astype(F32) + y_ref[...].astype(F32)
    gl = _gelu_tanh(y).astype(BF16)
    z = jnp.dot(gl, w_ref[...], preferred_element_type=F32) + b_ref[...]
    dd = o_ref.shape[1]
    o_ref[...] = z[:, :dd] * jax.nn.sigmoid(z[:, dd:])


def _s5_glu(h, y, d_skip, glu_w, glu_b, tm=ROW_TILE):
    n, d = h.shape
    return pl.pallas_call(
        _s5_glu_kernel,
        grid=(n // tm,),
        in_specs=[
            pl.BlockSpec((tm, d), lambda i: (i, 0)),
            pl.BlockSpec((tm, d), lambda i: (i, 0)),
            pl.BlockSpec((1, d), lambda i: (0, 0)),
            pl.BlockSpec((d, 2 * d), lambda i: (0, 0)),
            pl.BlockSpec((1, 2 * d), lambda i: (0, 0)),
        ],
        out_specs=pl.BlockSpec((tm, d), lambda i: (i, 0)),
        out_shape=jax.ShapeDtypeStruct((n, d), F32),
        compiler_params=_cp(("parallel",)),
        name="s5_glu",
    )(h, y, d_skip.reshape(1, d).astype(F32), glu_w.astype(BF16), glu_b.reshape(1, 2 * d).astype(F32))


def _s5_mixer(h, prm, geo):
    a_re, a_im, log_dt, b_re, b_im, c_re, c_im, d_skip, glu_w, glu_b = prm
    mats = _s5_matrices(a_re, a_im, log_dt, b_re, b_im, c_re, c_im)
    y = _s5_core(h, mats, geo)
    return _s5_glu(h, y, d_skip, glu_w, glu_b)


SSD_Q = 256
SSD_HEAD_DIM = 64
SSD_STATE = 128
SSD_CONV = 5
CONV_HALO = 16


def _ssd_inproj_kernel(x_ref, mod_ref, g_ref, wz_ref, wx_ref, wd_ref, z_ref, xbc_ref, dt_ref):
    m = mod_ref[0]
    h = (_rms(x_ref[...]) * g_ref[...] * (1.0 + m[1:2, :]) + m[0:1, :]).astype(BF16)
    z_ref[...] = jnp.dot(h, wz_ref[...], preferred_element_type=F32).astype(z_ref.dtype)
    xbc_ref[...] = jnp.dot(h, wx_ref[...], preferred_element_type=F32).astype(xbc_ref.dtype)
    dt_ref[...] = jnp.dot(h, wd_ref[...], preferred_element_type=F32)


def _ssd_inproj(x, mods, g1, in_w, d_inner, n_bc, geo, tm=MOD_TILE):
    n, d = x.shape
    nt, _, mod = _tile_maps(geo, tm, False)
    wz = in_w[:, :d_inner].astype(BF16)
    wx = in_w[:, d_inner:2 * d_inner + n_bc].astype(BF16)
    wd = in_w[:, 2 * d_inner + n_bc:]
    wd = jnp.pad(wd, ((0, 0), (0, LANES - wd.shape[1]))).astype(BF16)
    cx = d_inner + n_bc
    const = lambda i: (0, 0)
    row = lambda i: (i, 0)
    return pl.pallas_call(
        _ssd_inproj_kernel,
        grid=(nt,),
        in_specs=[
            pl.BlockSpec((tm, d), row),
            pl.BlockSpec((1, 8, d), lambda i: (mod(i), 0, 0)),
            pl.BlockSpec((1, d), const),
            pl.BlockSpec((d, d_inner), const),
            pl.BlockSpec((d, cx), const),
            pl.BlockSpec((d, LANES), const),
        ],
        out_specs=[pl.BlockSpec((tm, d_inner), row), pl.BlockSpec((tm, cx), row), pl.BlockSpec((tm, LANES), row)],
        out_shape=[
            jax.ShapeDtypeStruct((n, d_inner), BF16),
            jax.ShapeDtypeStruct((n, cx), BF16),
            jax.ShapeDtypeStruct((n, LANES), F32),
        ],
        compiler_params=_cp(("parallel",)),
        name="ssd_inproj",
    )(x, mods, g1.reshape(1, d), wz, wx, wd)


def _ssd_conv_kernel(prev_ref, cur_ref, next_ref, w_ref, b_ref, o_ref, *, tiles_per_batch, ctx_tiles):
    j = pl.program_id(0) % tiles_per_batch
    first = jnp.logical_or(j == 0, j == ctx_tiles)
    last = jnp.logical_or(j == ctx_tiles - 1, j == tiles_per_batch - 1)
    half = CONV_HALO // 2
    cur = cur_ref[...].astype(F32)
    tt = cur.shape[0]
    prev = jnp.where(first, 0.0, prev_ref[...].astype(F32)[half:, :])
    nxt = jnp.where(last, 0.0, next_ref[...].astype(F32)[:half, :])
    ext = jnp.concatenate([prev, cur, nxt], axis=0)
    w = w_ref[...]
    pad = (SSD_CONV - 1) // 2
    acc = jnp.broadcast_to(b_ref[...], cur.shape)
    for k in range(SSD_CONV):
        off = half + k - pad
        acc = acc + w[k:k + 1, :] * ext[off:off + tt, :]
    o_ref[...] = (acc * jax.nn.sigmoid(acc)).astype(o_ref.dtype)


def _ssd_conv(xbc, conv_w, conv_b, geo, tt=MOD_TILE, tc=2048):
    n, cx = xbc.shape
    tpb, ct, _ = geo.tiles(tt)
    nt = n // tt
    hb = tt // CONV_HALO
    nhb = n // CONV_HALO
    wpad = jnp.pad(conv_w, ((0, 8 - conv_w.shape[0]), (0, 0))).astype(F32)
    return pl.pallas_call(
        functools.partial(_ssd_conv_kernel, tiles_per_batch=tpb, ctx_tiles=ct),
        grid=(nt, cx // tc),
        in_specs=[
            pl.BlockSpec((CONV_HALO, tc), lambda i, c: (jnp.maximum(i * hb - 1, 0), c)),
            pl.BlockSpec((tt, tc), lambda i, c: (i, c)),
            pl.BlockSpec((CONV_HALO, tc), lambda i, c: (jnp.minimum((i + 1) * hb, nhb - 1), c)),
            pl.BlockSpec((8, tc), lambda i, c: (0, c)),
            pl.BlockSpec((1, tc), lambda i, c: (0, c)),
        ],
        out_specs=pl.BlockSpec((tt, tc), lambda i, c: (i, c)),
        out_shape=jax.ShapeDtypeStruct((n, cx), BF16),
        compiler_params=_cp(("parallel", "parallel")),
        name="ssd_conv",
    )(xbc, xbc, xbc, wpad, conv_b.reshape(1, cx).astype(F32))


def _ssd_dt_kernel(raw_ref, bias_ref, a_ref, dt_ref, cs_ref, cst_ref, *, n_heads):
    q = raw_ref.shape[0]
    dt = jax.nn.softplus(raw_ref[...] + bias_ref[...])
    dta = dt * a_ref[...]
    r = lax.broadcasted_iota(jnp.int32, (q, q), 0)
    c = lax.broadcasted_iota(jnp.int32, (q, q), 1)
    lower = jnp.where(c <= r, 1.0, 0.0)
    upper = jnp.where(c >= r, 1.0, 0.0)
    cs_f = jnp.dot(lower, dta, preferred_element_type=F32, precision=HP)
    cs_b = jnp.dot(upper, dta, preferred_element_type=F32, precision=HP)
    lane = lax.broadcasted_iota(jnp.int32, dt.shape, 1)
    cs = jnp.where(lane < n_heads, cs_f, cs_b)
    dt_ref[...] = dt
    cs_ref[...] = cs
    cst_ref[0] = cs.T


def _ssd_dt(dt_raw, dt_bias, a_log, n_heads):
    n = dt_raw.shape[0]
    q = SSD_Q
    pad = lambda v: jnp.pad(v.reshape(1, -1).astype(F32), ((0, 0), (0, LANES - v.size)))
    return pl.pallas_call(
        functools.partial(_ssd_dt_kernel, n_heads=n_heads),
        grid=(n // q,),
        in_specs=[
            pl.BlockSpec((q, LANES), lambda i: (i, 0)),
            pl.BlockSpec((1, LANES), lambda i: (0, 0)),
            pl.BlockSpec((1, LANES), lambda i: (0, 0)),
        ],
        out_specs=[
            pl.BlockSpec((q, LANES), lambda i: (i, 0)),
            pl.BlockSpec((q, LANES), lambda i: (i, 0)),
            pl.BlockSpec((1, LANES, q), lambda i: (i, 0, 0)),
        ],
        out_shape=[
            jax.ShapeDtypeStruct((n, LANES), F32),
            jax.ShapeDtypeStruct((n, LANES), F32),
            jax.ShapeDtypeStruct((n // q, LANES, q), F32),
        ],
        compiler_params=_cp(("parallel",)),
        name="ssd_dt",
    )(dt_raw, pad(dt_bias), pad(-jnp.exp(a_log.astype(F32))))


def _ssd_scan_kernel(x_ref, b_ref, c_ref, dt_ref, cs_ref, cst_ref, dsk_ref, y_ref, st_ref, *,
                     n_heads, heads_per_group, ctx_chunks, lat_chunks):
    g = pl.program_id(1)
    q = SSD_Q
    hd = SSD_HEAD_DIM
    width = heads_per_group * hd
    dot = functools.partial(jnp.dot, preferred_element_type=F32)
    lane = lax.broadcasted_iota(jnp.int32, (q, LANES), 1)
    blk = lax.broadcasted_iota(jnp.int32, (q, width), 1) // hd
    r_i = lax.broadcasted_iota(jnp.int32, (q, q), 0)
    c_i = lax.broadcasted_iota(jnp.int32, (q, q), 1)

    def spread(cols):
        out = jnp.broadcast_to(cols[0], (q, width))
        for hh in range(1, heads_per_group):
            out = jnp.where(blk == hh, cols[hh], out)
        return out

    def chunk(ci, direction, first_pass):
        r0 = ci * q if isinstance(ci, int) else pl.multiple_of(ci * q, q)
        rows = pl.ds(r0, q)
        xb = x_ref[rows, :].astype(F32)
        bm = b_ref[rows, :]
        cm = c_ref[rows, :]
        dt = dt_ref[rows, :]
        cs = cs_ref[rows, :]
        col0 = direction * n_heads + g * heads_per_group
        dt_cols, cs_cols, cs_rows = [], [], []
        for hh in range(heads_per_group):
            sel = lane == col0 + hh
            dt_cols.append(jnp.sum(jnp.where(sel, dt, 0.0), axis=1, keepdims=True))
            cs_cols.append(jnp.sum(jnp.where(sel, cs, 0.0), axis=1, keepdims=True))
            cs_rows.append(cst_ref[ci, pl.ds(col0 + hh, 1), :])
        xdt = xb * spread(dt_cols)
        csf = spread(cs_cols)
        edge = csf[q - 1:q, :] if direction == 0 else csf[0:1, :]
        scores = lax.dot_general(cm, bm, (((1,), (1,)), ((), ())), preferred_element_type=F32)
        keep = (c_i <= r_i) if direction == 0 else (c_i >= r_i)
        xdt_b = xdt.astype(BF16)
        state = st_ref[...]
        y = dot(cm, state.astype(BF16)) * jnp.exp(csf)
        for hh in range(heads_per_group):
            seg = jnp.where(keep, cs_cols[hh] - cs_rows[hh], -jnp.inf)
            pm = (scores * jnp.exp(seg)).astype(BF16)
            y = y + dot(pm, jnp.where(blk == hh, xdt_b, jnp.zeros_like(xdt_b)))
        w = (xdt * jnp.exp(edge - csf)).astype(BF16)
        bt = bm.astype(F32).T.astype(BF16)
        st_ref[...] = jnp.exp(edge) * state + dot(bt, w)
        if first_pass:
            y_ref[rows, :] = y + dsk_ref[0] * xb
        else:
            y_ref[rows, :] = y_ref[rows, :] + y

    for direction in range(2):
        st_ref[...] = jnp.zeros_like(st_ref)
        for ci in range(ctx_chunks):
            cc = ci if direction == 0 else ctx_chunks - 1 - ci
            chunk(cc, direction, direction == 0)

        def body(i, carry, direction=direction):
            cc = ctx_chunks + (i if direction == 0 else lat_chunks - 1 - i)
            chunk(cc, direction, direction == 0)
            return carry

        lax.fori_loop(0, lat_chunks, body, 0)


def _ssd_scan(xbc, dt, cs, cst, d_skip, geo, *, n_heads, n_groups):
    n = xbc.shape[0]
    rpb = geo.rows_per_batch
    hpg = n_heads // n_groups
    width = hpg * SSD_HEAD_DIM
    d_inner = n_heads * SSD_HEAD_DIM
    xblocks = d_inner // width
    bblocks = d_inner // SSD_STATE
    cpb = rpb // SSD_Q
    dsk = jnp.repeat(d_skip.astype(F32), SSD_HEAD_DIM).reshape(n_groups, 1, width)
    kern = functools.partial(_ssd_scan_kernel, n_heads=n_heads, heads_per_group=hpg,
                             ctx_chunks=geo.n_ctx // SSD_Q, lat_chunks=geo.n_lat // SSD_Q)
    return pl.pallas_call(
        kern,
        grid=(geo.n_batch, n_groups),
        in_specs=[
            pl.BlockSpec((rpb, width), lambda b, g: (b, g)),
            pl.BlockSpec((rpb, SSD_STATE), lambda b, g: (b, bblocks + g)),
            pl.BlockSpec((rpb, SSD_STATE), lambda b, g: (b, bblocks + n_groups + g)),
            pl.BlockSpec((rpb, LANES), lambda b, g: (b, 0)),
            pl.BlockSpec((rpb, LANES), lambda b, g: (b, 0)),
            pl.BlockSpec((cpb, LANES, SSD_Q), lambda b, g: (b, 0, 0)),
            pl.BlockSpec((1, 1, width), lambda b, g: (g, 0, 0)),
        ],
        out_specs=pl.BlockSpec((rpb, width), lambda b, g: (b, g)),
        out_shape=jax.ShapeDtypeStruct((n, d_inner), F32),
        scratch_shapes=[pltpu.VMEM((SSD_STATE, width), F32)],
        compiler_params=_cp(("parallel", "arbitrary")),
        name="ssd_scan",
    )(xbc, xbc, xbc, dt, cs, cst, dsk)


def _ssd_out_kernel(y_ref, z_ref, g_ref, w_ref, o_ref, *, n_groups):
    z = z_ref[...].astype(F32)
    gy = y_ref[...] * (z * jax.nn.sigmoid(z))
    gw = gy.shape[1] // n_groups
    acc = jnp.zeros(o_ref.shape, F32)
    for gi in range(n_groups):
        sl = slice(gi * gw, (gi + 1) * gw)
        ng = (_rms(gy[:, sl]) * g_ref[:, sl]).astype(BF16)
        acc = acc + jnp.dot(ng, w_ref[sl, :], preferred_element_type=F32)
    o_ref[...] = acc


def _ssd_out(y, z, norm_g, out_w, n_groups, tm=ROW_TILE):
    n, di = y.shape
    d = out_w.shape[1]
    return pl.pallas_call(
        functools.partial(_ssd_out_kernel, n_groups=n_groups),
        grid=(n // tm,),
        in_specs=[
            pl.BlockSpec((tm, di), lambda i: (i, 0)),
            pl.BlockSpec((tm, di), lambda i: (i, 0)),
            pl.BlockSpec((1, di), lambda i: (0, 0)),
            pl.BlockSpec((di, d), lambda i: (0, 0)),
        ],
        out_specs=pl.BlockSpec((tm, d), lambda i: (i, 0)),
        out_shape=jax.ShapeDtypeStruct((n, d), F32),
        compiler_params=_cp(("parallel",)),
        name="ssd_out",
    )(y, z, norm_g.reshape(1, di).astype(F32), out_w.astype(BF16))


def _ssd_mixer(x, mods, g1, prm, geo):
    in_w, conv_w, conv_b, dt_bias, a_log, d_skip, norm_g, out_w = prm
    n_heads = a_log.shape[1]
    d_inner = out_w.shape[0]
    n_bc = conv_w.shape[1] - d_inner
    n_groups = n_bc // (2 * SSD_STATE)
    z, xbc_raw, dt_raw = _ssd_inproj(x, mods, g1, in_w, d_inner, n_bc, geo)
    xbc = _ssd_conv(xbc_raw, conv_w, conv_b, geo)
    dt, cs, cst = _ssd_dt(dt_raw, dt_bias, a_log, n_heads)
    y = _ssd_scan(xbc, dt, cs, cst, d_skip, geo, n_heads=n_heads, n_groups=n_groups)
    return _ssd_out(y, z, norm_g, out_w, n_groups)


DA_HEAD_DIM = 64
ROPE_BASE = 10000.0
GRID_W = 64


def _rope_tables(geo, tm):
    hd = DA_HEAD_DIM
    n_freq = hd // 4
    t = jnp.arange(geo.n_lat, dtype=F32)
    row, col = jnp.floor(t / GRID_W), jnp.mod(t, GRID_W)
    inv = ROPE_BASE ** (-jnp.arange(n_freq, dtype=F32) / n_freq)
    lane = jnp.arange(LANES)
    within = lane % hd
    freq = inv[within % n_freq]
    pos = jnp.where((within < hd // 2)[None, :], row[:, None], col[:, None])
    ang = pos * freq[None, :]
    sign = jnp.where((within % (2 * n_freq)) < n_freq, -1.0, 1.0)
    cos = jnp.concatenate([jnp.cos(ang), jnp.ones((tm, LANES), F32)], axis=0)
    sin = jnp.concatenate([jnp.sin(ang) * sign[None, :], jnp.zeros((tm, LANES), F32)], axis=0)
    return cos, sin


def _da_qkv_kernel(x_ref, mod_ref, g_ref, w_ref, cos_ref, sin_ref, qg_ref, kg_ref, seg_ref,
                   q_ref, k_ref, v_ref):
    m = mod_ref[0]
    h = (_rms(x_ref[...]) * g_ref[...] * (1.0 + m[1:2, :]) + m[0:1, :]).astype(BF16)
    width = q_ref.shape[1]
    tm = h.shape[0]
    cos = cos_ref[...]
    sin = sin_ref[...]
    lane = lax.broadcasted_iota(jnp.int32, (tm, LANES), 1)
    n_freq = DA_HEAD_DIM // 4
    first_half = (lane % (2 * n_freq)) < n_freq
    scale = DA_HEAD_DIM ** -0.5 * math.log2(math.e)
    v_ref[...] = jnp.dot(h, w_ref[:, 2 * width:], preferred_element_type=F32).astype(v_ref.dtype)
    for o_ref, off, gain, mult in ((q_ref, 0, qg_ref, scale), (k_ref, width, kg_ref, 1.0)):
        for blk in range(width // LANES):
            cols = slice(off + blk * LANES, off + (blk + 1) * LANES)
            a = jnp.dot(h, w_ref[:, cols], preferred_element_type=F32)
            ms = jnp.dot((a * a).astype(BF16), seg_ref[...], preferred_element_type=F32)
            a = a * lax.rsqrt(ms + EPS) * gain[...]
            partner = jnp.where(first_half, pltpu.roll(a, LANES - n_freq, 1), pltpu.roll(a, n_freq, 1))
            a = a * cos + partner * sin
            o_ref[:, blk * LANES:(blk + 1) * LANES] = (a * mult).astype(o_ref.dtype)


def _da_qkv(x, mods, g1, qkv_w, q_g, k_g, geo, tm=MOD_TILE):
    n, d = x.shape
    width = qkv_w.shape[1] // 3
    nt, _, mod = _tile_maps(geo, tm, False)
    tpb, ct, lt = geo.tiles(tm)
    cos, sin = _rope_tables(geo, tm)
    tab = lambda i: (jnp.where(i % tpb < ct, lt, i % tpb - ct), 0)
    lane = jnp.arange(LANES)
    seg = jnp.where((lane[:, None] // DA_HEAD_DIM) == (lane[None, :] // DA_HEAD_DIM), 1.0 / DA_HEAD_DIM, 0.0)
    tile2 = lambda v: jnp.tile(v.astype(F32), LANES // DA_HEAD_DIM).reshape(1, LANES)
    const = lambda i: (0, 0)
    row = lambda i: (i, 0)
    return pl.pallas_call(
        _da_qkv_kernel,
        grid=(nt,),
        in_specs=[
            pl.BlockSpec((tm, d), row),
            pl.BlockSpec((1, 8, d), lambda i: (mod(i), 0, 0)),
            pl.BlockSpec((1, d), const),
            pl.BlockSpec((d, 3 * width), const),
            pl.BlockSpec((tm, LANES), tab),
            pl.BlockSpec((tm, LANES), tab),
            pl.BlockSpec((1, LANES), const),
            pl.BlockSpec((1, LANES), const),
            pl.BlockSpec((LANES, LANES), const),
        ],
        out_specs=[pl.BlockSpec((tm, width), row)] * 3,
        out_shape=[jax.ShapeDtypeStruct((n, width), BF16)] * 3,
        compiler_params=_cp(("parallel",)),
        name="da_qkv",
    )(x, mods, g1.reshape(1, d), qkv_w.astype(BF16), cos, sin, tile2(q_g), tile2(k_g), seg.astype(BF16))


DA_KEY_CHUNK = 256


DA_MIN_ROW_SUM = 2.0 ** -100


def _da_attn_kernel(lam_ref, q_ref, k_ref, v_ref, sg_ref, o_ref, vx_ref, acc_ref, kn_ref, *, ctx_tiles, ctx_chunks,
                    all_chunks, out_scale):
    qi = pl.program_id(2)
    hw = v_ref.shape[1]
    kc = DA_KEY_CHUNK
    q = q_ref[...]
    tq = q.shape[0]
    lane = lax.broadcasted_iota(jnp.int32, (1, hw), 1)
    comp = (lane < DA_HEAD_DIM, lane >= DA_HEAD_DIM)

    @pl.when(qi == 0)
    def _():
        vx_ref[:, :hw] = v_ref[...]
        vx_ref[:, hw:] = jnp.ones((vx_ref.shape[0], hw), vx_ref.dtype)
        k2 = jnp.square(k_ref[...].astype(F32))
        norms = [jnp.max(jnp.sum(jnp.where(comp[c], k2, 0.0), axis=-1, keepdims=True), axis=0, keepdims=True)
                 for c in range(2)]
        kn_ref[...] = jnp.concatenate([jnp.broadcast_to(jnp.sqrt(n), (4, hw)) for n in norms], axis=0)

    zero = jnp.zeros_like(q)
    qs = tuple(jnp.where(comp[c], q, zero) for c in range(2))
    q2 = jnp.square(q.astype(F32))
    bound = [jnp.sqrt(jnp.sum(jnp.where(comp[c], q2, 0.0), axis=-1, keepdims=True)) * kn_ref[4 * c:4 * c + 1, 0:1]
             for c in range(2)]
    nt = (((1,), (1,)), ((), ()))

    def finish():
        o = acc_ref[0][:, :hw] / acc_ref[0][:, hw:] - lam_ref[0] * (acc_ref[1][:, :hw] / acc_ref[1][:, hw:])
        o_ref[...] = (_rms(o) * sg_ref[...] * out_scale).astype(o_ref.dtype)

    def shifted_step(j):
        kb = k_ref[pl.ds(j * kc, kc), :]
        vb = vx_ref[pl.ds(j * kc, kc), :]
        for c in range(2):
            s = lax.dot_general(qs[c], kb, nt, preferred_element_type=F32)
            e = jnp.exp2(s - bound[c]).astype(BF16)
            pv = jnp.dot(e, vb, preferred_element_type=F32)
            acc_ref[c] = pv if j == 0 else acc_ref[c] + pv

    for j in range(ctx_chunks):
        shifted_step(j)

    @pl.when(qi >= ctx_tiles)
    def _():
        for j in range(ctx_chunks, all_chunks):
            shifted_step(j)

    safe = jnp.min(jnp.minimum(acc_ref[0][:, hw:hw + 1], acc_ref[1][:, hw:hw + 1])) >= DA_MIN_ROW_SUM

    @pl.when(safe)
    def _():
        finish()

    @pl.when(jnp.logical_not(safe))
    def _():
        acc_ref[...] = jnp.zeros_like(acc_ref)

        def online_step(j, ms):
            r0 = pl.multiple_of(j * kc, kc)
            kb = k_ref[pl.ds(r0, kc), :]
            vb = vx_ref[pl.ds(r0, kc), :]
            new = []
            for c in range(2):
                s = lax.dot_general(qs[c], kb, nt, preferred_element_type=F32)
                mn = jnp.maximum(ms[c], jnp.max(s, axis=-1, keepdims=True))
                e = jnp.exp2(s - mn).astype(BF16)
                acc_ref[c] = jnp.exp2(ms[c] - mn) * acc_ref[c] + jnp.dot(e, vb, preferred_element_type=F32)
                new.append(mn)
            return tuple(new)

        neg = jnp.full((tq, 1), -jnp.inf, F32)
        lax.fori_loop(0, jnp.where(qi < ctx_tiles, ctx_chunks, all_chunks), online_step, (neg, neg))
        finish()


def _da_attn(q, k, v, lam, sub_g, lam_init, geo, tq=MOD_TILE):
    n, width = q.shape
    hw = 2 * DA_HEAD_DIM
    n_heads = width // hw
    rpb = geo.rows_per_batch
    tpb, ct, _ = geo.tiles(tq)
    grid_spec = pltpu.PrefetchScalarGridSpec(
        num_scalar_prefetch=0,
        grid=(geo.n_batch, n_heads, tpb),
        in_specs=[
            pl.BlockSpec(memory_space=pltpu.SMEM),
            pl.BlockSpec((tq, hw), lambda b, h, i: (b * tpb + i, h)),
            pl.BlockSpec((rpb, hw), lambda b, h, i: (b, h)),
            pl.BlockSpec((rpb, hw), lambda b, h, i: (b, h)),
            pl.BlockSpec((1, hw), lambda b, h, i: (0, 0)),
        ],
        out_specs=pl.BlockSpec((tq, hw), lambda b, h, i: (b * tpb + i, h)),
        scratch_shapes=[pltpu.VMEM((rpb, 2 * hw), BF16), pltpu.VMEM((2, tq, 2 * hw), F32), pltpu.VMEM((8, hw), F32)],
    )
    return pl.pallas_call(
        functools.partial(_da_attn_kernel, ctx_tiles=ct, ctx_chunks=geo.n_ctx // DA_KEY_CHUNK,
                          all_chunks=rpb // DA_KEY_CHUNK, out_scale=1.0 - lam_init),
        grid_spec=grid_spec,
        out_shape=jax.ShapeDtypeStruct((n, width), BF16),
        compiler_params=_cp(("parallel", "parallel", "arbitrary")),
        name="da_attn",
    )(lam.reshape(1).astype(F32), q, k, v, sub_g.reshape(1, hw).astype(F32))


def _matmul_kernel(x_ref, w_ref, o_ref):
    o_ref[...] = jnp.dot(x_ref[...], w_ref[...], preferred_element_type=F32)


def _matmul(x, w, tm=ROW_TILE):
    n, kd = x.shape
    d = w.shape[1]
    return pl.pallas_call(
        _matmul_kernel,
        grid=(n // tm,),
        in_specs=[pl.BlockSpec((tm, kd), lambda i: (i, 0)), pl.BlockSpec((kd, d), lambda i: (0, 0))],
        out_specs=pl.BlockSpec((tm, d), lambda i: (i, 0)),
        out_shape=jax.ShapeDtypeStruct((n, d), F32),
        compiler_params=_cp(("parallel",)),
        name="da_out",
    )(x, w)


def _da_mixer(x, mods, g1, prm, layer_idx, geo):
    qkv_w, q_g, k_g, lam_vec, sub_g, out_w = prm
    lam_init = 0.8 - 0.6 * math.exp(-0.3 * layer_idx)
    lv = lam_vec.astype(F32)
    lam = jnp.exp(jnp.sum(lv[0] * lv[1])) - jnp.exp(jnp.sum(lv[2] * lv[3])) + lam_init
    q, k, v = _da_qkv(x, mods, g1, qkv_w, q_g, k_g, geo)
    o = _da_attn(q, k, v, lam, sub_g, lam_init, geo)
    return _matmul(o, out_w.astype(BF16))


def kernel(x, c, ctx, c_ctx, mod_w, mod_b, norm1_g, norm2_g, s5_a_re, s5_a_im, s5_log_dt, s5_b_re, s5_b_im, s5_c_re, s5_c_im, s5_d, s5_glu_w, s5_glu_b, ssd_in_w, ssd_conv_w, ssd_conv_b, ssd_dt_bias, ssd_a_log, ssd_d, ssd_norm_g, ssd_out_w, da_qkv_w, da_q_g, da_k_g, da_lam, da_sub_g, da_out_w, moe_router_w, moe_router_b, moe_gu_w, moe_gu_b, moe_dn_w, moe_dn_b):
    bsz, n_lat, d = x.shape
    n_ctx = ctx.shape[1]
    depth = mod_w.shape[0]
    geo = Geo(bsz, n_ctx, n_lat)
    mods_all = _mods(c, c_ctx, mod_w, mod_b)
    n_mixers = 3
    xs = None
    for i in range(depth):
        last = i == depth - 1
        kind, j = i % n_mixers, i // n_mixers
        mods = mods_all[i]
        if kind == 0:
            prm = (s5_a_re[j], s5_a_im[j], s5_log_dt[j], s5_b_re[j], s5_b_im[j], s5_c_re[j], s5_c_im[j],
                   s5_d[j], s5_glu_w[j], s5_glu_b[j])
            if i == 0:
                h, xs = _prenorm_join(ctx, x, mods, norm1_g[i], geo, F32)
            else:
                h = _prenorm(xs, mods, norm1_g[i], geo, F32)
            y = _s5_mixer(h, prm, geo)
        elif kind == 1:
            prm = (ssd_in_w[j], ssd_conv_w[j], ssd_conv_b[j], ssd_dt_bias[j], ssd_a_log[j], ssd_d[j],
                   ssd_norm_g[j], ssd_out_w[j])
            y = _ssd_mixer(xs, mods, norm1_g[i], prm, geo)
        else:
            prm = (da_qkv_w[j], da_q_g[j], da_k_g[j], da_lam[j], da_sub_g[j], da_out_w[j])
            y = _da_mixer(xs, mods, norm1_g[i], prm, i, geo)
        xs = _moe_layer(xs, y, mods, norm2_g[i], moe_router_w[i], moe_router_b[i], i, moe_gu_w, moe_gu_b[i],
                        moe_dn_w, moe_dn_b[i], geo, last)
    return xs.reshape(bsz, n_lat, d)
```

```python
import functools
import math
from typing import NamedTuple

import jax
import jax.numpy as jnp
from jax import lax
from jax.experimental import pallas as pl
from jax.experimental.pallas import tpu as pltpu

F32 = jnp.float32
BF16 = jnp.bfloat16
EPS = 1e-6
HP = lax.Precision.HIGHEST

LANES = 128
VMEM_LIMIT = 56 * 1024 * 1024

N_EXPERTS = 32
TOP_K = 4
SWIGLU_ALPHA = 1.702
SWIGLU_LIMIT = 7.0
MOE_BLK = 512
MOD_TILE = 256
ROW_TILE = 512


class Geo(NamedTuple):
    n_batch: int
    n_ctx: int
    n_lat: int

    @property
    def rows_per_batch(self):
        return self.n_ctx + self.n_lat

    @property
    def rows(self):
        return self.n_batch * self.rows_per_batch

    def tiles(self, tm):
        return self.rows_per_batch // tm, self.n_ctx // tm, self.n_lat // tm


def _cp(sem, vmem=VMEM_LIMIT):
    return pltpu.CompilerParams(dimension_semantics=sem, vmem_limit_bytes=vmem)


def _tile_maps(geo, tm, lat_only):
    tpb, ct, lt = geo.tiles(tm)
    if lat_only:
        src = lambda i: (i // lt) * tpb + ct + i % lt
        mod = lambda i: i // lt
        return geo.n_batch * lt, src, mod
    mod = lambda i: jnp.where(i % tpb < ct, geo.n_batch, i // tpb)
    return geo.n_batch * tpb, (lambda i: i), mod


def _rms(x):
    return x * lax.rsqrt(jnp.mean(x * x, axis=-1, keepdims=True) + EPS)


def _mods_kernel(s_ref, w_ref, b_ref, o_ref):
    s = s_ref[...]
    s = s * jax.nn.sigmoid(s)
    o_ref[0] = jnp.dot(s, w_ref[0], preferred_element_type=F32, precision=HP) + b_ref[0]


def _mods(c, c_ctx, mod_w, mod_b):
    depth, d, d6 = mod_w.shape
    bsz = c.shape[0]
    rows = -(-(bsz + 1) // 8) * 8
    s = jnp.zeros((rows, d), F32).at[:bsz].set(c).at[bsz].set(c_ctx)
    nj = d6 // d
    out = pl.pallas_call(
        _mods_kernel,
        grid=(depth, nj),
        in_specs=[
            pl.BlockSpec((rows, d), lambda l, j: (0, 0)),
            pl.BlockSpec((1, d, d), lambda l, j: (l, 0, j)),
            pl.BlockSpec((1, 1, d), lambda l, j: (l, 0, j)),
        ],
        out_specs=pl.BlockSpec((1, rows, d), lambda l, j: (l, 0, j)),
        out_shape=jax.ShapeDtypeStruct((depth, rows, d6), F32),
        compiler_params=_cp(("arbitrary", "arbitrary")),
        name="mods",
    )(s, mod_w, mod_b.reshape(depth, 1, d6))
    out = out[:, :bsz + 1].reshape(depth, bsz + 1, nj, d)
    return jnp.pad(out, ((0, 0), (0, 0), (0, 8 - nj), (0, 0)))


def _to_row_tiles(ref, val):
    rows, d = val.shape
    sub = d // LANES
    for s in range(sub):
        ref[pl.ds(s, rows, stride=sub), :] = val[:, s * LANES:(s + 1) * LANES]


def _from_row_tiles(ref, rows, sub, dtype=None):
    parts = [ref[pl.ds(s, rows, stride=sub), :] for s in range(sub)]
    if dtype is not None:
        parts = [p.astype(dtype) for p in parts]
    return jnp.concatenate(parts, axis=1)


def _route_kernel(x_ref, y_ref, mod_ref, g_ref, rwh_ref, rwl_ref, rb_ref, tri_ref,
                  xn_ref, h_ref, route_ref, cnt_ref, base_ref):
    i = pl.program_id(0)

    @pl.when(i == 0)
    def _():
        base_ref[...] = jnp.zeros_like(base_ref)

    m = mod_ref[0]
    x = x_ref[...] + m[2:3, :] * y_ref[...]
    xn_ref[...] = x
    h = _rms(x) * g_ref[...] * (1.0 + m[4:5, :]) + m[3:4, :]
    _to_row_tiles(h_ref, h)
    dot = functools.partial(jnp.dot, preferred_element_type=F32)
    h_hi = h.astype(BF16)
    h_lo = (h - h_hi.astype(F32)).astype(BF16)
    logits = dot(h_hi, rwh_ref[...]) + dot(h_lo, rwh_ref[...]) + dot(h_hi, rwl_ref[...]) + rb_ref[...]
    tm = logits.shape[0]
    lane = lax.broadcasted_iota(jnp.int32, (tm, LANES), 1).astype(F32)
    neg = jnp.float32(-jnp.inf)
    work = jnp.where(lane < N_EXPERTS, logits, neg)
    vals, ohs, idxs = [], [], []
    for _ in range(TOP_K):
        mk = jnp.max(work, axis=-1, keepdims=True)
        ik = jnp.min(jnp.where(work == mk, lane, float(LANES)), axis=-1, keepdims=True)
        oh = lane == ik
        work = jnp.where(oh, neg, work)
        vals.append(mk)
        ohs.append(oh)
        idxs.append(ik)
    es = [jnp.exp(v - vals[0]) for v in vals]
    den = es[0] + es[1] + es[2] + es[3]
    onehot = jnp.zeros((tm, LANES), F32)
    for oh in ohs:
        onehot = onehot + jnp.where(oh, 1.0, 0.0)
    prefix = jnp.dot(tri_ref[...], onehot.astype(BF16), preferred_element_type=F32)
    pos = prefix + base_ref[0:1, :]
    out = jnp.zeros((tm, LANES), F32)
    for k in range(TOP_K):
        rank = jnp.sum(jnp.where(ohs[k], pos, 0.0), axis=-1, keepdims=True)
        out = out + jnp.where(lane == k, idxs[k], 0.0)
        out = out + jnp.where(lane == TOP_K + k, rank, 0.0)
        out = out + jnp.where(lane == 2 * TOP_K + k, es[k] / den, 0.0)
    route_ref[...] = out
    newbase = base_ref[0:1, :] + jnp.sum(onehot, axis=0, keepdims=True)
    base_ref[...] = jnp.broadcast_to(newbase, base_ref.shape)
    cnt_ref[...] = jnp.broadcast_to(newbase, cnt_ref.shape)


def _route(x, y, mods, g2, rw, rb, geo, lat_only, tm=MOD_TILE):
    d = x.shape[1]
    sub = d // LANES
    nt, src, mod = _tile_maps(geo, tm, lat_only)
    n = nt * tm
    rw_p = jnp.zeros((d, LANES), F32).at[:, :N_EXPERTS].set(rw)
    rw_hi = rw_p.astype(BF16)
    rw_lo = (rw_p - rw_hi.astype(F32)).astype(BF16)
    rb_p = jnp.zeros((1, LANES), F32).at[0, :N_EXPERTS].set(rb)
    tri = (jnp.arange(tm)[:, None] > jnp.arange(tm)[None, :]).astype(BF16)
    const = lambda i: (0, 0)
    return pl.pallas_call(
        _route_kernel,
        grid=(nt,),
        in_specs=[
            pl.BlockSpec((tm, d), lambda i: (src(i), 0)),
            pl.BlockSpec((tm, d), lambda i: (src(i), 0)),
            pl.BlockSpec((1, 8, d), lambda i: (mod(i), 0, 0)),
            pl.BlockSpec((1, d), const),
            pl.BlockSpec((d, LANES), const),
            pl.BlockSpec((d, LANES), const),
            pl.BlockSpec((1, LANES), const),
            pl.BlockSpec((tm, tm), const),
        ],
        out_specs=[
            pl.BlockSpec((tm, d), lambda i: (i, 0)),
            pl.BlockSpec((tm * sub, LANES), lambda i: (i, 0)),
            pl.BlockSpec((tm, LANES), lambda i: (i, 0)),
            pl.BlockSpec((8, LANES), const),
        ],
        out_shape=[
            jax.ShapeDtypeStruct((n, d), F32),
            jax.ShapeDtypeStruct((n * sub, LANES), F32),
            jax.ShapeDtypeStruct((n, LANES), F32),
            jax.ShapeDtypeStruct((8, LANES), F32),
        ],
        scratch_shapes=[pltpu.VMEM((8, LANES), F32)],
        compiler_params=_cp(("arbitrary",)),
        name="route",
    )(x, y, mods, g2.reshape(1, d), rw_hi, rw_lo, rb_p, tri)


def _dispatch_kernel(last_ref, dest_ref, h_ref, xs_ref, zero_ref, sem, zsem, *, sub, blk):
    i = pl.program_id(0)
    tm = h_ref.shape[0] // sub

    @pl.when(i == 0)
    def _():
        zero_ref[...] = jnp.zeros_like(zero_ref)

        def zcopy(e):
            return pltpu.make_async_copy(zero_ref, xs_ref.at[pl.ds(last_ref[e] * (blk * sub), blk * sub), :], zsem)

        def zstart(e, c):
            @pl.when(last_ref[e] >= 0)
            def _():
                zcopy(e).start()
            return c

        def zwait(e, c):
            @pl.when(last_ref[e] >= 0)
            def _():
                zcopy(e).wait()
            return c

        lax.fori_loop(0, N_EXPERTS, zstart, 0)
        lax.fori_loop(0, N_EXPERTS, zwait, 0)

    tok_per_row = LANES // TOP_K

    def start(row, c):
        for ln in range(LANES):
            src0 = pl.multiple_of(row * (tok_per_row * sub) + (ln // TOP_K) * sub, sub)
            dst0 = pl.multiple_of(dest_ref[row, ln] * sub, sub)
            pltpu.make_async_copy(h_ref.at[pl.ds(src0, sub), :], xs_ref.at[pl.ds(dst0, sub), :], sem).start(
                priority=ln % 2)
        return c

    lax.fori_loop(0, tm // tok_per_row, start, 0)
    for k in range(TOP_K):
        pltpu.make_async_copy(h_ref, xs_ref.at[pl.ds(0, tm * sub), :], sem).wait()


def _dispatch(h, dest2d, last_blk, n_slots, sub, tm=ROW_TILE):
    n = h.shape[0] // sub
    nt = n // tm
    db = tm * TOP_K // LANES
    grid_spec = pltpu.PrefetchScalarGridSpec(
        num_scalar_prefetch=1,
        grid=(nt,),
        in_specs=[
            pl.BlockSpec((db, LANES), lambda i, lb: (i, 0), memory_space=pltpu.SMEM),
            pl.BlockSpec((tm * sub, LANES), lambda i, lb: (i, 0)),
        ],
        out_specs=pl.BlockSpec(memory_space=pl.ANY),
        scratch_shapes=[pltpu.VMEM((MOE_BLK * sub, LANES), F32), pltpu.SemaphoreType.DMA(()),
                        pltpu.SemaphoreType.DMA(())],
    )
    return pl.pallas_call(
        functools.partial(_dispatch_kernel, sub=sub, blk=MOE_BLK),
        grid_spec=grid_spec,
        out_shape=jax.ShapeDtypeStruct((n_slots * sub, LANES), F32),
        compiler_params=_cp(("arbitrary",)),
        name="dispatch",
    )(last_blk, dest2d, h)


def _expert_kernel(be_ref, na_ref, xs_ref, guw_ref, gub_ref, dnw_ref, dnb_ref, ys_ref, guw_bf, dnw_bf):
    j = pl.program_id(0)
    active = j < na_ref[0]

    @pl.when(jnp.logical_and(active, jnp.logical_or(j == 0, be_ref[j] != be_ref[jnp.maximum(j - 1, 0)])))
    def _():
        guw_bf[...] = guw_ref[0, 0].astype(BF16)
        dnw_bf[...] = dnw_ref[0, 0].astype(BF16)

    @pl.when(active)
    def _():
        f, d = dnw_bf.shape
        sub = d // LANES
        x = _from_row_tiles(xs_ref, xs_ref.shape[0] // sub, sub, BF16)
        gu = jnp.dot(x, guw_bf[...], preferred_element_type=F32) + gub_ref[0]
        gate = jnp.minimum(gu[:, :f], SWIGLU_LIMIT)
        up = jnp.clip(gu[:, f:], -SWIGLU_LIMIT, SWIGLU_LIMIT)
        act = (up + 1.0) * gate * jax.nn.sigmoid(SWIGLU_ALPHA * gate)
        y = jnp.dot(act.astype(BF16), dnw_bf[...], preferred_element_type=F32) + dnb_ref[0]
        _to_row_tiles(ys_ref, y)


def _experts(xs, block_e, n_active, layer, guw, gub, dnw, dnb, blk=MOE_BLK):
    _, e, d, f2 = guw.shape
    f = f2 // 2
    sub = d // LANES
    nb = xs.shape[0] // (blk * sub)
    act_blk = lambda j, be, na: (jnp.minimum(j, na[0] - 1), 0)
    wsel = lambda j, be, na: (layer, be[j], 0, 0)
    bsel = lambda j, be, na: (be[j], 0, 0)
    grid_spec = pltpu.PrefetchScalarGridSpec(
        num_scalar_prefetch=2,
        grid=(nb,),
        in_specs=[
            pl.BlockSpec((blk * sub, LANES), act_blk),
            pl.BlockSpec((1, 1, d, f2), wsel),
            pl.BlockSpec((1, 1, f2), bsel),
            pl.BlockSpec((1, 1, f, d), wsel),
            pl.BlockSpec((1, 1, d), bsel),
        ],
        out_specs=pl.BlockSpec((blk * sub, LANES), act_blk),
        scratch_shapes=[pltpu.VMEM((d, f2), BF16), pltpu.VMEM((f, d), BF16)],
    )
    return pl.pallas_call(
        _expert_kernel,
        grid_spec=grid_spec,
        out_shape=jax.ShapeDtypeStruct(xs.shape, F32),
        compiler_params=_cp(("arbitrary",)),
        name="experts",
    )(block_e, n_active, xs, guw, gub.reshape(e, 1, f2), dnw, dnb.reshape(e, 1, d))


def _combine_kernel(dest_ref, next_ref, x_ref, route_ref, mod_ref, ys_ref, o_ref, buf, sems):
    i = pl.program_id(0)
    nt = pl.num_programs(0)
    tm, d = x_ref.shape
    sub = d // LANES
    tok_per_row = LANES // TOP_K

    def gather(idx_ref, slot):
        def start(row, c):
            for ln in range(LANES):
                src0 = pl.multiple_of(idx_ref[row, ln] * sub, sub)
                dst0 = pl.multiple_of(row * (tok_per_row * sub) + (ln // TOP_K) * sub, sub)
                pltpu.make_async_copy(ys_ref.at[pl.ds(src0, sub), :], buf.at[slot, ln % TOP_K, pl.ds(dst0, sub), :],
                                      sems.at[slot]).start(priority=ln % 2)
            return c

        lax.fori_loop(0, tm // tok_per_row, start, 0)

    @pl.when(i == 0)
    def _():
        gather(dest_ref, 0)

    @pl.when(i + 1 < nt)
    def _():
        gather(next_ref, (i + 1) % 2)

    slot = i % 2
    for k in range(TOP_K):
        pltpu.make_async_copy(ys_ref.at[pl.ds(0, tm * sub), :], buf.at[slot, k], sems.at[slot]).wait()
    route = route_ref[...]
    gates = [route[:, 2 * TOP_K + k:2 * TOP_K + k + 1] for k in range(TOP_K)]
    scale = mod_ref[0][5:6, :]
    for s in range(sub):
        cols = slice(s * LANES, (s + 1) * LANES)
        f = gates[0] * buf[slot, 0, pl.ds(s, tm, stride=sub), :]
        for k in range(1, TOP_K):
            f = f + gates[k] * buf[slot, k, pl.ds(s, tm, stride=sub), :]
        o_ref[:, cols] = x_ref[:, cols] + scale[:, cols] * f


def _combine(x, route, mods, ys, dest2d, geo, lat_only, tm=MOD_TILE):
    n, d = x.shape
    sub = d // LANES
    nt, _, mod = _tile_maps(geo, tm, lat_only)
    db = tm * TOP_K // LANES
    return pl.pallas_call(
        _combine_kernel,
        grid=(nt,),
        in_specs=[
            pl.BlockSpec((db, LANES), lambda i: (i, 0), memory_space=pltpu.SMEM),
            pl.BlockSpec((db, LANES), lambda i: (jnp.minimum(i + 1, nt - 1), 0), memory_space=pltpu.SMEM),
            pl.BlockSpec((tm, d), lambda i: (i, 0)),
            pl.BlockSpec((tm, LANES), lambda i: (i, 0)),
            pl.BlockSpec((1, 8, d), lambda i: (mod(i), 0, 0)),
            pl.BlockSpec(memory_space=pl.ANY),
        ],
        out_specs=pl.BlockSpec((tm, d), lambda i: (i, 0)),
        out_shape=jax.ShapeDtypeStruct((n, d), F32),
        scratch_shapes=[pltpu.VMEM((2, TOP_K, tm * sub, LANES), F32), pltpu.SemaphoreType.DMA((2,))],
        compiler_params=_cp(("arbitrary",)),
        name="combine",
    )(dest2d, dest2d, x, route, mods, ys)


def _moe_layer(x, y, mods, g2, rw, rb, layer, guw, gub, dnw, dnb, geo, lat_only):
    xn, h, route, cnt = _route(x, y, mods, g2, rw, rb, geo, lat_only)
    n, d = xn.shape
    sub = d // LANES
    counts = cnt[0, :N_EXPERTS].astype(jnp.int32)
    nblk = (counts + MOE_BLK - 1) // MOE_BLK
    blk_end = jnp.cumsum(nblk)
    pad_start = (blk_end - nblk) * MOE_BLK
    n_blocks = -(-(n * TOP_K) // MOE_BLK) + N_EXPERTS
    block_e = jnp.minimum(jnp.sum(blk_end[None, :] <= jnp.arange(n_blocks)[:, None], axis=1), N_EXPERTS - 1)
    n_active = blk_end[-1:].astype(jnp.int32)
    last_blk = jnp.where(nblk > 0, blk_end - 1, -1).astype(jnp.int32)
    eid = route[:, :TOP_K].astype(jnp.int32)
    rank = route[:, TOP_K:2 * TOP_K].astype(jnp.int32)
    dest = (pad_start[eid] + rank).astype(jnp.int32).reshape(n * TOP_K // LANES, LANES)
    xs = _dispatch(h, dest, last_blk, n_blocks * MOE_BLK, sub)
    ys = _experts(xs, block_e.astype(jnp.int32), n_active, layer, guw, gub, dnw, dnb)
    return _combine(xn, route, mods, ys, dest, geo, lat_only)


def _prenorm_kernel(x_ref, mod_ref, g_ref, h_ref):
    m = mod_ref[0]
    h = _rms(x_ref[...]) * g_ref[...] * (1.0 + m[1:2, :]) + m[0:1, :]
    h_ref[...] = h.astype(h_ref.dtype)


def _prenorm(x, mods, g1, geo, dtype, tm=MOD_TILE):
    n, d = x.shape
    nt, _, mod = _tile_maps(geo, tm, False)
    return pl.pallas_call(
        _prenorm_kernel,
        grid=(nt,),
        in_specs=[
            pl.BlockSpec((tm, d), lambda i: (i, 0)),
            pl.BlockSpec((1, 8, d), lambda i: (mod(i), 0, 0)),
            pl.BlockSpec((1, d), lambda i: (0, 0)),
        ],
        out_specs=pl.BlockSpec((tm, d), lambda i: (i, 0)),
        out_shape=jax.ShapeDtypeStruct((n, d), dtype),
        compiler_params=_cp(("parallel",)),
        name="prenorm",
    )(x, mods, g1.reshape(1, d))


def _prenorm_join_kernel(c_ref, x_ref, mod_ref, g_ref, h_ref, xs_ref, *, tiles_per_batch, ctx_tiles):
    is_ctx = pl.program_id(0) % tiles_per_batch < ctx_tiles
    x = jnp.where(is_ctx, c_ref[0], x_ref[0])
    xs_ref[...] = x
    m = mod_ref[0]
    h_ref[...] = (_rms(x) * g_ref[...] * (1.0 + m[1:2, :]) + m[0:1, :]).astype(h_ref.dtype)


def _prenorm_join(ctx, x, mods, g1, geo, dtype, tm=MOD_TILE):
    d = x.shape[-1]
    nt, _, mod = _tile_maps(geo, tm, False)
    tpb, ct, _ = geo.tiles(tm)
    row = lambda i: (i, 0)
    return pl.pallas_call(
        functools.partial(_prenorm_join_kernel, tiles_per_batch=tpb, ctx_tiles=ct),
        grid=(nt,),
        in_specs=[
            pl.BlockSpec((1, tm, d), lambda i: (i // tpb, jnp.minimum(i % tpb, ct - 1), 0)),
            pl.BlockSpec((1, tm, d), lambda i: (i // tpb, jnp.maximum(i % tpb - ct, 0), 0)),
            pl.BlockSpec((1, 8, d), lambda i: (mod(i), 0, 0)),
            pl.BlockSpec((1, d), lambda i: (0, 0)),
        ],
        out_specs=[pl.BlockSpec((tm, d), row), pl.BlockSpec((tm, d), row)],
        out_shape=[jax.ShapeDtypeStruct((geo.rows, d), dtype), jax.ShapeDtypeStruct((geo.rows, d), F32)],
        compiler_params=_cp(("parallel",)),
        name="prenorm_join",
    )(ctx, x, mods, g1.reshape(1, d))


S5_CHUNK = 16
S5_GB = 8


def _s5_matrices(a_re, a_im, log_dt, b_re, b_im, c_re, c_im):
    L = S5_CHUNK
    a_re = a_re.astype(F32)
    a_im = a_im.astype(F32)
    dt = jnp.exp(log_dt.astype(F32))[..., None]
    mag = jnp.exp(dt * a_re)
    ang = dt * a_im
    ab_re, ab_im = mag * jnp.cos(ang), mag * jnp.sin(ang)
    den = a_re * a_re + a_im * a_im
    num_re = ab_re - 1.0
    co_re = (num_re * a_re + ab_im * a_im) / den
    co_im = (ab_im * a_re - num_re * a_im) / den
    bb_re = co_re[..., None] * b_re - co_im[..., None] * b_im
    bb_im = co_re[..., None] * b_im + co_im[..., None] * b_re
    taus = jnp.arange(L + 1, dtype=F32)[:, None, None, None]
    pmag = jnp.exp(taus * dt * a_re)
    pang = taus * dt * a_im
    pw_re, pw_im = pmag * jnp.cos(pang), pmag * jnp.sin(pang)
    cp_re = c_re[None] * pw_re[:, :, :, None, :] - c_im[None] * pw_im[:, :, :, None, :]
    cp_im = c_re[None] * pw_im[:, :, :, None, :] + c_im[None] * pw_re[:, :, :, None, :]
    kk = (jnp.einsum('tdgjp,dgpi->tdgji', cp_re, bb_re, precision=HP)
          - jnp.einsum('tdgjp,dgpi->tdgji', cp_im, bb_im, precision=HP))
    t_idx = jnp.arange(L)
    lag = t_idx[None, :] - t_idx[:, None]

    def toeplitz(kd, lagm):
        kt = kd[jnp.clip(lagm, 0, L)]
        kt = jnp.where((lagm >= 0)[:, :, None, None, None], kt, 0.0)
        return kt.transpose(2, 0, 4, 1, 3)

    g = a_re.shape[1]
    p = a_re.shape[2]
    j = b_re.shape[3]
    m = (toeplitz(kk[:, 0], lag) + toeplitz(kk[:, 1], -lag)).reshape(g, L * j, L * j)

    def inject(d, steps):
        pr, pi = pw_re[steps, d], pw_im[steps, d]
        re = pr[:, :, :, None] * bb_re[d][None] - pi[:, :, :, None] * bb_im[d][None]
        im = pr[:, :, :, None] * bb_im[d][None] + pi[:, :, :, None] * bb_re[d][None]
        f = lambda z: z.transpose(1, 0, 3, 2).reshape(g, L * j, p)
        return f(re), f(im)

    inj = inject(0, L - 1 - t_idx) + inject(1, t_idx)

    def readout(d, steps):
        re = cp_re[steps, d]
        im = cp_im[steps, d]
        f = lambda z: z.transpose(1, 3, 0, 2).reshape(g, p, L * j)
        return f(re), f(-im)

    rd = readout(0, t_idx + 1) + readout(1, L - t_idx)

    gb = S5_GB
    nblk = g // gb

    def spread(z, width):
        z = z.astype(BF16)
        lanes = jnp.arange(gb * width)
        mask = (jnp.arange(gb)[:, None] == lanes[None, :] // width).astype(BF16)
        mask = mask.reshape((1, gb) + (1,) * (z.ndim - 3) + (gb * width,))
        rep = (jnp.arange(width)[:, None] == lanes[None, :] % width).astype(BF16)
        tiled = lax.dot_general(z, rep, (((z.ndim - 1,), (0,)), ((), ())), preferred_element_type=BF16)
        return tiled * mask

    m_blk = spread(m.reshape(nblk, gb, L, j, L, j), j).transpose(0, 2, 1, 3, 4, 5).reshape(nblk, L * gb * j, L * gb * j)

    def inj_blk(re, im):
        f = lambda z: spread(z.reshape(nblk, gb, L, j, p), p).transpose(0, 2, 1, 3, 4).reshape(nblk, L * gb * j, gb * p)
        return jnp.concatenate([f(re), f(im)], axis=2)

    def rd_blk(re, im):
        f = lambda z: spread(z.reshape(nblk, gb, p, L, j), j).reshape(nblk, gb * p, L * gb * j)
        return jnp.concatenate([f(re), f(im)], axis=1)

    def adv_blk(d):
        return jnp.concatenate([pw_re[L, d].reshape(nblk, 1, gb * p), pw_im[L, d].reshape(nblk, 1, gb * p)], axis=2)

    adv = jnp.pad(jnp.concatenate([adv_blk(0), adv_blk(1)], axis=1), ((0, 0), (0, 6), (0, 0)))
    return (m_blk, inj_blk(inj[0], inj[1]), inj_blk(inj[2], inj[3]), rd_blk(rd[0], rd[1]), rd_blk(rd[2], rd[3]), adv)


def _s5_chunk_vectors(x_ref):
    nb, rows, _ = x_ref.shape
    c = rows // S5_CHUNK
    parts = [x_ref[:, pl.ds(l, c, stride=S5_CHUNK), :].reshape(nb * c, LANES).astype(BF16) for l in range(S5_CHUNK)]
    return jnp.concatenate(parts, axis=1)


def _s5_put_states(sloc_ref, val):
    for k in range(sloc_ref.shape[0]):
        sloc_ref[k] = val[:, k * LANES:(k + 1) * LANES]


def _s5_get_states(sloc_ref):
    return jnp.concatenate([sloc_ref[k] for k in range(sloc_ref.shape[0])], axis=1)


def _s5_chunk_scan(sloc_ref, carry_ref, adv, nb, c, reverse):
    nk = sloc_ref.shape[0]
    half = nk // 2
    a = [jnp.broadcast_to(adv[:, k * LANES:(k + 1) * LANES], (nb, LANES)) for k in range(nk)]

    def step(i, st):
        ci = (c - 1 - i) if reverse else i
        rows = pl.ds(ci, nb, stride=c)
        new_re, new_im = [], []
        for k in range(half):
            l_re, l_im = sloc_ref[k, rows, :], sloc_ref[half + k, rows, :]
            s_re, s_im = st[k], st[half + k]
            sloc_ref[k, rows, :] = s_re
            sloc_ref[half + k, rows, :] = s_im
            new_re.append(a[k] * s_re - a[half + k] * s_im + l_re)
            new_im.append(a[k] * s_im + a[half + k] * s_re + l_im)
        return tuple(new_re + new_im)

    st = lax.fori_loop(0, c, step, tuple(carry_ref[k] for k in range(nk)))
    for k in range(nk):
        carry_ref[k] = st[k]


def _s5_bwd_kernel(x_ref, inj_ref, adv_ref, sn_ref, sloc_ref, carry_ref):
    @pl.when(pl.program_id(1) == 0)
    def _():
        carry_ref[...] = jnp.zeros_like(carry_ref)

    nb = x_ref.shape[0]
    c = x_ref.shape[1] // S5_CHUNK
    z = _s5_chunk_vectors(x_ref)
    _s5_put_states(sloc_ref, jnp.dot(z, inj_ref[0], preferred_element_type=F32))
    _s5_chunk_scan(sloc_ref, carry_ref, adv_ref[0][1:2, :], nb, c, True)
    sn_ref[0, 0] = _s5_get_states(sloc_ref).astype(sn_ref.dtype)


def _s5_fwd_kernel(x_ref, sn_ref, m_ref, inj_ref, rdf_ref, rdb_ref, adv_ref, y_ref, sloc_ref, carry_ref):
    @pl.when(pl.program_id(1) == 0)
    def _():
        carry_ref[...] = jnp.zeros_like(carry_ref)

    nb = x_ref.shape[0]
    c = x_ref.shape[1] // S5_CHUNK
    dot = functools.partial(jnp.dot, preferred_element_type=F32)
    z = _s5_chunk_vectors(x_ref)
    _s5_put_states(sloc_ref, dot(z, inj_ref[0]))
    _s5_chunk_scan(sloc_ref, carry_ref, adv_ref[0][0:1, :], nb, c, False)
    y = dot(z, m_ref[0]) + dot(_s5_get_states(sloc_ref).astype(BF16), rdf_ref[0]) + dot(sn_ref[0, 0], rdb_ref[0])
    for t in range(S5_CHUNK):
        y_ref[:, pl.ds(t, c, stride=S5_CHUNK), :] = y[:, t * LANES:(t + 1) * LANES].reshape(nb, c, LANES)


def _s5_core(h, mats, geo, tile=MOD_TILE):
    m, inj_f, inj_b, rd_f, rd_b, adv = mats
    nblk, kdim, sdim = inj_f.shape
    nb, rpb = geo.n_batch, geo.rows_per_batch
    d = h.shape[1]
    tpb, ct, _ = geo.tiles(tile)
    c = tile // S5_CHUNK
    h3 = h.reshape(nb, rpb, d)
    bwd_tile = lambda s: jnp.where(s < ct, ct - 1 - s, tpb - 1 - (s - ct))
    once = pl.Buffered(1)
    wspec = lambda shape: pl.BlockSpec((1,) + shape, lambda g, s: (g, 0, 0), pipeline_mode=once)
    scratch = [pltpu.VMEM((sdim // LANES, nb * c, LANES), F32), pltpu.VMEM((sdim // LANES, nb, LANES), F32)]
    sn = pl.pallas_call(
        _s5_bwd_kernel,
        grid=(nblk, tpb),
        in_specs=[
            pl.BlockSpec((nb, tile, LANES), lambda g, s: (0, bwd_tile(s), g)),
            wspec((kdim, sdim)),
            wspec((8, sdim)),
        ],
        out_specs=pl.BlockSpec((1, 1, nb * c, sdim), lambda g, s: (g, bwd_tile(s), 0, 0)),
        out_shape=jax.ShapeDtypeStruct((nblk, tpb, nb * c, sdim), BF16),
        scratch_shapes=scratch,
        compiler_params=_cp(("parallel", "arbitrary")),
        name="s5_bwd",
    )(h3, inj_b, adv)
    y = pl.pallas_call(
        _s5_fwd_kernel,
        grid=(nblk, tpb),
        in_specs=[
            pl.BlockSpec((nb, tile, LANES), lambda g, s: (0, s, g)),
            pl.BlockSpec((1, 1, nb * c, sdim), lambda g, s: (g, s, 0, 0)),
            wspec((kdim, kdim)),
            wspec((kdim, sdim)),
            wspec((sdim, kdim)),
            wspec((sdim, kdim)),
            wspec((8, sdim)),
        ],
        out_specs=pl.BlockSpec((nb, tile, LANES), lambda g, s: (0, s, g)),
        out_shape=jax.ShapeDtypeStruct((nb, rpb, d), F32),
        scratch_shapes=scratch,
        compiler_params=_cp(("parallel", "arbitrary")),
        name="s5_fwd",
    )(h3, sn, m, inj_f, rd_f, rd_b, adv)
    return y.reshape(nb * rpb, d)


def _gelu_tanh(x):
    return 0.5 * x * (1.0 + jnp.tanh(math.sqrt(2.0 / math.pi) * (x + 0.044715 * (x * x * x))))


def _s5_glu_kernel(h_ref, y_ref, d_ref, w_ref, b_ref, o_ref):
    y = d_ref[...] * h_ref[...].astype(F32) + y_ref[...].astype(F32)
    gl = _gelu_tanh(y).astype(BF16)
    z = jnp.dot(gl, w_ref[...], preferred_element_type=F32) + b_ref[...]
    dd = o_ref.shape[1]
    o_ref[...] = z[:, :dd] * jax.nn.sigmoid(z[:, dd:])


def _s5_glu(h, y, d_skip, glu_w, glu_b, tm=ROW_TILE):
    n, d = h.shape
    return pl.pallas_call(
        _s5_glu_kernel,
        grid=(n // tm,),
        in_specs=[
            pl.BlockSpec((tm, d), lambda i: (i, 0)),
            pl.BlockSpec((tm, d), lambda i: (i, 0)),
            pl.BlockSpec((1, d), lambda i: (0, 0)),
            pl.BlockSpec((d, 2 * d), lambda i: (0, 0)),
            pl.BlockSpec((1, 2 * d), lambda i: (0, 0)),
        ],
        out_specs=pl.BlockSpec((tm, d), lambda i: (i, 0)),
        out_shape=jax.ShapeDtypeStruct((n, d), F32),
        compiler_params=_cp(("parallel",)),
        name="s5_glu",
    )(h, y, d_skip.reshape(1, d).astype(F32), glu_w.astype(BF16), glu_b.reshape(1, 2 * d).astype(F32))


def _s5_mixer(h, prm, geo):
    a_re, a_im, log_dt, b_re, b_im, c_re, c_im, d_skip, glu_w, glu_b = prm
    mats = _s5_matrices(a_re, a_im, log_dt, b_re, b_im, c_re, c_im)
    y = _s5_core(h, mats, geo)
    return _s5_glu(h, y, d_skip, glu_w, glu_b)


SSD_Q = 256
SSD_HEAD_DIM = 64
SSD_STATE = 128
SSD_CONV = 5
CONV_HALO = 16


def _ssd_inproj_kernel(x_ref, mod_ref, g_ref, wz_ref, wx_ref, wd_ref, z_ref, xbc_ref, dt_ref):
    m = mod_ref[0]
    h = (_rms(x_ref[...]) * g_ref[...] * (1.0 + m[1:2, :]) + m[0:1, :]).astype(BF16)
    z_ref[...] = jnp.dot(h, wz_ref[...], preferred_element_type=F32).astype(z_ref.dtype)
    xbc_ref[...] = jnp.dot(h, wx_ref[...], preferred_element_type=F32).astype(xbc_ref.dtype)
    dt_ref[...] = jnp.dot(h, wd_ref[...], preferred_element_type=F32)


def _ssd_inproj(x, mods, g1, in_w, d_inner, n_bc, geo, tm=MOD_TILE):
    n, d = x.shape
    nt, _, mod = _tile_maps(geo, tm, False)
    wz = in_w[:, :d_inner].astype(BF16)
    wx = in_w[:, d_inner:2 * d_inner + n_bc].astype(BF16)
    wd = in_w[:, 2 * d_inner + n_bc:]
    wd = jnp.pad(wd, ((0, 0), (0, LANES - wd.shape[1]))).astype(BF16)
    cx = d_inner + n_bc
    const = lambda i: (0, 0)
    row = lambda i: (i, 0)
    return pl.pallas_call(
        _ssd_inproj_kernel,
        grid=(nt,),
        in_specs=[
            pl.BlockSpec((tm, d), row),
            pl.BlockSpec((1, 8, d), lambda i: (mod(i), 0, 0)),
            pl.BlockSpec((1, d), const),
            pl.BlockSpec((d, d_inner), const),
            pl.BlockSpec((d, cx), const),
            pl.BlockSpec((d, LANES), const),
        ],
        out_specs=[pl.BlockSpec((tm, d_inner), row), pl.BlockSpec((tm, cx), row), pl.BlockSpec((tm, LANES), row)],
        out_shape=[
            jax.ShapeDtypeStruct((n, d_inner), BF16),
            jax.ShapeDtypeStruct((n, cx), BF16),
            jax.ShapeDtypeStruct((n, LANES), F32),
        ],
        compiler_params=_cp(("parallel",)),
        name="ssd_inproj",
    )(x, mods, g1.reshape(1, d), wz, wx, wd)


def _ssd_conv_kernel(prev_ref, cur_ref, next_ref, w_ref, b_ref, o_ref, *, tiles_per_batch, ctx_tiles):
    j = pl.program_id(0) % tiles_per_batch
    first = jnp.logical_or(j == 0, j == ctx_tiles)
    last = jnp.logical_or(j == ctx_tiles - 1, j == tiles_per_batch - 1)
    half = CONV_HALO // 2
    cur = cur_ref[...].astype(F32)
    tt = cur.shape[0]
    prev = jnp.where(first, 0.0, prev_ref[...].astype(F32)[half:, :])
    nxt = jnp.where(last, 0.0, next_ref[...].astype(F32)[:half, :])
    ext = jnp.concatenate([prev, cur, nxt], axis=0)
    w = w_ref[...]
    pad = (SSD_CONV - 1) // 2
    acc = jnp.broadcast_to(b_ref[...], cur.shape)
    for k in range(SSD_CONV):
        off = half + k - pad
        acc = acc + w[k:k + 1, :] * ext[off:off + tt, :]
    o_ref[...] = (acc * jax.nn.sigmoid(acc)).astype(o_ref.dtype)


def _ssd_conv(xbc, conv_w, conv_b, geo, tt=MOD_TILE, tc=2048):
    n, cx = xbc.shape
    tpb, ct, _ = geo.tiles(tt)
    nt = n // tt
    hb = tt // CONV_HALO
    nhb = n // CONV_HALO
    wpad = jnp.pad(conv_w, ((0, 8 - conv_w.shape[0]), (0, 0))).astype(F32)
    return pl.pallas_call(
        functools.partial(_ssd_conv_kernel, tiles_per_batch=tpb, ctx_tiles=ct),
        grid=(nt, cx // tc),
        in_specs=[
            pl.BlockSpec((CONV_HALO, tc), lambda i, c: (jnp.maximum(i * hb - 1, 0), c)),
            pl.BlockSpec((tt, tc), lambda i, c: (i, c)),
            pl.BlockSpec((CONV_HALO, tc), lambda i, c: (jnp.minimum((i + 1) * hb, nhb - 1), c)),
            pl.BlockSpec((8, tc), lambda i, c: (0, c)),
            pl.BlockSpec((1, tc), lambda i, c: (0, c)),
        ],
        out_specs=pl.BlockSpec((tt, tc), lambda i, c: (i, c)),
        out_shape=jax.ShapeDtypeStruct((n, cx), BF16),
        compiler_params=_cp(("parallel", "parallel")),
        name="ssd_conv",
    )(xbc, xbc, xbc, wpad, conv_b.reshape(1, cx).astype(F32))


def _ssd_dt_kernel(raw_ref, bias_ref, a_ref, dt_ref, cs_ref, cst_ref, *, n_heads):
    q = raw_ref.shape[0]
    dt = jax.nn.softplus(raw_ref[...] + bias_ref[...])
    dta = dt * a_ref[...]
    r = lax.broadcasted_iota(jnp.int32, (q, q), 0)
    c = lax.broadcasted_iota(jnp.int32, (q, q), 1)
    lower = jnp.where(c <= r, 1.0, 0.0)
    upper = jnp.where(c >= r, 1.0, 0.0)
    cs_f = jnp.dot(lower, dta, preferred_element_type=F32, precision=HP)
    cs_b = jnp.dot(upper, dta, preferred_element_type=F32, precision=HP)
    lane = lax.broadcasted_iota(jnp.int32, dt.shape, 1)
    cs = jnp.where(lane < n_heads, cs_f, cs_b)
    dt_ref[...] = dt
    cs_ref[...] = cs
    cst_ref[0] = cs.T


def _ssd_dt(dt_raw, dt_bias, a_log, n_heads):
    n = dt_raw.shape[0]
    q = SSD_Q
    pad = lambda v: jnp.pad(v.reshape(1, -1).astype(F32), ((0, 0), (0, LANES - v.size)))
    return pl.pallas_call(
        functools.partial(_ssd_dt_kernel, n_heads=n_heads),
        grid=(n // q,),
        in_specs=[
            pl.BlockSpec((q, LANES), lambda i: (i, 0)),
            pl.BlockSpec((1, LANES), lambda i: (0, 0)),
            pl.BlockSpec((1, LANES), lambda i: (0, 0)),
        ],
        out_specs=[
            pl.BlockSpec((q, LANES), lambda i: (i, 0)),
            pl.BlockSpec((q, LANES), lambda i: (i, 0)),
            pl.BlockSpec((1, LANES, q), lambda i: (i, 0, 0)),
        ],
        out_shape=[
            jax.ShapeDtypeStruct((n, LANES), F32),
            jax.ShapeDtypeStruct((n, LANES), F32),
            jax.ShapeDtypeStruct((n // q, LANES, q), F32),
        ],
        compiler_params=_cp(("parallel",)),
        name="ssd_dt",
    )(dt_raw, pad(dt_bias), pad(-jnp.exp(a_log.astype(F32))))


def _ssd_scan_kernel(x_ref, b_ref, c_ref, dt_ref, cs_ref, cst_ref, dsk_ref, y_ref, st_ref, *,
                     n_heads, heads_per_group, ctx_chunks, lat_chunks):
    g = pl.program_id(1)
    q = SSD_Q
    hd = SSD_HEAD_DIM
    width = heads_per_group * hd
    dot = functools.partial(jnp.dot, preferred_element_type=F32)
    lane = lax.broadcasted_iota(jnp.int32, (q, LANES), 1)
    blk = lax.broadcasted_iota(jnp.int32, (q, width), 1) // hd
    r_i = lax.broadcasted_iota(jnp.int32, (q, q), 0)
    c_i = lax.broadcasted_iota(jnp.int32, (q, q), 1)

    def spread(cols):
        out = jnp.broadcast_to(cols[0], (q, width))
        for hh in range(1, heads_per_group):
            out = jnp.where(blk == hh, cols[hh], out)
        return out

    def chunk(ci, direction, first_pass):
        r0 = ci * q if isinstance(ci, int) else pl.multiple_of(ci * q, q)
        rows = pl.ds(r0, q)
        xb = x_ref[rows, :].astype(F32)
        bm = b_ref[rows, :]
        cm = c_ref[rows, :]
        dt = dt_ref[rows, :]
        cs = cs_ref[rows, :]
        col0 = direction * n_heads + g * heads_per_group
        dt_cols, cs_cols, cs_rows = [], [], []
        for hh in range(heads_per_group):
            sel = lane == col0 + hh
            dt_cols.append(jnp.sum(jnp.where(sel, dt, 0.0), axis=1, keepdims=True))
            cs_cols.append(jnp.sum(jnp.where(sel, cs, 0.0), axis=1, keepdims=True))
            cs_rows.append(cst_ref[ci, pl.ds(col0 + hh, 1), :])
        xdt = xb * spread(dt_cols)
        csf = spread(cs_cols)
        edge = csf[q - 1:q, :] if direction == 0 else csf[0:1, :]
        scores_b = lax.dot_general(cm, bm, (((1,), (1,)), ((), ())), preferred_element_type=F32).astype(BF16)
        keep = (c_i <= r_i) if direction == 0 else (c_i >= r_i)
        xdt_b = xdt.astype(BF16)
        state = st_ref[...]
        y = dot(cm, state.astype(BF16)) * jnp.exp(csf)
        for hh in range(heads_per_group):
            seg = jnp.where(keep, cs_cols[hh] - cs_rows[hh], -jnp.inf)
            pm = scores_b * jnp.exp(seg).astype(BF16)
            y = y + dot(pm, jnp.where(blk == hh, xdt_b, jnp.zeros_like(xdt_b)))
        w = (xdt * jnp.exp(edge - csf)).astype(BF16)
        bt = bm.astype(F32).T.astype(BF16)
        st_ref[...] = jnp.exp(edge) * state + dot(bt, w)
        if first_pass:
            y_ref[rows, :] = y + dsk_ref[0] * xb
        else:
            y_ref[rows, :] = y_ref[rows, :] + y

    for direction in range(2):
        st_ref[...] = jnp.zeros_like(st_ref)
        for ci in range(ctx_chunks):
            cc = ci if direction == 0 else ctx_chunks - 1 - ci
            chunk(cc, direction, direction == 0)

        def body(i, carry, direction=direction):
            cc = ctx_chunks + (i if direction == 0 else lat_chunks - 1 - i)
            chunk(cc, direction, direction == 0)
            return carry

        lax.fori_loop(0, lat_chunks, body, 0)


def _ssd_scan(xbc, dt, cs, cst, d_skip, geo, *, n_heads, n_groups):
    n = xbc.shape[0]
    rpb = geo.rows_per_batch
    hpg = n_heads // n_groups
    width = hpg * SSD_HEAD_DIM
    d_inner = n_heads * SSD_HEAD_DIM
    xblocks = d_inner // width
    bblocks = d_inner // SSD_STATE
    cpb = rpb // SSD_Q
    dsk = jnp.repeat(d_skip.astype(F32), SSD_HEAD_DIM).reshape(n_groups, 1, width)
    kern = functools.partial(_ssd_scan_kernel, n_heads=n_heads, heads_per_group=hpg,
                             ctx_chunks=geo.n_ctx // SSD_Q, lat_chunks=geo.n_lat // SSD_Q)
    return pl.pallas_call(
        kern,
        grid=(geo.n_batch, n_groups),
        in_specs=[
            pl.BlockSpec((rpb, width), lambda b, g: (b, g)),
            pl.BlockSpec((rpb, SSD_STATE), lambda b, g: (b, bblocks + g)),
            pl.BlockSpec((rpb, SSD_STATE), lambda b, g: (b, bblocks + n_groups + g)),
            pl.BlockSpec((rpb, LANES), lambda b, g: (b, 0)),
            pl.BlockSpec((rpb, LANES), lambda b, g: (b, 0)),
            pl.BlockSpec((cpb, LANES, SSD_Q), lambda b, g: (b, 0, 0)),
            pl.BlockSpec((1, 1, width), lambda b, g: (g, 0, 0)),
        ],
        out_specs=pl.BlockSpec((rpb, width), lambda b, g: (b, g)),
        out_shape=jax.ShapeDtypeStruct((n, d_inner), F32),
        scratch_shapes=[pltpu.VMEM((SSD_STATE, width), F32)],
        compiler_params=_cp(("parallel", "arbitrary")),
        name="ssd_scan",
    )(xbc, xbc, xbc, dt, cs, cst, dsk)


def _ssd_out_kernel(y_ref, z_ref, g_ref, w_ref, o_ref, *, n_groups):
    z = z_ref[...].astype(F32)
    gy = y_ref[...] * (z * jax.nn.sigmoid(z))
    gw = gy.shape[1] // n_groups
    acc = jnp.zeros(o_ref.shape, F32)
    for gi in range(n_groups):
        sl = slice(gi * gw, (gi + 1) * gw)
        ng = (_rms(gy[:, sl]) * g_ref[:, sl]).astype(BF16)
        acc = acc + jnp.dot(ng, w_ref[sl, :], preferred_element_type=F32)
    o_ref[...] = acc


def _ssd_out(y, z, norm_g, out_w, n_groups, tm=ROW_TILE):
    n, di = y.shape
    d = out_w.shape[1]
    return pl.pallas_call(
        functools.partial(_ssd_out_kernel, n_groups=n_groups),
        grid=(n // tm,),
        in_specs=[
            pl.BlockSpec((tm, di), lambda i: (i, 0)),
            pl.BlockSpec((tm, di), lambda i: (i, 0)),
            pl.BlockSpec((1, di), lambda i: (0, 0)),
            pl.BlockSpec((di, d), lambda i: (0, 0)),
        ],
        out_specs=pl.BlockSpec((tm, d), lambda i: (i, 0)),
        out_shape=jax.ShapeDtypeStruct((n, d), F32),
        compiler_params=_cp(("parallel",)),
        name="ssd_out",
    )(y, z, norm_g.reshape(1, di).astype(F32), out_w.astype(BF16))


def _ssd_mixer(x, mods, g1, prm, geo):
    in_w, conv_w, conv_b, dt_bias, a_log, d_skip, norm_g, out_w = prm
    n_heads = a_log.shape[1]
    d_inner = out_w.shape[0]
    n_bc = conv_w.shape[1] - d_inner
    n_groups = n_bc // (2 * SSD_STATE)
    z, xbc_raw, dt_raw = _ssd_inproj(x, mods, g1, in_w, d_inner, n_bc, geo)
    xbc = _ssd_conv(xbc_raw, conv_w, conv_b, geo)
    dt, cs, cst = _ssd_dt(dt_raw, dt_bias, a_log, n_heads)
    y = _ssd_scan(xbc, dt, cs, cst, d_skip, geo, n_heads=n_heads, n_groups=n_groups)
    return _ssd_out(y, z, norm_g, out_w, n_groups)


DA_HEAD_DIM = 64
ROPE_BASE = 10000.0
GRID_W = 64


def _rope_tables(geo, tm):
    hd = DA_HEAD_DIM
    n_freq = hd // 4
    t = jnp.arange(geo.n_lat, dtype=F32)
    row, col = jnp.floor(t / GRID_W), jnp.mod(t, GRID_W)
    inv = ROPE_BASE ** (-jnp.arange(n_freq, dtype=F32) / n_freq)
    lane = jnp.arange(LANES)
    within = lane % hd
    freq = inv[within % n_freq]
    pos = jnp.where((within < hd // 2)[None, :], row[:, None], col[:, None])
    ang = pos * freq[None, :]
    sign = jnp.where((within % (2 * n_freq)) < n_freq, -1.0, 1.0)
    cos = jnp.concatenate([jnp.cos(ang), jnp.ones((tm, LANES), F32)], axis=0)
    sin = jnp.concatenate([jnp.sin(ang) * sign[None, :], jnp.zeros((tm, LANES), F32)], axis=0)
    return cos, sin


def _da_qkv_kernel(x_ref, mod_ref, g_ref, w_ref, cos_ref, sin_ref, qg_ref, kg_ref, seg_ref,
                   q_ref, k_ref, v_ref):
    m = mod_ref[0]
    h = (_rms(x_ref[...]) * g_ref[...] * (1.0 + m[1:2, :]) + m[0:1, :]).astype(BF16)
    width = q_ref.shape[1]
    tm = h.shape[0]
    cos = cos_ref[...]
    sin = sin_ref[...]
    lane = lax.broadcasted_iota(jnp.int32, (tm, LANES), 1)
    n_freq = DA_HEAD_DIM // 4
    first_half = (lane % (2 * n_freq)) < n_freq
    scale = DA_HEAD_DIM ** -0.5 * math.log2(math.e)
    v_ref[...] = jnp.dot(h, w_ref[:, 2 * width:], preferred_element_type=F32).astype(v_ref.dtype)
    for o_ref, off, gain, mult in ((q_ref, 0, qg_ref, scale), (k_ref, width, kg_ref, 1.0)):
        for blk in range(width // LANES):
            cols = slice(off + blk * LANES, off + (blk + 1) * LANES)
            a = jnp.dot(h, w_ref[:, cols], preferred_element_type=F32)
            ms = jnp.dot((a * a).astype(BF16), seg_ref[...], preferred_element_type=F32)
            a = a * lax.rsqrt(ms + EPS) * gain[...]
            partner = jnp.where(first_half, pltpu.roll(a, LANES - n_freq, 1), pltpu.roll(a, n_freq, 1))
            a = a * cos + partner * sin
            o_ref[:, blk * LANES:(blk + 1) * LANES] = (a * mult).astype(o_ref.dtype)


def _da_qkv(x, mods, g1, qkv_w, q_g, k_g, geo, tm=MOD_TILE):
    n, d = x.shape
    width = qkv_w.shape[1] // 3
    nt, _, mod = _tile_maps(geo, tm, False)
    tpb, ct, lt = geo.tiles(tm)
    cos, sin = _rope_tables(geo, tm)
    tab = lambda i: (jnp.where(i % tpb < ct, lt, i % tpb - ct), 0)
    lane = jnp.arange(LANES)
    seg = jnp.where((lane[:, None] // DA_HEAD_DIM) == (lane[None, :] // DA_HEAD_DIM), 1.0 / DA_HEAD_DIM, 0.0)
    tile2 = lambda v: jnp.tile(v.astype(F32), LANES // DA_HEAD_DIM).reshape(1, LANES)
    const = lambda i: (0, 0)
    row = lambda i: (i, 0)
    return pl.pallas_call(
        _da_qkv_kernel,
        grid=(nt,),
        in_specs=[
            pl.BlockSpec((tm, d), row),
            pl.BlockSpec((1, 8, d), lambda i: (mod(i), 0, 0)),
            pl.BlockSpec((1, d), const),
            pl.BlockSpec((d, 3 * width), const),
            pl.BlockSpec((tm, LANES), tab),
            pl.BlockSpec((tm, LANES), tab),
            pl.BlockSpec((1, LANES), const),
            pl.BlockSpec((1, LANES), const),
            pl.BlockSpec((LANES, LANES), const),
        ],
        out_specs=[pl.BlockSpec((tm, width), row)] * 3,
        out_shape=[jax.ShapeDtypeStruct((n, width), BF16)] * 3,
        compiler_params=_cp(("parallel",)),
        name="da_qkv",
    )(x, mods, g1.reshape(1, d), qkv_w.astype(BF16), cos, sin, tile2(q_g), tile2(k_g), seg.astype(BF16))


DA_KEY_CHUNK = 256


def _da_attn_kernel(lam_ref, q_ref, k_ref, v_ref, sg_ref, o_ref, vx_ref, acc_ref, *, ctx_tiles, ctx_chunks,
                    all_chunks, out_scale):
    qi = pl.program_id(2)
    hw = v_ref.shape[1]
    kc = DA_KEY_CHUNK

    @pl.when(qi == 0)
    def _():
        vx_ref[:, :hw] = v_ref[...]
        vx_ref[:, hw:] = jnp.ones((vx_ref.shape[0], hw), vx_ref.dtype)

    q = q_ref[...]
    lane = lax.broadcasted_iota(jnp.int32, q.shape, 1)
    zero = jnp.zeros_like(q)
    qs = (jnp.where(lane < DA_HEAD_DIM, q, zero), jnp.where(lane < DA_HEAD_DIM, zero, q))
    nt = (((1,), (1,)), ((), ()))

    def step(j, ms):
        kb = k_ref[pl.ds(j * kc, kc), :]
        vb = vx_ref[pl.ds(j * kc, kc), :]
        new = []
        for c in range(2):
            s = lax.dot_general(qs[c], kb, nt, preferred_element_type=F32)
            mx = jnp.max(s, axis=-1, keepdims=True)
            mn = mx if ms is None else jnp.maximum(ms[c], mx)
            pv = jnp.dot(jnp.exp2(s - mn).astype(BF16), vb, preferred_element_type=F32)
            acc_ref[c] = pv if ms is None else jnp.exp2(ms[c] - mn) * acc_ref[c] + pv
            new.append(mn)
        return tuple(new)

    ms = None
    for j in range(ctx_chunks):
        ms = step(j, ms)

    @pl.when(qi >= ctx_tiles)
    def _():
        mm = ms
        for j in range(ctx_chunks, all_chunks):
            mm = step(j, mm)

    o = acc_ref[0][:, :hw] / acc_ref[0][:, hw:] - lam_ref[0] * (acc_ref[1][:, :hw] / acc_ref[1][:, hw:])
    o_ref[...] = (_rms(o) * sg_ref[...] * out_scale).astype(o_ref.dtype)


def _da_attn(q, k, v, lam, sub_g, lam_init, geo, tq=MOD_TILE):
    n, width = q.shape
    hw = 2 * DA_HEAD_DIM
    n_heads = width // hw
    rpb = geo.rows_per_batch
    tpb, ct, _ = geo.tiles(tq)
    grid_spec = pltpu.PrefetchScalarGridSpec(
        num_scalar_prefetch=0,
        grid=(geo.n_batch, n_heads, tpb),
        in_specs=[
            pl.BlockSpec(memory_space=pltpu.SMEM),
            pl.BlockSpec((tq, hw), lambda b, h, i: (b * tpb + i, h)),
            pl.BlockSpec((rpb, hw), lambda b, h, i: (b, h)),
            pl.BlockSpec((rpb, hw), lambda b, h, i: (b, h)),
            pl.BlockSpec((1, hw), lambda b, h, i: (0, 0)),
        ],
        out_specs=pl.BlockSpec((tq, hw), lambda b, h, i: (b * tpb + i, h)),
        scratch_shapes=[pltpu.VMEM((rpb, 2 * hw), BF16), pltpu.VMEM((2, tq, 2 * hw), F32)],
    )
    return pl.pallas_call(
        functools.partial(_da_attn_kernel, ctx_tiles=ct, ctx_chunks=geo.n_ctx // DA_KEY_CHUNK,
                          all_chunks=rpb // DA_KEY_CHUNK, out_scale=1.0 - lam_init),
        grid_spec=grid_spec,
        out_shape=jax.ShapeDtypeStruct((n, width), BF16),
        compiler_params=_cp(("parallel", "parallel", "arbitrary")),
        name="da_attn",
    )(lam.reshape(1).astype(F32), q, k, v, sub_g.reshape(1, hw).astype(F32))


def _matmul_kernel(x_ref, w_ref, o_ref):
    o_ref[...] = jnp.dot(x_ref[...], w_ref[...], preferred_element_type=F32)


def _matmul(x, w, tm=ROW_TILE):
    n, kd = x.shape
    d = w.shape[1]
    return pl.pallas_call(
        _matmul_kernel,
        grid=(n // tm,),
        in_specs=[pl.BlockSpec((tm, kd), lambda i: (i, 0)), pl.BlockSpec((kd, d), lambda i: (0, 0))],
        out_specs=pl.BlockSpec((tm, d), lambda i: (i, 0)),
        out_shape=jax.ShapeDtypeStruct((n, d), F32),
        compiler_params=_cp(("parallel",)),
        name="da_out",
    )(x, w)


def _da_mixer(x, mods, g1, prm, layer_idx, geo):
    qkv_w, q_g, k_g, lam_vec, sub_g, out_w = prm
    lam_init = 0.8 - 0.6 * math.exp(-0.3 * layer_idx)
    lv = lam_vec.astype(F32)
    lam = jnp.exp(jnp.sum(lv[0] * lv[1])) - jnp.exp(jnp.sum(lv[2] * lv[3])) + lam_init
    q, k, v = _da_qkv(x, mods, g1, qkv_w, q_g, k_g, geo)
    o = _da_attn(q, k, v, lam, sub_g, lam_init, geo)
    return _matmul(o, out_w.astype(BF16))


def kernel(x, c, ctx, c_ctx, mod_w, mod_b, norm1_g, norm2_g, s5_a_re, s5_a_im, s5_log_dt, s5_b_re, s5_b_im, s5_c_re, s5_c_im, s5_d, s5_glu_w, s5_glu_b, ssd_in_w, ssd_conv_w, ssd_conv_b, ssd_dt_bias, ssd_a_log, ssd_d, ssd_norm_g, ssd_out_w, da_qkv_w, da_q_g, da_k_g, da_lam, da_sub_g, da_out_w, moe_router_w, moe_router_b, moe_gu_w, moe_gu_b, moe_dn_w, moe_dn_b):
    bsz, n_lat, d = x.shape
    n_ctx = ctx.shape[1]
    depth = mod_w.shape[0]
    geo = Geo(bsz, n_ctx, n_lat)
    mods_all = _mods(c, c_ctx, mod_w, mod_b)
    n_mixers = 3
    xs = None
    for i in range(depth):
        last = i == depth - 1
        kind, j = i % n_mixers, i // n_mixers
        mods = mods_all[i]
        if kind == 0:
            prm = (s5_a_re[j], s5_a_im[j], s5_log_dt[j], s5_b_re[j], s5_b_im[j], s5_c_re[j], s5_c_im[j],
                   s5_d[j], s5_glu_w[j], s5_glu_b[j])
            if i == 0:
                h, xs = _prenorm_join(ctx, x, mods, norm1_g[i], geo, F32)
            else:
                h = _prenorm(xs, mods, norm1_g[i], geo, F32)
            y = _s5_mixer(h, prm, geo)
        elif kind == 1:
            prm = (ssd_in_w[j], ssd_conv_w[j], ssd_conv_b[j], ssd_dt_bias[j], ssd_a_log[j], ssd_d[j],
                   ssd_norm_g[j], ssd_out_w[j])
            y = _ssd_mixer(xs, mods, norm1_g[i], prm, geo)
        else:
            prm = (da_qkv_w[j], da_q_g[j], da_k_g[j], da_lam[j], da_sub_g[j], da_out_w[j])
            y = _da_mixer(xs, mods, norm1_g[i], prm, i, geo)
        xs = _moe_layer(xs, y, mods, norm2_g[i], moe_router_w[i], moe_router_b[i], i, moe_gu_w, moe_gu_b[i],
                        moe_dn_w, moe_dn_b[i], geo, last)
    return xs.reshape(bsz, n_lat, d)
```

```python
import functools
import math
from typing import NamedTuple

import jax
import jax.numpy as jnp
from jax import lax
from jax.experimental import pallas as pl
from jax.experimental.pallas import tpu as pltpu

F32 = jnp.float32
BF16 = jnp.bfloat16
EPS = 1e-6
HP = lax.Precision.HIGHEST

LANES = 128
VMEM_LIMIT = 56 * 1024 * 1024

N_EXPERTS = 32
TOP_K = 4
SWIGLU_ALPHA = 1.702
SWIGLU_LIMIT = 7.0
MOE_BLK = 512
MOD_TILE = 256
ROW_TILE = 512
BIG_TILE = 1024


class Geo(NamedTuple):
    n_batch: int
    n_ctx: int
    n_lat: int

    @property
    def rows_per_batch(self):
        return self.n_ctx + self.n_lat

    @property
    def rows(self):
        return self.n_batch * self.rows_per_batch

    def tiles(self, tm):
        return self.rows_per_batch // tm, self.n_ctx // tm, self.n_lat // tm


def _cp(sem, vmem=VMEM_LIMIT):
    return pltpu.CompilerParams(dimension_semantics=sem, vmem_limit_bytes=vmem)


def _tile_maps(geo, tm, lat_only):
    tpb, ct, lt = geo.tiles(tm)
    if lat_only:
        src = lambda i: (i // lt) * tpb + ct + i % lt
        mod = lambda i: i // lt
        return geo.n_batch * lt, src, mod
    mod = lambda i: jnp.where(i % tpb < ct, geo.n_batch, i // tpb)
    return geo.n_batch * tpb, (lambda i: i), mod


def _rms(x):
    return x * lax.rsqrt(jnp.mean(x * x, axis=-1, keepdims=True) + EPS)


def _mods_kernel(s_ref, w_ref, b_ref, o_ref):
    s = s_ref[...]
    s = s * jax.nn.sigmoid(s)
    o_ref[0] = jnp.dot(s, w_ref[0], preferred_element_type=F32, precision=HP) + b_ref[0]


def _mods(c, c_ctx, mod_w, mod_b):
    depth, d, d6 = mod_w.shape
    bsz = c.shape[0]
    rows = -(-(bsz + 1) // 8) * 8
    s = jnp.zeros((rows, d), F32).at[:bsz].set(c).at[bsz].set(c_ctx)
    nj = d6 // d
    out = pl.pallas_call(
        _mods_kernel,
        grid=(depth, nj),
        in_specs=[
            pl.BlockSpec((rows, d), lambda l, j: (0, 0)),
            pl.BlockSpec((1, d, d), lambda l, j: (l, 0, j)),
            pl.BlockSpec((1, 1, d), lambda l, j: (l, 0, j)),
        ],
        out_specs=pl.BlockSpec((1, rows, d), lambda l, j: (l, 0, j)),
        out_shape=jax.ShapeDtypeStruct((depth, rows, d6), F32),
        compiler_params=_cp(("arbitrary", "arbitrary")),
        name="mods",
    )(s, mod_w, mod_b.reshape(depth, 1, d6))
    out = out[:, :bsz + 1].reshape(depth, bsz + 1, nj, d)
    return jnp.pad(out, ((0, 0), (0, 0), (0, 8 - nj), (0, 0)))


def _to_row_tiles(ref, val):
    rows, d = val.shape
    sub = d // LANES
    for s in range(sub):
        ref[pl.ds(s, rows, stride=sub), :] = val[:, s * LANES:(s + 1) * LANES]


def _from_row_tiles(ref, rows, sub, dtype=None):
    parts = [ref[pl.ds(s, rows, stride=sub), :] for s in range(sub)]
    if dtype is not None:
        parts = [p.astype(dtype) for p in parts]
    return jnp.concatenate(parts, axis=1)


def _route_kernel(x_ref, y_ref, mod_ref, g_ref, rwh_ref, rwl_ref, rb_ref, tri_ref,
                  xn_ref, h_ref, route_ref, cnt_ref, base_ref):
    i = pl.program_id(0)

    @pl.when(i == 0)
    def _():
        base_ref[...] = jnp.zeros_like(base_ref)

    m = mod_ref[0]
    x = x_ref[...] + m[2:3, :] * y_ref[...]
    xn_ref[...] = x
    h = _rms(x) * g_ref[...] * (1.0 + m[4:5, :]) + m[3:4, :]
    _to_row_tiles(h_ref, h)
    dot = functools.partial(jnp.dot, preferred_element_type=F32)
    h_hi = h.astype(BF16)
    h_lo = (h - h_hi.astype(F32)).astype(BF16)
    logits = dot(h_hi, rwh_ref[...]) + dot(h_lo, rwh_ref[...]) + dot(h_hi, rwl_ref[...]) + rb_ref[...]
    tm = logits.shape[0]
    lane = lax.broadcasted_iota(jnp.int32, (tm, LANES), 1).astype(F32)
    neg = jnp.float32(-jnp.inf)
    work = jnp.where(lane < N_EXPERTS, logits, neg)
    vals, ohs, idxs = [], [], []
    for _ in range(TOP_K):
        mk = jnp.max(work, axis=-1, keepdims=True)
        ik = jnp.min(jnp.where(work == mk, lane, float(LANES)), axis=-1, keepdims=True)
        oh = lane == ik
        work = jnp.where(oh, neg, work)
        vals.append(mk)
        ohs.append(oh)
        idxs.append(ik)
    es = [jnp.exp(v - vals[0]) for v in vals]
    den = es[0] + es[1] + es[2] + es[3]
    onehot = jnp.zeros((tm, LANES), F32)
    for oh in ohs:
        onehot = onehot + jnp.where(oh, 1.0, 0.0)
    prefix = jnp.dot(tri_ref[...], onehot.astype(BF16), preferred_element_type=F32)
    pos = prefix + base_ref[0:1, :]
    out = jnp.zeros((tm, LANES), F32)
    for k in range(TOP_K):
        rank = jnp.sum(jnp.where(ohs[k], pos, 0.0), axis=-1, keepdims=True)
        out = out + jnp.where(lane == k, idxs[k], 0.0)
        out = out + jnp.where(lane == TOP_K + k, rank, 0.0)
        out = out + jnp.where(lane == 2 * TOP_K + k, es[k] / den, 0.0)
    route_ref[...] = out
    newbase = base_ref[0:1, :] + jnp.sum(onehot, axis=0, keepdims=True)
    base_ref[...] = jnp.broadcast_to(newbase, base_ref.shape)
    cnt_ref[...] = jnp.broadcast_to(newbase, cnt_ref.shape)


def _route(x, y, mods, g2, rw, rb, geo, lat_only, tm=MOD_TILE):
    d = x.shape[1]
    sub = d // LANES
    nt, src, mod = _tile_maps(geo, tm, lat_only)
    n = nt * tm
    rw_p = jnp.zeros((d, LANES), F32).at[:, :N_EXPERTS].set(rw)
    rw_hi = rw_p.astype(BF16)
    rw_lo = (rw_p - rw_hi.astype(F32)).astype(BF16)
    rb_p = jnp.zeros((1, LANES), F32).at[0, :N_EXPERTS].set(rb)
    tri = (jnp.arange(tm)[:, None] > jnp.arange(tm)[None, :]).astype(BF16)
    const = lambda i: (0, 0)
    return pl.pallas_call(
        _route_kernel,
        grid=(nt,),
        in_specs=[
            pl.BlockSpec((tm, d), lambda i: (src(i), 0)),
            pl.BlockSpec((tm, d), lambda i: (src(i), 0)),
            pl.BlockSpec((1, 8, d), lambda i: (mod(i), 0, 0)),
            pl.BlockSpec((1, d), const),
            pl.BlockSpec((d, LANES), const),
            pl.BlockSpec((d, LANES), const),
            pl.BlockSpec((1, LANES), const),
            pl.BlockSpec((tm, tm), const),
        ],
        out_specs=[
            pl.BlockSpec((tm, d), lambda i: (i, 0)),
            pl.BlockSpec((tm * sub, LANES), lambda i: (i, 0)),
            pl.BlockSpec((tm, LANES), lambda i: (i, 0)),
            pl.BlockSpec((8, LANES), const),
        ],
        out_shape=[
            jax.ShapeDtypeStruct((n, d), F32),
            jax.ShapeDtypeStruct((n * sub, LANES), F32),
            jax.ShapeDtypeStruct((n, LANES), F32),
            jax.ShapeDtypeStruct((8, LANES), F32),
        ],
        scratch_shapes=[pltpu.VMEM((8, LANES), F32)],
        compiler_params=_cp(("arbitrary",)),
        name="route",
    )(x, y, mods, g2.reshape(1, d), rw_hi, rw_lo, rb_p, tri)


def _dispatch_kernel(last_ref, dest_ref, h_ref, xs_ref, zero_ref, sem, zsem, *, sub, blk):
    i = pl.program_id(0)
    tm = h_ref.shape[0] // sub

    @pl.when(i == 0)
    def _():
        zero_ref[...] = jnp.zeros_like(zero_ref)

        def zcopy(e):
            return pltpu.make_async_copy(zero_ref, xs_ref.at[pl.ds(last_ref[e] * (blk * sub), blk * sub), :], zsem)

        def zstart(e, c):
            @pl.when(last_ref[e] >= 0)
            def _():
                zcopy(e).start()
            return c

        def zwait(e, c):
            @pl.when(last_ref[e] >= 0)
            def _():
                zcopy(e).wait()
            return c

        lax.fori_loop(0, N_EXPERTS, zstart, 0)
        lax.fori_loop(0, N_EXPERTS, zwait, 0)

    tok_per_row = LANES // TOP_K

    def start(row, c):
        for ln in range(LANES):
            src0 = pl.multiple_of(row * (tok_per_row * sub) + (ln // TOP_K) * sub, sub)
            dst0 = pl.multiple_of(dest_ref[row, ln] * sub, sub)
            pltpu.make_async_copy(h_ref.at[pl.ds(src0, sub), :], xs_ref.at[pl.ds(dst0, sub), :], sem).start(
                priority=ln % 2)
        return c

    lax.fori_loop(0, tm // tok_per_row, start, 0)
    for k in range(TOP_K):
        pltpu.make_async_copy(h_ref, xs_ref.at[pl.ds(0, tm * sub), :], sem).wait()


def _dispatch(h, dest2d, last_blk, n_slots, sub, tm=BIG_TILE):
    n = h.shape[0] // sub
    nt = n // tm
    db = tm * TOP_K // LANES
    grid_spec = pltpu.PrefetchScalarGridSpec(
        num_scalar_prefetch=1,
        grid=(nt,),
        in_specs=[
            pl.BlockSpec((db, LANES), lambda i, lb: (i, 0), memory_space=pltpu.SMEM),
            pl.BlockSpec((tm * sub, LANES), lambda i, lb: (i, 0)),
        ],
        out_specs=pl.BlockSpec(memory_space=pl.ANY),
        scratch_shapes=[pltpu.VMEM((MOE_BLK * sub, LANES), F32), pltpu.SemaphoreType.DMA(()),
                        pltpu.SemaphoreType.DMA(())],
    )
    return pl.pallas_call(
        functools.partial(_dispatch_kernel, sub=sub, blk=MOE_BLK),
        grid_spec=grid_spec,
        out_shape=jax.ShapeDtypeStruct((n_slots * sub, LANES), F32),
        compiler_params=_cp(("arbitrary",)),
        name="dispatch",
    )(last_blk, dest2d, h)


def _expert_kernel(be_ref, na_ref, xs_ref, guw_ref, gub_ref, dnw_ref, dnb_ref, ys_ref, guw_bf, dnw_bf):
    j = pl.program_id(0)
    active = j < na_ref[0]

    @pl.when(jnp.logical_and(active, jnp.logical_or(j == 0, be_ref[j] != be_ref[jnp.maximum(j - 1, 0)])))
    def _():
        guw_bf[...] = guw_ref[0, 0].astype(BF16)
        dnw_bf[...] = dnw_ref[0, 0].astype(BF16)

    @pl.when(active)
    def _():
        f, d = dnw_bf.shape
        sub = d // LANES
        x = _from_row_tiles(xs_ref, xs_ref.shape[0] // sub, sub, BF16)
        gu = jnp.dot(x, guw_bf[...], preferred_element_type=F32) + gub_ref[0]
        gate = jnp.minimum(gu[:, :f], SWIGLU_LIMIT)
        up = jnp.clip(gu[:, f:], -SWIGLU_LIMIT, SWIGLU_LIMIT)
        act = (up + 1.0) * gate * jax.nn.sigmoid(SWIGLU_ALPHA * gate)
        y = jnp.dot(act.astype(BF16), dnw_bf[...], preferred_element_type=F32) + dnb_ref[0]
        _to_row_tiles(ys_ref, y)


def _experts(xs, block_e, n_active, layer, guw, gub, dnw, dnb, blk=MOE_BLK):
    _, e, d, f2 = guw.shape
    f = f2 // 2
    sub = d // LANES
    nb = xs.shape[0] // (blk * sub)
    act_blk = lambda j, be, na: (jnp.minimum(j, na[0] - 1), 0)
    wsel = lambda j, be, na: (layer, be[j], 0, 0)
    bsel = lambda j, be, na: (be[j], 0, 0)
    grid_spec = pltpu.PrefetchScalarGridSpec(
        num_scalar_prefetch=2,
        grid=(nb,),
        in_specs=[
            pl.BlockSpec((blk * sub, LANES), act_blk),
            pl.BlockSpec((1, 1, d, f2), wsel),
            pl.BlockSpec((1, 1, f2), bsel),
            pl.BlockSpec((1, 1, f, d), wsel),
            pl.BlockSpec((1, 1, d), bsel),
        ],
        out_specs=pl.BlockSpec((blk * sub, LANES), act_blk),
        scratch_shapes=[pltpu.VMEM((d, f2), BF16), pltpu.VMEM((f, d), BF16)],
    )
    return pl.pallas_call(
        _expert_kernel,
        grid_spec=grid_spec,
        out_shape=jax.ShapeDtypeStruct(xs.shape, F32),
        compiler_params=_cp(("arbitrary",)),
        name="experts",
    )(block_e, n_active, xs, guw, gub.reshape(e, 1, f2), dnw, dnb.reshape(e, 1, d))


def _combine_kernel(dest_ref, next_ref, x_ref, route_ref, mod_ref, ys_ref, o_ref, buf, sems):
    i = pl.program_id(0)
    nt = pl.num_programs(0)
    tm, d = x_ref.shape
    sub = d // LANES
    tok_per_row = LANES // TOP_K

    def gather(idx_ref, slot):
        def start(row, c):
            for ln in range(LANES):
                src0 = pl.multiple_of(idx_ref[row, ln] * sub, sub)
                dst0 = pl.multiple_of(row * (tok_per_row * sub) + (ln // TOP_K) * sub, sub)
                pltpu.make_async_copy(ys_ref.at[pl.ds(src0, sub), :], buf.at[slot, ln % TOP_K, pl.ds(dst0, sub), :],
                                      sems.at[slot]).start(priority=ln % 2)
            return c

        lax.fori_loop(0, tm // tok_per_row, start, 0)

    @pl.when(i == 0)
    def _():
        gather(dest_ref, 0)

    @pl.when(i + 1 < nt)
    def _():
        gather(next_ref, (i + 1) % 2)

    slot = i % 2
    for k in range(TOP_K):
        pltpu.make_async_copy(ys_ref.at[pl.ds(0, tm * sub), :], buf.at[slot, k], sems.at[slot]).wait()
    route = route_ref[...]
    gates = [route[:, 2 * TOP_K + k:2 * TOP_K + k + 1] for k in range(TOP_K)]
    scale = mod_ref[0][5:6, :]
    for s in range(sub):
        cols = slice(s * LANES, (s + 1) * LANES)
        f = gates[0] * buf[slot, 0, pl.ds(s, tm, stride=sub), :]
        for k in range(1, TOP_K):
            f = f + gates[k] * buf[slot, k, pl.ds(s, tm, stride=sub), :]
        o_ref[:, cols] = x_ref[:, cols] + scale[:, cols] * f


def _combine(x, route, mods, ys, dest2d, geo, lat_only, tm=MOD_TILE):
    n, d = x.shape
    sub = d // LANES
    nt, _, mod = _tile_maps(geo, tm, lat_only)
    db = tm * TOP_K // LANES
    return pl.pallas_call(
        _combine_kernel,
        grid=(nt,),
        in_specs=[
            pl.BlockSpec((db, LANES), lambda i: (i, 0), memory_space=pltpu.SMEM),
            pl.BlockSpec((db, LANES), lambda i: (jnp.minimum(i + 1, nt - 1), 0), memory_space=pltpu.SMEM),
            pl.BlockSpec((tm, d), lambda i: (i, 0)),
            pl.BlockSpec((tm, LANES), lambda i: (i, 0)),
            pl.BlockSpec((1, 8, d), lambda i: (mod(i), 0, 0)),
            pl.BlockSpec(memory_space=pl.ANY),
        ],
        out_specs=pl.BlockSpec((tm, d), lambda i: (i, 0)),
        out_shape=jax.ShapeDtypeStruct((n, d), F32),
        scratch_shapes=[pltpu.VMEM((2, TOP_K, tm * sub, LANES), F32), pltpu.SemaphoreType.DMA((2,))],
        compiler_params=_cp(("arbitrary",)),
        name="combine",
    )(dest2d, dest2d, x, route, mods, ys)


def _moe_layer(x, y, mods, g2, rw, rb, layer, guw, gub, dnw, dnb, geo, lat_only):
    xn, h, route, cnt = _route(x, y, mods, g2, rw, rb, geo, lat_only)
    n, d = xn.shape
    sub = d // LANES
    counts = cnt[0, :N_EXPERTS].astype(jnp.int32)
    nblk = (counts + MOE_BLK - 1) // MOE_BLK
    blk_end = jnp.cumsum(nblk)
    pad_start = (blk_end - nblk) * MOE_BLK
    n_blocks = -(-(n * TOP_K) // MOE_BLK) + N_EXPERTS
    block_e = jnp.minimum(jnp.sum(blk_end[None, :] <= jnp.arange(n_blocks)[:, None], axis=1), N_EXPERTS - 1)
    n_active = blk_end[-1:].astype(jnp.int32)
    last_blk = jnp.where(nblk > 0, blk_end - 1, -1).astype(jnp.int32)
    eid = route[:, :TOP_K].astype(jnp.int32)
    rank = route[:, TOP_K:2 * TOP_K].astype(jnp.int32)
    dest = (pad_start[eid] + rank).astype(jnp.int32).reshape(n * TOP_K // LANES, LANES)
    xs = _dispatch(h, dest, last_blk, n_blocks * MOE_BLK, sub)
    ys = _experts(xs, block_e.astype(jnp.int32), n_active, layer, guw, gub, dnw, dnb)
    return _combine(xn, route, mods, ys, dest, geo, lat_only)


def _prenorm_kernel(x_ref, mod_ref, g_ref, h_ref):
    m = mod_ref[0]
    h = _rms(x_ref[...]) * g_ref[...] * (1.0 + m[1:2, :]) + m[0:1, :]
    h_ref[...] = h.astype(h_ref.dtype)


def _prenorm(x, mods, g1, geo, dtype, tm=MOD_TILE):
    n, d = x.shape
    nt, _, mod = _tile_maps(geo, tm, False)
    return pl.pallas_call(
        _prenorm_kernel,
        grid=(nt,),
        in_specs=[
            pl.BlockSpec((tm, d), lambda i: (i, 0)),
            pl.BlockSpec((1, 8, d), lambda i: (mod(i), 0, 0)),
            pl.BlockSpec((1, d), lambda i: (0, 0)),
        ],
        out_specs=pl.BlockSpec((tm, d), lambda i: (i, 0)),
        out_shape=jax.ShapeDtypeStruct((n, d), dtype),
        compiler_params=_cp(("parallel",)),
        name="prenorm",
    )(x, mods, g1.reshape(1, d))


def _prenorm_join_kernel(c_ref, x_ref, mod_ref, g_ref, h_ref, xs_ref, *, tiles_per_batch, ctx_tiles):
    is_ctx = pl.program_id(0) % tiles_per_batch < ctx_tiles
    x = jnp.where(is_ctx, c_ref[0], x_ref[0])
    xs_ref[...] = x
    m = mod_ref[0]
    h_ref[...] = (_rms(x) * g_ref[...] * (1.0 + m[1:2, :]) + m[0:1, :]).astype(h_ref.dtype)


def _prenorm_join(ctx, x, mods, g1, geo, dtype, tm=MOD_TILE):
    d = x.shape[-1]
    nt, _, mod = _tile_maps(geo, tm, False)
    tpb, ct, _ = geo.tiles(tm)
    row = lambda i: (i, 0)
    return pl.pallas_call(
        functools.partial(_prenorm_join_kernel, tiles_per_batch=tpb, ctx_tiles=ct),
        grid=(nt,),
        in_specs=[
            pl.BlockSpec((1, tm, d), lambda i: (i // tpb, jnp.minimum(i % tpb, ct - 1), 0)),
            pl.BlockSpec((1, tm, d), lambda i: (i // tpb, jnp.maximum(i % tpb - ct, 0), 0)),
            pl.BlockSpec((1, 8, d), lambda i: (mod(i), 0, 0)),
            pl.BlockSpec((1, d), lambda i: (0, 0)),
        ],
        out_specs=[pl.BlockSpec((tm, d), row), pl.BlockSpec((tm, d), row)],
        out_shape=[jax.ShapeDtypeStruct((geo.rows, d), dtype), jax.ShapeDtypeStruct((geo.rows, d), F32)],
        compiler_params=_cp(("parallel",)),
        name="prenorm_join",
    )(ctx, x, mods, g1.reshape(1, d))


S5_CHUNK = 16
S5_GB = 8


def _s5_matrices(a_re, a_im, log_dt, b_re, b_im, c_re, c_im):
    L = S5_CHUNK
    a_re = a_re.astype(F32)
    a_im = a_im.astype(F32)
    dt = jnp.exp(log_dt.astype(F32))[..., None]
    mag = jnp.exp(dt * a_re)
    ang = dt * a_im
    ab_re, ab_im = mag * jnp.cos(ang), mag * jnp.sin(ang)
    den = a_re * a_re + a_im * a_im
    num_re = ab_re - 1.0
    co_re = (num_re * a_re + ab_im * a_im) / den
    co_im = (ab_im * a_re - num_re * a_im) / den
    bb_re = co_re[..., None] * b_re - co_im[..., None] * b_im
    bb_im = co_re[..., None] * b_im + co_im[..., None] * b_re
    taus = jnp.arange(L + 1, dtype=F32)[:, None, None, None]
    pmag = jnp.exp(taus * dt * a_re)
    pang = taus * dt * a_im
    pw_re, pw_im = pmag * jnp.cos(pang), pmag * jnp.sin(pang)
    cp_re = c_re[None] * pw_re[:, :, :, None, :] - c_im[None] * pw_im[:, :, :, None, :]
    cp_im = c_re[None] * pw_im[:, :, :, None, :] + c_im[None] * pw_re[:, :, :, None, :]
    kk = (jnp.einsum('tdgjp,dgpi->tdgji', cp_re, bb_re, precision=HP)
          - jnp.einsum('tdgjp,dgpi->tdgji', cp_im, bb_im, precision=HP))
    t_idx = jnp.arange(L)
    lag = t_idx[None, :] - t_idx[:, None]

    def toeplitz(kd, lagm):
        kt = kd[jnp.clip(lagm, 0, L)]
        kt = jnp.where((lagm >= 0)[:, :, None, None, None], kt, 0.0)
        return kt.transpose(2, 0, 4, 1, 3)

    g = a_re.shape[1]
    p = a_re.shape[2]
    j = b_re.shape[3]
    m = (toeplitz(kk[:, 0], lag) + toeplitz(kk[:, 1], -lag)).reshape(g, L * j, L * j)

    def inject(d, steps):
        pr, pi = pw_re[steps, d], pw_im[steps, d]
        re = pr[:, :, :, None] * bb_re[d][None] - pi[:, :, :, None] * bb_im[d][None]
        im = pr[:, :, :, None] * bb_im[d][None] + pi[:, :, :, None] * bb_re[d][None]
        f = lambda z: z.transpose(1, 0, 3, 2).reshape(g, L * j, p)
        return f(re), f(im)

    inj = inject(0, L - 1 - t_idx) + inject(1, t_idx)

    def readout(d, steps):
        re = cp_re[steps, d]
        im = cp_im[steps, d]
        f = lambda z: z.transpose(1, 3, 0, 2).reshape(g, p, L * j)
        return f(re), f(-im)

    rd = readout(0, t_idx + 1) + readout(1, L - t_idx)

    gb = S5_GB
    nblk = g // gb

    def spread(z, width):
        z = z.astype(BF16)
        lanes = jnp.arange(gb * width)
        mask = (jnp.arange(gb)[:, None] == lanes[None, :] // width).astype(BF16)
        mask = mask.reshape((1, gb) + (1,) * (z.ndim - 3) + (gb * width,))
        rep = (jnp.arange(width)[:, None] == lanes[None, :] % width).astype(BF16)
        tiled = lax.dot_general(z, rep, (((z.ndim - 1,), (0,)), ((), ())), preferred_element_type=BF16)
        return tiled * mask

    m_blk = spread(m.reshape(nblk, gb, L, j, L, j), j).transpose(0, 2, 1, 3, 4, 5).reshape(nblk, L * gb * j, L * gb * j)

    def inj_blk(re, im):
        f = lambda z: spread(z.reshape(nblk, gb, L, j, p), p).transpose(0, 2, 1, 3, 4).reshape(nblk, L * gb * j, gb * p)
        return jnp.concatenate([f(re), f(im)], axis=2)

    def rd_blk(re, im):
        f = lambda z: spread(z.reshape(nblk, gb, p, L, j), j).reshape(nblk, gb * p, L * gb * j)
        return jnp.concatenate([f(re), f(im)], axis=1)

    def adv_blk(d):
        return jnp.concatenate([pw_re[L, d].reshape(nblk, 1, gb * p), pw_im[L, d].reshape(nblk, 1, gb * p)], axis=2)

    adv = jnp.pad(jnp.concatenate([adv_blk(0), adv_blk(1)], axis=1), ((0, 0), (0, 6), (0, 0)))
    return (m_blk, inj_blk(inj[0], inj[1]), inj_blk(inj[2], inj[3]), rd_blk(rd[0], rd[1]), rd_blk(rd[2], rd[3]), adv)


def _s5_chunk_vectors(x_ref):
    nb, rows, _ = x_ref.shape
    c = rows // S5_CHUNK
    parts = [x_ref[:, pl.ds(l, c, stride=S5_CHUNK), :].reshape(nb * c, LANES).astype(BF16) for l in range(S5_CHUNK)]
    return jnp.concatenate(parts, axis=1)


def _s5_put_states(sloc_ref, val):
    for k in range(sloc_ref.shape[0]):
        sloc_ref[k] = val[:, k * LANES:(k + 1) * LANES]


def _s5_get_states(sloc_ref):
    return jnp.concatenate([sloc_ref[k] for k in range(sloc_ref.shape[0])], axis=1)


def _s5_chunk_scan(sloc_ref, carry_ref, adv, nb, c, reverse):
    nk = sloc_ref.shape[0]
    half = nk // 2
    a = [jnp.broadcast_to(adv[:, k * LANES:(k + 1) * LANES], (nb, LANES)) for k in range(nk)]

    def step(i, st):
        ci = (c - 1 - i) if reverse else i
        rows = pl.ds(ci, nb, stride=c)
        new_re, new_im = [], []
        for k in range(half):
            l_re, l_im = sloc_ref[k, rows, :], sloc_ref[half + k, rows, :]
            s_re, s_im = st[k], st[half + k]
            sloc_ref[k, rows, :] = s_re
            sloc_ref[half + k, rows, :] = s_im
            new_re.append(a[k] * s_re - a[half + k] * s_im + l_re)
            new_im.append(a[k] * s_im + a[half + k] * s_re + l_im)
        return tuple(new_re + new_im)

    st = lax.fori_loop(0, c, step, tuple(carry_ref[k] for k in range(nk)))
    for k in range(nk):
        carry_ref[k] = st[k]


def _s5_bwd_kernel(x_ref, inj_ref, adv_ref, sn_ref, sloc_ref, carry_ref):
    @pl.when(pl.program_id(1) == 0)
    def _():
        carry_ref[...] = jnp.zeros_like(carry_ref)

    nb = x_ref.shape[0]
    c = x_ref.shape[1] // S5_CHUNK
    z = _s5_chunk_vectors(x_ref)
    _s5_put_states(sloc_ref, jnp.dot(z, inj_ref[0], preferred_element_type=F32))
    _s5_chunk_scan(sloc_ref, carry_ref, adv_ref[0][1:2, :], nb, c, True)
    sn_ref[0, 0] = _s5_get_states(sloc_ref).astype(sn_ref.dtype)


def _s5_fwd_kernel(x_ref, sn_ref, m_ref, inj_ref, rdf_ref, rdb_ref, adv_ref, y_ref, sloc_ref, carry_ref):
    @pl.when(pl.program_id(1) == 0)
    def _():
        carry_ref[...] = jnp.zeros_like(carry_ref)

    nb = x_ref.shape[0]
    c = x_ref.shape[1] // S5_CHUNK
    dot = functools.partial(jnp.dot, preferred_element_type=F32)
    z = _s5_chunk_vectors(x_ref)
    _s5_put_states(sloc_ref, dot(z, inj_ref[0]))
    _s5_chunk_scan(sloc_ref, carry_ref, adv_ref[0][0:1, :], nb, c, False)
    y = dot(z, m_ref[0]) + dot(_s5_get_states(sloc_ref).astype(BF16), rdf_ref[0]) + dot(sn_ref[0, 0], rdb_ref[0])
    for t in range(S5_CHUNK):
        y_ref[:, pl.ds(t, c, stride=S5_CHUNK), :] = y[:, t * LANES:(t + 1) * LANES].reshape(nb, c, LANES)


def _s5_core(h, mats, geo, tile=MOD_TILE):
    m, inj_f, inj_b, rd_f, rd_b, adv = mats
    nblk, kdim, sdim = inj_f.shape
    nb, rpb = geo.n_batch, geo.rows_per_batch
    d = h.shape[1]
    tpb, ct, _ = geo.tiles(tile)
    c = tile // S5_CHUNK
    h3 = h.reshape(nb, rpb, d)
    bwd_tile = lambda s: jnp.where(s < ct, ct - 1 - s, tpb - 1 - (s - ct))
    once = pl.Buffered(1)
    wspec = lambda shape: pl.BlockSpec((1,) + shape, lambda g, s: (g, 0, 0), pipeline_mode=once)
    scratch = [pltpu.VMEM((sdim // LANES, nb * c, LANES), F32), pltpu.VMEM((sdim // LANES, nb, LANES), F32)]
    sn = pl.pallas_call(
        _s5_bwd_kernel,
        grid=(nblk, tpb),
        in_specs=[
            pl.BlockSpec((nb, tile, LANES), lambda g, s: (0, bwd_tile(s), g)),
            wspec((kdim, sdim)),
            wspec((8, sdim)),
        ],
        out_specs=pl.BlockSpec((1, 1, nb * c, sdim), lambda g, s: (g, bwd_tile(s), 0, 0)),
        out_shape=jax.ShapeDtypeStruct((nblk, tpb, nb * c, sdim), BF16),
        scratch_shapes=scratch,
        compiler_params=_cp(("parallel", "arbitrary")),
        name="s5_bwd",
    )(h3, inj_b, adv)
    y = pl.pallas_call(
        _s5_fwd_kernel,
        grid=(nblk, tpb),
        in_specs=[
            pl.BlockSpec((nb, tile, LANES), lambda g, s: (0, s, g)),
            pl.BlockSpec((1, 1, nb * c, sdim), lambda g, s: (g, s, 0, 0)),
            wspec((kdim, kdim)),
            wspec((kdim, sdim)),
            wspec((sdim, kdim)),
            wspec((sdim, kdim)),
            wspec((8, sdim)),
        ],
        out_specs=pl.BlockSpec((nb, tile, LANES), lambda g, s: (0, s, g)),
        out_shape=jax.ShapeDtypeStruct((nb, rpb, d), F32),
        scratch_shapes=scratch,
        compiler_params=_cp(("parallel", "arbitrary")),
        name="s5_fwd",
    )(h3, sn, m, inj_f, rd_f, rd_b, adv)
    return y.reshape(nb * rpb, d)


def _gelu_tanh(x):
    return 0.5 * x * (1.0 + jnp.tanh(math.sqrt(2.0 / math.pi) * (x + 0.044715 * (x * x * x))))


def _s5_glu_kernel(h_ref, y_ref, d_ref, w_ref, b_ref, o_ref):
    y = d_ref[...] * h_ref[...].astype(F32) + y_ref[...].astype(F32)
    gl = _gelu_tanh(y).astype(BF16)
    z = jnp.dot(gl, w_ref[...], preferred_element_type=F32) + b_ref[...]
    dd = o_ref.shape[1]
    o_ref[...] = z[:, :dd] * jax.nn.sigmoid(z[:, dd:])


def _s5_glu(h, y, d_skip, glu_w, glu_b, tm=BIG_TILE):
    n, d = h.shape
    return pl.pallas_call(
        _s5_glu_kernel,
        grid=(n // tm,),
        in_specs=[
            pl.BlockSpec((tm, d), lambda i: (i, 0)),
            pl.BlockSpec((tm, d), lambda i: (i, 0)),
            pl.BlockSpec((1, d), lambda i: (0, 0)),
            pl.BlockSpec((d, 2 * d), lambda i: (0, 0)),
            pl.BlockSpec((1, 2 * d), lambda i: (0, 0)),
        ],
        out_specs=pl.BlockSpec((tm, d), lambda i: (i, 0)),
        out_shape=jax.ShapeDtypeStruct((n, d), F32),
        compiler_params=_cp(("parallel",)),
        name="s5_glu",
    )(h, y, d_skip.reshape(1, d).astype(F32), glu_w.astype(BF16), glu_b.reshape(1, 2 * d).astype(F32))


def _s5_mixer(h, prm, geo):
    a_re, a_im, log_dt, b_re, b_im, c_re, c_im, d_skip, glu_w, glu_b = prm
    mats = _s5_matrices(a_re, a_im, log_dt, b_re, b_im, c_re, c_im)
    y = _s5_core(h, mats, geo)
    return _s5_glu(h, y, d_skip, glu_w, glu_b)


SSD_Q = 256
SSD_HEAD_DIM = 64
SSD_STATE = 128
SSD_CONV = 5
CONV_HALO = 16


def _ssd_inproj_kernel(x_ref, mod_ref, g_ref, wz_ref, wx_ref, wd_ref, z_ref, xbc_ref, dt_ref):
    m = mod_ref[0]
    h = (_rms(x_ref[...]) * g_ref[...] * (1.0 + m[1:2, :]) + m[0:1, :]).astype(BF16)
    z_ref[...] = jnp.dot(h, wz_ref[...], preferred_element_type=F32).astype(z_ref.dtype)
    xbc_ref[...] = jnp.dot(h, wx_ref[...], preferred_element_type=F32).astype(xbc_ref.dtype)
    dt_ref[...] = jnp.dot(h, wd_ref[...], preferred_element_type=F32)


def _ssd_inproj(x, mods, g1, in_w, d_inner, n_bc, geo, tm=MOD_TILE):
    n, d = x.shape
    nt, _, mod = _tile_maps(geo, tm, False)
    wz = in_w[:, :d_inner].astype(BF16)
    wx = in_w[:, d_inner:2 * d_inner + n_bc].astype(BF16)
    wd = in_w[:, 2 * d_inner + n_bc:]
    wd = jnp.pad(wd, ((0, 0), (0, LANES - wd.shape[1]))).astype(BF16)
    cx = d_inner + n_bc
    const = lambda i: (0, 0)
    row = lambda i: (i, 0)
    return pl.pallas_call(
        _ssd_inproj_kernel,
        grid=(nt,),
        in_specs=[
            pl.BlockSpec((tm, d), row),
            pl.BlockSpec((1, 8, d), lambda i: (mod(i), 0, 0)),
            pl.BlockSpec((1, d), const),
            pl.BlockSpec((d, d_inner), const),
            pl.BlockSpec((d, cx), const),
            pl.BlockSpec((d, LANES), const),
        ],
        out_specs=[pl.BlockSpec((tm, d_inner), row), pl.BlockSpec((tm, cx), row), pl.BlockSpec((tm, LANES), row)],
        out_shape=[
            jax.ShapeDtypeStruct((n, d_inner), BF16),
            jax.ShapeDtypeStruct((n, cx), BF16),
            jax.ShapeDtypeStruct((n, LANES), F32),
        ],
        compiler_params=_cp(("parallel",)),
        name="ssd_inproj",
    )(x, mods, g1.reshape(1, d), wz, wx, wd)


def _ssd_conv_kernel(prev_ref, cur_ref, next_ref, w_ref, b_ref, o_ref, *, tiles_per_batch, ctx_tiles):
    j = pl.program_id(0) % tiles_per_batch
    first = jnp.logical_or(j == 0, j == ctx_tiles)
    last = jnp.logical_or(j == ctx_tiles - 1, j == tiles_per_batch - 1)
    half = CONV_HALO // 2
    cur = cur_ref[...].astype(F32)
    tt = cur.shape[0]
    prev = jnp.where(first, 0.0, prev_ref[...].astype(F32)[half:, :])
    nxt = jnp.where(last, 0.0, next_ref[...].astype(F32)[:half, :])
    ext = jnp.concatenate([prev, cur, nxt], axis=0)
    w = w_ref[...]
    pad = (SSD_CONV - 1) // 2
    acc = jnp.broadcast_to(b_ref[...], cur.shape)
    for k in range(SSD_CONV):
        off = half + k - pad
        acc = acc + w[k:k + 1, :] * ext[off:off + tt, :]
    o_ref[...] = (acc * jax.nn.sigmoid(acc)).astype(o_ref.dtype)


def _ssd_conv(xbc, conv_w, conv_b, geo, tt=MOD_TILE, tc=4096):
    n, cx = xbc.shape
    tpb, ct, _ = geo.tiles(tt)
    nt = n // tt
    hb = tt // CONV_HALO
    nhb = n // CONV_HALO
    wpad = jnp.pad(conv_w, ((0, 8 - conv_w.shape[0]), (0, 0))).astype(F32)
    return pl.pallas_call(
        functools.partial(_ssd_conv_kernel, tiles_per_batch=tpb, ctx_tiles=ct),
        grid=(nt, cx // tc),
        in_specs=[
            pl.BlockSpec((CONV_HALO, tc), lambda i, c: (jnp.maximum(i * hb - 1, 0), c)),
            pl.BlockSpec((tt, tc), lambda i, c: (i, c)),
            pl.BlockSpec((CONV_HALO, tc), lambda i, c: (jnp.minimum((i + 1) * hb, nhb - 1), c)),
            pl.BlockSpec((8, tc), lambda i, c: (0, c)),
            pl.BlockSpec((1, tc), lambda i, c: (0, c)),
        ],
        out_specs=pl.BlockSpec((tt, tc), lambda i, c: (i, c)),
        out_shape=jax.ShapeDtypeStruct((n, cx), BF16),
        compiler_params=_cp(("parallel", "parallel")),
        name="ssd_conv",
    )(xbc, xbc, xbc, wpad, conv_b.reshape(1, cx).astype(F32))


def _ssd_dt_kernel(raw_ref, bias_ref, a_ref, dt_ref, cs_ref, cst_ref, *, n_heads):
    q = raw_ref.shape[0]
    dt = jax.nn.softplus(raw_ref[...] + bias_ref[...])
    dta = dt * a_ref[...]
    r = lax.broadcasted_iota(jnp.int32, (q, q), 0)
    c = lax.broadcasted_iota(jnp.int32, (q, q), 1)
    lower = jnp.where(c <= r, 1.0, 0.0)
    upper = jnp.where(c >= r, 1.0, 0.0)
    cs_f = jnp.dot(lower, dta, preferred_element_type=F32, precision=HP)
    cs_b = jnp.dot(upper, dta, preferred_element_type=F32, precision=HP)
    lane = lax.broadcasted_iota(jnp.int32, dt.shape, 1)
    cs = jnp.where(lane < n_heads, cs_f, cs_b)
    dt_ref[...] = dt
    cs_ref[...] = cs
    cst_ref[0] = cs.T


def _ssd_dt(dt_raw, dt_bias, a_log, n_heads):
    n = dt_raw.shape[0]
    q = SSD_Q
    pad = lambda v: jnp.pad(v.reshape(1, -1).astype(F32), ((0, 0), (0, LANES - v.size)))
    return pl.pallas_call(
        functools.partial(_ssd_dt_kernel, n_heads=n_heads),
        grid=(n // q,),
        in_specs=[
            pl.BlockSpec((q, LANES), lambda i: (i, 0)),
            pl.BlockSpec((1, LANES), lambda i: (0, 0)),
            pl.BlockSpec((1, LANES), lambda i: (0, 0)),
        ],
        out_specs=[
            pl.BlockSpec((q, LANES), lambda i: (i, 0)),
            pl.BlockSpec((q, LANES), lambda i: (i, 0)),
            pl.BlockSpec((1, LANES, q), lambda i: (i, 0, 0)),
        ],
        out_shape=[
            jax.ShapeDtypeStruct((n, LANES), F32),
            jax.ShapeDtypeStruct((n, LANES), F32),
            jax.ShapeDtypeStruct((n // q, LANES, q), F32),
        ],
        compiler_params=_cp(("parallel",)),
        name="ssd_dt",
    )(dt_raw, pad(dt_bias), pad(-jnp.exp(a_log.astype(F32))))


def _ssd_scan_kernel(x_ref, b_ref, c_ref, dt_ref, cs_ref, cst_ref, dsk_ref, y_ref, st_ref, *,
                     n_heads, heads_per_group, ctx_chunks, lat_chunks):
    g = pl.program_id(1)
    q = SSD_Q
    hd = SSD_HEAD_DIM
    width = heads_per_group * hd
    dot = functools.partial(jnp.dot, preferred_element_type=F32)
    lane = lax.broadcasted_iota(jnp.int32, (q, LANES), 1)
    blk = lax.broadcasted_iota(jnp.int32, (q, width), 1) // hd
    r_i = lax.broadcasted_iota(jnp.int32, (q, q), 0)
    c_i = lax.broadcasted_iota(jnp.int32, (q, q), 1)

    def spread(cols):
        out = jnp.broadcast_to(cols[0], (q, width))
        for hh in range(1, heads_per_group):
            out = jnp.where(blk == hh, cols[hh], out)
        return out

    def chunk(ci, direction, first_pass):
        r0 = ci * q if isinstance(ci, int) else pl.multiple_of(ci * q, q)
        rows = pl.ds(r0, q)
        xb = x_ref[rows, :].astype(F32)
        bm = b_ref[rows, :]
        cm = c_ref[rows, :]
        dt = dt_ref[rows, :]
        cs = cs_ref[rows, :]
        col0 = direction * n_heads + g * heads_per_group
        dt_cols, cs_cols, cs_rows = [], [], []
        for hh in range(heads_per_group):
            sel = lane == col0 + hh
            dt_cols.append(jnp.sum(jnp.where(sel, dt, 0.0), axis=1, keepdims=True))
            cs_cols.append(jnp.sum(jnp.where(sel, cs, 0.0), axis=1, keepdims=True))
            cs_rows.append(cst_ref[ci, pl.ds(col0 + hh, 1), :])
        xdt = xb * spread(dt_cols)
        csf = spread(cs_cols)
        edge = csf[q - 1:q, :] if direction == 0 else csf[0:1, :]
        scores_b = lax.dot_general(cm, bm, (((1,), (1,)), ((), ())), preferred_element_type=F32).astype(BF16)
        keep = (c_i <= r_i) if direction == 0 else (c_i >= r_i)
        xdt_b = xdt.astype(BF16)
        state = st_ref[...]
        y = dot(cm, state.astype(BF16)) * jnp.exp(csf)
        for hh in range(heads_per_group):
            seg = jnp.where(keep, cs_cols[hh] - cs_rows[hh], -jnp.inf)
            pm = scores_b * jnp.exp(seg).astype(BF16)
            y = y + dot(pm, jnp.where(blk == hh, xdt_b, jnp.zeros_like(xdt_b)))
        w = (xdt * jnp.exp(edge - csf)).astype(BF16)
        bt = bm.astype(F32).T.astype(BF16)
        st_ref[...] = jnp.exp(edge) * state + dot(bt, w)
        if first_pass:
            y_ref[rows, :] = y + dsk_ref[0] * xb
        else:
            y_ref[rows, :] = y_ref[rows, :] + y

    for direction in range(2):
        st_ref[...] = jnp.zeros_like(st_ref)
        for ci in range(ctx_chunks):
            cc = ci if direction == 0 else ctx_chunks - 1 - ci
            chunk(cc, direction, direction == 0)

        def body(i, carry, direction=direction):
            cc = ctx_chunks + (i if direction == 0 else lat_chunks - 1 - i)
            chunk(cc, direction, direction == 0)
            return carry

        lax.fori_loop(0, lat_chunks, body, 0)


def _ssd_scan(xbc, dt, cs, cst, d_skip, geo, *, n_heads, n_groups):
    n = xbc.shape[0]
    rpb = geo.rows_per_batch
    hpg = n_heads // n_groups
    width = hpg * SSD_HEAD_DIM
    d_inner = n_heads * SSD_HEAD_DIM
    xblocks = d_inner // width
    bblocks = d_inner // SSD_STATE
    cpb = rpb // SSD_Q
    dsk = jnp.repeat(d_skip.astype(F32), SSD_HEAD_DIM).reshape(n_groups, 1, width)
    kern = functools.partial(_ssd_scan_kernel, n_heads=n_heads, heads_per_group=hpg,
                             ctx_chunks=geo.n_ctx // SSD_Q, lat_chunks=geo.n_lat // SSD_Q)
    return pl.pallas_call(
        kern,
        grid=(geo.n_batch, n_groups),
        in_specs=[
            pl.BlockSpec((rpb, width), lambda b, g: (b, g)),
            pl.BlockSpec((rpb, SSD_STATE), lambda b, g: (b, bblocks + g)),
            pl.BlockSpec((rpb, SSD_STATE), lambda b, g: (b, bblocks + n_groups + g)),
            pl.BlockSpec((rpb, LANES), lambda b, g: (b, 0)),
            pl.BlockSpec((rpb, LANES), lambda b, g: (b, 0)),
            pl.BlockSpec((cpb, LANES, SSD_Q), lambda b, g: (b, 0, 0)),
            pl.BlockSpec((1, 1, width), lambda b, g: (g, 0, 0)),
        ],
        out_specs=pl.BlockSpec((rpb, width), lambda b, g: (b, g)),
        out_shape=jax.ShapeDtypeStruct((n, d_inner), F32),
        scratch_shapes=[pltpu.VMEM((SSD_STATE, width), F32)],
        compiler_params=_cp(("parallel", "arbitrary")),
        name="ssd_scan",
    )(xbc, xbc, xbc, dt, cs, cst, dsk)


def _ssd_out_kernel(y_ref, z_ref, g_ref, w_ref, o_ref, *, n_groups):
    z = z_ref[...].astype(F32)
    gy = y_ref[...] * (z * jax.nn.sigmoid(z))
    gw = gy.shape[1] // n_groups
    acc = jnp.zeros(o_ref.shape, F32)
    for gi in range(n_groups):
        sl = slice(gi * gw, (gi + 1) * gw)
        ng = (_rms(gy[:, sl]) * g_ref[:, sl]).astype(BF16)
        acc = acc + jnp.dot(ng, w_ref[sl, :], preferred_element_type=F32)
    o_ref[...] = acc


def _ssd_out(y, z, norm_g, out_w, n_groups, tm=ROW_TILE):
    n, di = y.shape
    d = out_w.shape[1]
    return pl.pallas_call(
        functools.partial(_ssd_out_kernel, n_groups=n_groups),
        grid=(n // tm,),
        in_specs=[
            pl.BlockSpec((tm, di), lambda i: (i, 0)),
            pl.BlockSpec((tm, di), lambda i: (i, 0)),
            pl.BlockSpec((1, di), lambda i: (0, 0)),
            pl.BlockSpec((di, d), lambda i: (0, 0)),
        ],
        out_specs=pl.BlockSpec((tm, d), lambda i: (i, 0)),
        out_shape=jax.ShapeDtypeStruct((n, d), F32),
        compiler_params=_cp(("parallel",)),
        name="ssd_out",
    )(y, z, norm_g.reshape(1, di).astype(F32), out_w.astype(BF16))


def _ssd_mixer(x, mods, g1, prm, geo):
    in_w, conv_w, conv_b, dt_bias, a_log, d_skip, norm_g, out_w = prm
    n_heads = a_log.shape[1]
    d_inner = out_w.shape[0]
    n_bc = conv_w.shape[1] - d_inner
    n_groups = n_bc // (2 * SSD_STATE)
    z, xbc_raw, dt_raw = _ssd_inproj(x, mods, g1, in_w, d_inner, n_bc, geo)
    xbc = _ssd_conv(xbc_raw, conv_w, conv_b, geo)
    dt, cs, cst = _ssd_dt(dt_raw, dt_bias, a_log, n_heads)
    y = _ssd_scan(xbc, dt, cs, cst, d_skip, geo, n_heads=n_heads, n_groups=n_groups)
    return _ssd_out(y, z, norm_g, out_w, n_groups)


DA_HEAD_DIM = 64
ROPE_BASE = 10000.0
GRID_W = 64


def _rope_tables(geo, tm):
    hd = DA_HEAD_DIM
    n_freq = hd // 4
    t = jnp.arange(geo.n_lat, dtype=F32)
    row, col = jnp.floor(t / GRID_W), jnp.mod(t, GRID_W)
    inv = ROPE_BASE ** (-jnp.arange(n_freq, dtype=F32) / n_freq)
    lane = jnp.arange(LANES)
    within = lane % hd
    freq = inv[within % n_freq]
    pos = jnp.where((within < hd // 2)[None, :], row[:, None], col[:, None])
    ang = pos * freq[None, :]
    sign = jnp.where((within % (2 * n_freq)) < n_freq, -1.0, 1.0)
    cos = jnp.concatenate([jnp.cos(ang), jnp.ones((tm, LANES), F32)], axis=0)
    sin = jnp.concatenate([jnp.sin(ang) * sign[None, :], jnp.zeros((tm, LANES), F32)], axis=0)
    return cos, sin


def _da_qkv_kernel(x_ref, mod_ref, g_ref, w_ref, cos_ref, sin_ref, qg_ref, kg_ref, seg_ref,
                   q_ref, k_ref, v_ref):
    m = mod_ref[0]
    h = (_rms(x_ref[...]) * g_ref[...] * (1.0 + m[1:2, :]) + m[0:1, :]).astype(BF16)
    width = q_ref.shape[1]
    tm = h.shape[0]
    cos = cos_ref[...]
    sin = sin_ref[...]
    lane = lax.broadcasted_iota(jnp.int32, (tm, LANES), 1)
    n_freq = DA_HEAD_DIM // 4
    first_half = (lane % (2 * n_freq)) < n_freq
    scale = DA_HEAD_DIM ** -0.5 * math.log2(math.e)
    v_ref[...] = jnp.dot(h, w_ref[:, 2 * width:], preferred_element_type=F32).astype(v_ref.dtype)
    for o_ref, off, gain, mult in ((q_ref, 0, qg_ref, scale), (k_ref, width, kg_ref, 1.0)):
        for blk in range(width // LANES):
            cols = slice(off + blk * LANES, off + (blk + 1) * LANES)
            a = jnp.dot(h, w_ref[:, cols], preferred_element_type=F32)
            ms = jnp.dot((a * a).astype(BF16), seg_ref[...], preferred_element_type=F32)
            a = a * lax.rsqrt(ms + EPS) * gain[...]
            partner = jnp.where(first_half, pltpu.roll(a, LANES - n_freq, 1), pltpu.roll(a, n_freq, 1))
            a = a * cos + partner * sin
            o_ref[:, blk * LANES:(blk + 1) * LANES] = (a * mult).astype(o_ref.dtype)


def _da_qkv(x, mods, g1, qkv_w, q_g, k_g, geo, tm=MOD_TILE):
    n, d = x.shape
    width = qkv_w.shape[1] // 3
    nt, _, mod = _tile_maps(geo, tm, False)
    tpb, ct, lt = geo.tiles(tm)
    cos, sin = _rope_tables(geo, tm)
    tab = lambda i: (jnp.where(i % tpb < ct, lt, i % tpb - ct), 0)
    lane = jnp.arange(LANES)
    seg = jnp.where((lane[:, None] // DA_HEAD_DIM) == (lane[None, :] // DA_HEAD_DIM), 1.0 / DA_HEAD_DIM, 0.0)
    tile2 = lambda v: jnp.tile(v.astype(F32), LANES // DA_HEAD_DIM).reshape(1, LANES)
    const = lambda i: (0, 0)
    row = lambda i: (i, 0)
    return pl.pallas_call(
        _da_qkv_kernel,
        grid=(nt,),
        in_specs=[
            pl.BlockSpec((tm, d), row),
            pl.BlockSpec((1, 8, d), lambda i: (mod(i), 0, 0)),
            pl.BlockSpec((1, d), const),
            pl.BlockSpec((d, 3 * width), const),
            pl.BlockSpec((tm, LANES), tab),
            pl.BlockSpec((tm, LANES), tab),
            pl.BlockSpec((1, LANES), const),
            pl.BlockSpec((1, LANES), const),
            pl.BlockSpec((LANES, LANES), const),
        ],
        out_specs=[pl.BlockSpec((tm, width), row)] * 3,
        out_shape=[jax.ShapeDtypeStruct((n, width), BF16)] * 3,
        compiler_params=_cp(("parallel",)),
        name="da_qkv",
    )(x, mods, g1.reshape(1, d), qkv_w.astype(BF16), cos, sin, tile2(q_g), tile2(k_g), seg.astype(BF16))


DA_KEY_CHUNK = 256


def _da_attn_kernel(lam_ref, q_ref, k_ref, v_ref, sg_ref, o_ref, vx_ref, acc_ref, *, ctx_tiles, ctx_chunks,
                    all_chunks, out_scale):
    qi = pl.program_id(2)
    hw = v_ref.shape[1]
    kc = DA_KEY_CHUNK

    @pl.when(qi == 0)
    def _():
        vx_ref[:, :hw] = v_ref[...]
        vx_ref[:, hw:] = jnp.ones((vx_ref.shape[0], hw), vx_ref.dtype)

    q = q_ref[...]
    lane = lax.broadcasted_iota(jnp.int32, q.shape, 1)
    zero = jnp.zeros_like(q)
    qs = (jnp.where(lane < DA_HEAD_DIM, q, zero), jnp.where(lane < DA_HEAD_DIM, zero, q))
    nt = (((1,), (1,)), ((), ()))

    def step(j, ms):
        kb = k_ref[pl.ds(j * kc, kc), :]
        vb = vx_ref[pl.ds(j * kc, kc), :]
        new = []
        for c in range(2):
            s = lax.dot_general(qs[c], kb, nt, preferred_element_type=F32)
            mx = jnp.max(s, axis=-1, keepdims=True)
            mn = mx if ms is None else jnp.maximum(ms[c], mx)
            pv = jnp.dot(jnp.exp2(s - mn).astype(BF16), vb, preferred_element_type=F32)
            acc_ref[c] = pv if ms is None else jnp.exp2(ms[c] - mn) * acc_ref[c] + pv
            new.append(mn)
        return tuple(new)

    ms = None
    for j in range(ctx_chunks):
        ms = step(j, ms)

    @pl.when(qi >= ctx_tiles)
    def _():
        mm = ms
        for j in range(ctx_chunks, all_chunks):
            mm = step(j, mm)

    o = acc_ref[0][:, :hw] / acc_ref[0][:, hw:] - lam_ref[0] * (acc_ref[1][:, :hw] / acc_ref[1][:, hw:])
    o_ref[...] = (_rms(o) * sg_ref[...] * out_scale).astype(o_ref.dtype)


def _da_attn(q, k, v, lam, sub_g, lam_init, geo, tq=MOD_TILE):
    n, width = q.shape
    hw = 2 * DA_HEAD_DIM
    n_heads = width // hw
    rpb = geo.rows_per_batch
    tpb, ct, _ = geo.tiles(tq)
    grid_spec = pltpu.PrefetchScalarGridSpec(
        num_scalar_prefetch=0,
        grid=(geo.n_batch, n_heads, tpb),
        in_specs=[
            pl.BlockSpec(memory_space=pltpu.SMEM),
            pl.BlockSpec((tq, hw), lambda b, h, i: (b * tpb + i, h)),
            pl.BlockSpec((rpb, hw), lambda b, h, i: (b, h)),
            pl.BlockSpec((rpb, hw), lambda b, h, i: (b, h)),
            pl.BlockSpec((1, hw), lambda b, h, i: (0, 0)),
        ],
        out_specs=pl.BlockSpec((tq, hw), lambda b, h, i: (b * tpb + i, h)),
        scratch_shapes=[pltpu.VMEM((rpb, 2 * hw), BF16), pltpu.VMEM((2, tq, 2 * hw), F32)],
    )
    return pl.pallas_call(
        functools.partial(_da_attn_kernel, ctx_tiles=ct, ctx_chunks=geo.n_ctx // DA_KEY_CHUNK,
                          all_chunks=rpb // DA_KEY_CHUNK, out_scale=1.0 - lam_init),
        grid_spec=grid_spec,
        out_shape=jax.ShapeDtypeStruct((n, width), BF16),
        compiler_params=_cp(("parallel", "parallel", "arbitrary")),
        name="da_attn",
    )(lam.reshape(1).astype(F32), q, k, v, sub_g.reshape(1, hw).astype(F32))


def _matmul_kernel(x_ref, w_ref, o_ref):
    o_ref[...] = jnp.dot(x_ref[...], w_ref[...], preferred_element_type=F32)


def _matmul(x, w, tm=BIG_TILE):
    n, kd = x.shape
    d = w.shape[1]
    return pl.pallas_call(
        _matmul_kernel,
        grid=(n // tm,),
        in_specs=[pl.BlockSpec((tm, kd), lambda i: (i, 0)), pl.BlockSpec((kd, d), lambda i: (0, 0))],
        out_specs=pl.BlockSpec((tm, d), lambda i: (i, 0)),
        out_shape=jax.ShapeDtypeStruct((n, d), F32),
        compiler_params=_cp(("parallel",)),
        name="da_out",
    )(x, w)


def _da_mixer(x, mods, g1, prm, layer_idx, geo):
    qkv_w, q_g, k_g, lam_vec, sub_g, out_w = prm
    lam_init = 0.8 - 0.6 * math.exp(-0.3 * layer_idx)
    lv = lam_vec.astype(F32)
    lam = jnp.exp(jnp.sum(lv[0] * lv[1])) - jnp.exp(jnp.sum(lv[2] * lv[3])) + lam_init
    q, k, v = _da_qkv(x, mods, g1, qkv_w, q_g, k_g, geo)
    o = _da_attn(q, k, v, lam, sub_g, lam_init, geo)
    return _matmul(o, out_w.astype(BF16))


def kernel(x, c, ctx, c_ctx, mod_w, mod_b, norm1_g, norm2_g, s5_a_re, s5_a_im, s5_log_dt, s5_b_re, s5_b_im, s5_c_re, s5_c_im, s5_d, s5_glu_w, s5_glu_b, ssd_in_w, ssd_conv_w, ssd_conv_b, ssd_dt_bias, ssd_a_log, ssd_d, ssd_norm_g, ssd_out_w, da_qkv_w, da_q_g, da_k_g, da_lam, da_sub_g, da_out_w, moe_router_w, moe_router_b, moe_gu_w, moe_gu_b, moe_dn_w, moe_dn_b):
    bsz, n_lat, d = x.shape
    n_ctx = ctx.shape[1]
    depth = mod_w.shape[0]
    geo = Geo(bsz, n_ctx, n_lat)
    mods_all = _mods(c, c_ctx, mod_w, mod_b)
    n_mixers = 3
    xs = None
    for i in range(depth):
        last = i == depth - 1
        kind, j = i % n_mixers, i // n_mixers
        mods = mods_all[i]
        if kind == 0:
            prm = (s5_a_re[j], s5_a_im[j], s5_log_dt[j], s5_b_re[j], s5_b_im[j], s5_c_re[j], s5_c_im[j],
                   s5_d[j], s5_glu_w[j], s5_glu_b[j])
            if i == 0:
                h, xs = _prenorm_join(ctx, x, mods, norm1_g[i], geo, F32)
            else:
                h = _prenorm(xs, mods, norm1_g[i], geo, F32)
            y = _s5_mixer(h, prm, geo)
        elif kind == 1:
            prm = (ssd_in_w[j], ssd_conv_w[j], ssd_conv_b[j], ssd_dt_bias[j], ssd_a_log[j], ssd_d[j],
                   ssd_norm_g[j], ssd_out_w[j])
            y = _ssd_mixer(xs, mods, norm1_g[i], prm, geo)
        else:
            prm = (da_qkv_w[j], da_q_g[j], da_k_g[j], da_lam[j], da_sub_g[j], da_out_w[j])
            y = _da_mixer(xs, mods, norm1_g[i], prm, i, geo)
        xs = _moe_layer(xs, y, mods, norm2_g[i], moe_router_w[i], moe_router_b[i], i, moe_gu_w, moe_gu_b[i],
                        moe_dn_w, moe_dn_b[i], geo, last)
    return xs.reshape(bsz, n_lat, d)
```
